```python
import math
import jax, jax.numpy as jnp
from jax import lax
import numpy as np

D_MODEL = 1024
BATCH = 1
SEQ = 16384
DEPTH = 1

PLE_DIM = 256
SSM_WIDTH = D_MODEL // 2
SSM_GROUP = 16
SSM_GROUPS = SSM_WIDTH // SSM_GROUP
SSM_STATE = 64
CONV_WIDTH = D_MODEL
CONV_K = 3
DT_MIN = 1e-3
DT_MAX = 1e-1
RMS_EPS = 1e-6
IN_SPLITS = (SSM_WIDTH, SSM_WIDTH, CONV_WIDTH, CONV_WIDTH, CONV_WIDTH, CONV_WIDTH, D_MODEL, D_MODEL)
IN_WIDTH = sum(IN_SPLITS)

kernel_name = "hybrid_s5_shortconv_gated_block"


def rms_norm(x, gain):
    xf = x.astype(jnp.float32)
    y = xf * lax.rsqrt(jnp.mean(xf * xf, axis=-1, keepdims=True) + RMS_EPS)
    return (y * gain.astype(jnp.float32)).astype(x.dtype)


def _complex_affine_combine(e1, e2):
    a1r, a1i, b1r, b1i = e1
    a2r, a2i, b2r, b2i = e2
    ar = a1r * a2r - a1i * a2i
    ai = a1r * a2i + a1i * a2r
    br = a2r * b1r - a2i * b1i + b2r
    bi = a2r * b1i + a2i * b1r + b2i
    return (ar, ai, br, bi)


def s5_ssm(u, a_re, a_im, log_step, b_re, b_im, c_re, c_im, d_skip):
    f32 = jnp.float32
    bsz, seqlen, _ = u.shape
    uf = u.astype(f32).reshape(bsz, seqlen, SSM_GROUPS, SSM_GROUP)
    a_re = a_re.astype(f32); a_im = a_im.astype(f32)
    dt = jnp.exp(log_step.astype(f32))[:, None]
    mag = jnp.exp(a_re * dt)
    abar_r = mag * jnp.cos(a_im * dt)
    abar_i = mag * jnp.sin(a_im * dt)
    den = a_re * a_re + a_im * a_im
    nr = abar_r - 1.0
    fr = (nr * a_re + abar_i * a_im) / den
    fi = (abar_i * a_re - nr * a_im) / den
    b_re = b_re.astype(f32); b_im = b_im.astype(f32)
    bbar_r = fr[..., None] * b_re - fi[..., None] * b_im
    bbar_i = fr[..., None] * b_im + fi[..., None] * b_re
    bu_r = jnp.einsum('gph,blgh->blgp', bbar_r, uf)
    bu_i = jnp.einsum('gph,blgh->blgp', bbar_i, uf)
    ar = jnp.broadcast_to(abar_r, bu_r.shape)
    ai = jnp.broadcast_to(abar_i, bu_i.shape)
    _, _, s_r, s_i = lax.associative_scan(_complex_affine_combine, (ar, ai, bu_r, bu_i), axis=1)
    c_re = c_re.astype(f32); c_im = c_im.astype(f32)
    y = (jnp.einsum('ghp,blgp->blgh', c_re, s_r)
         - jnp.einsum('ghp,blgp->blgh', c_im, s_i)
         + d_skip.astype(f32).reshape(SSM_GROUPS, SSM_GROUP) * uf)
    return y.reshape(bsz, seqlen, SSM_WIDTH).astype(u.dtype)


def causal_dwconv(v, w):
    c = v.shape[-1]
    return lax.conv_general_dilated(
        v, w.astype(v.dtype)[:, None, :], window_strides=(1,), padding=[(CONV_K - 1, 0)],
        dimension_numbers=('NWC', 'WIO', 'NWC'), feature_group_count=c)


def setup_inputs(seed: int = 0) -> dict:
    key = jax.random.key(seed)
    ks = jax.random.split(key, 32)
    f32 = jnp.float32
    nrm = lambda k, s, sc: jax.random.normal(k, s, f32) * sc
    gain = lambda k, s: 1.0 + 0.02 * jax.random.normal(k, s, f32)
    G, P, H = SSM_GROUPS, SSM_STATE, SSM_GROUP
    n_idx = jnp.arange(P, dtype=f32)[None, :]
    return {
        "x": nrm(ks[0], (BATCH, SEQ, D_MODEL), 1.0),
        "p": nrm(ks[1], (DEPTH, BATCH, SEQ, PLE_DIM), 1.0),
        "pre_norm": gain(ks[2], (DEPTH, D_MODEL)),
        "w_in": nrm(ks[3], (DEPTH, D_MODEL, IN_WIDTH), D_MODEL ** -0.5),
        "a_re": -0.5 + 0.01 * jax.random.normal(ks[4], (DEPTH, G, P), f32),
        "a_im": math.pi * n_idx + 0.01 * jax.random.normal(ks[5], (DEPTH, G, P), f32),
        "log_step": jax.random.uniform(ks[6], (DEPTH, G), f32, math.log(DT_MIN), math.log(DT_MAX)),
        "b_re": nrm(ks[7], (DEPTH, G, P, H), (2.0 * H) ** -0.5),
        "b_im": nrm(ks[8], (DEPTH, G, P, H), (2.0 * H) ** -0.5),
        "c_re": nrm(ks[9], (DEPTH, G, H, P), (2.0 * P) ** -0.5),
        "c_im": nrm(ks[10], (DEPTH, G, H, P), (2.0 * P) ** -0.5),
        "d_skip": nrm(ks[11], (DEPTH, SSM_WIDTH), 1.0),
        "w_glu": nrm(ks[12], (DEPTH, SSM_WIDTH, SSM_WIDTH), SSM_WIDTH ** -0.5),
        "b_glu": nrm(ks[13], (DEPTH, SSM_WIDTH), 0.01),
        "w_ssm_out": nrm(ks[14], (DEPTH, SSM_WIDTH, D_MODEL), SSM_WIDTH ** -0.5),
        "conv_w": nrm(ks[15], (DEPTH, CONV_K, CONV_WIDTH), CONV_K ** -0.5),
        "w_conv_out": nrm(ks[16], (DEPTH, CONV_WIDTH, D_MODEL), CONV_WIDTH ** -0.5),
        "w_o": nrm(ks[17], (DEPTH, D_MODEL, D_MODEL), D_MODEL ** -0.5),
        "post_norm": gain(ks[18], (DEPTH, D_MODEL)),
        "w_ple": nrm(ks[19], (DEPTH, PLE_DIM, D_MODEL), PLE_DIM ** -0.5),
        "w_ple_gate": nrm(ks[20], (DEPTH, D_MODEL, D_MODEL), D_MODEL ** -0.5),
        "ple_norm": gain(ks[21], (DEPTH, D_MODEL)),
    }


def reference(x, p, pre_norm, w_in, a_re, a_im, log_step, b_re, b_im, c_re, c_im, d_skip,
              w_glu, b_glu, w_ssm_out, conv_w, w_conv_out, w_o, post_norm,
              w_ple, w_ple_gate, ple_norm):
    split_pts = list(np.cumsum(IN_SPLITS)[:-1])
    for i in range(DEPTH):
        h = rms_norm(x, pre_norm[i])
        proj = jnp.einsum('bld,de->ble', h, w_in[i])
        u_ssm, g_ssm, xc, bc, cc, g_conv, r_ssm, r_conv = jnp.split(proj, split_pts, axis=-1)

        y = s5_ssm(u_ssm, a_re[i], a_im[i], log_step[i], b_re[i], b_im[i], c_re[i], c_im[i], d_skip[i])
        y = jax.nn.gelu(y)
        y = y * jax.nn.sigmoid(jnp.einsum('blc,ce->ble', y, w_glu[i]) + b_glu[i])
        y = y * jax.nn.silu(g_ssm)
        y_a = jnp.einsum('blc,cd->bld', y, w_ssm_out[i])

        z = bc * causal_dwconv(cc * xc, conv_w[i])
        z = z * jax.nn.silu(g_conv)
        y_b = jnp.einsum('blc,cd->bld', z, w_conv_out[i])

        merged = jax.nn.sigmoid(r_ssm) * y_a + jax.nn.sigmoid(r_conv) * y_b
        o = jnp.einsum('bld,de->ble', merged, w_o[i])
        x = x + rms_norm(o, post_norm[i])

        e = jnp.einsum('blk,kd->bld', p[i], w_ple[i]) * jax.nn.sigmoid(
            jnp.einsum('bld,de->ble', x, w_ple_gate[i]))
        x = x + rms_norm(e, ple_norm[i])
    return x
```

```python
import functools
import math

import jax
import jax.numpy as jnp
from jax import lax
from jax.experimental import pallas as pl
from jax.experimental.pallas import tpu as pltpu

D_MODEL = 1024
PLE_DIM = 256
SSM_WIDTH = 512
SSM_GROUP = 16
SSM_GROUPS = 32
SSM_STATE = 64
CONV_WIDTH = 1024
CONV_K = 3
RMS_EPS = 1e-6
IN_WIDTH = 7168

LANES = 128
SUBLANES = 8
GROUPS_PER_TILE = LANES // SSM_GROUP
N_TILES = SSM_WIDTH // LANES
HALF = GROUPS_PER_TILE * SSM_STATE
TILE_STATE = 2 * HALF
ROW_BLOCK = 256
VMEM_LIMIT_BYTES = 56 * 1024 * 1024

O_U, O_GS, O_XC, O_BC, O_CC, O_GC, O_RS, O_RC = 0, 512, 1024, 2048, 3072, 4096, 5120, 6144


def _rms(v, gain):
    return v * lax.rsqrt(jnp.mean(v * v, axis=-1, keepdims=True) + RMS_EPS) * gain


def _sigmoid(v):
    return 1.0 / (1.0 + jnp.exp(-v))


def _gelu_tanh(v):
    c = math.sqrt(2.0 / math.pi)
    return v * (0.5 * (1.0 + jnp.tanh(c * (v + 0.044715 * (v * v * v)))))


def _bdot(a, b):
    return jnp.dot(a.astype(jnp.bfloat16), b, preferred_element_type=jnp.float32)


def _block_kernel(x_ref, p_ref, pre_ref, win_ref, wb_ref, wc_ref, coef_ref, dskip_ref,
                  wglu_ref, bglu_ref, wso_ref, convw_ref, wco_ref, wo_ref, post_ref,
                  wple_ref, wpg_ref, plen_ref, out_ref,
                  state_ref, vtail_ref, bu_ref, y_ref, v_ref):
    tb = x_ref.shape[0]

    @pl.when(pl.program_id(0) == 0)
    def _():
        state_ref[...] = jnp.zeros_like(state_ref)
        vtail_ref[...] = jnp.zeros_like(vtail_ref)

    x = x_ref[...]
    h = _rms(x, pre_ref[...]).astype(jnp.bfloat16)

    def proj(off, width):
        return jnp.dot(h, win_ref[:, off:off + width], preferred_element_type=jnp.float32)

    u = proj(O_U, SSM_WIDTH)
    u_bf = u.astype(jnp.bfloat16)
    for j in range(N_TILES):
        bu_ref[...] = jnp.dot(u_bf[:, j * LANES:(j + 1) * LANES], wb_ref[j],
                              preferred_element_type=jnp.float32)
        lo = j * HALF
        c1r = coef_ref[0, :, lo:lo + HALF]; c1i = coef_ref[1, :, lo:lo + HALF]
        c2r = coef_ref[2, :, lo:lo + HALF]; c2i = coef_ref[3, :, lo:lo + HALF]
        c4r = coef_ref[4, :, lo:lo + HALF]; c4i = coef_ref[5, :, lo:lo + HALF]
        pwr = coef_ref[6, :, lo:lo + HALF]; pwi = coef_ref[7, :, lo:lo + HALF]

        def slab(i, carry):
            cr, ci = carry
            r0 = pl.multiple_of(i * SUBLANES, SUBLANES)
            br = bu_ref[pl.ds(r0, SUBLANES), 0:HALF]
            bi = bu_ref[pl.ds(r0, SUBLANES), HALF:TILE_STATE]
            for k, (kr, ki) in ((1, (c1r, c1i)), (2, (c2r, c2i)), (4, (c4r, c4i))):
                rr = pltpu.roll(br, k, 0)
                ri = pltpu.roll(bi, k, 0)
                br, bi = br + (kr * rr - ki * ri), bi + (kr * ri + ki * rr)
            sr = br + (pwr * cr - pwi * ci)
            si = bi + (pwr * ci + pwi * cr)
            bu_ref[pl.ds(r0, SUBLANES), 0:HALF] = sr
            bu_ref[pl.ds(r0, SUBLANES), HALF:TILE_STATE] = si
            ncr = jnp.broadcast_to(sr[SUBLANES - 1:SUBLANES, :], sr.shape)
            nci = jnp.broadcast_to(si[SUBLANES - 1:SUBLANES, :], si.shape)
            return ncr, nci

        cr0 = state_ref[0, :, lo:lo + HALF]
        ci0 = state_ref[1, :, lo:lo + HALF]
        crf, cif = lax.fori_loop(0, tb // SUBLANES, slab, (cr0, ci0))
        state_ref[0, :, lo:lo + HALF] = crf
        state_ref[1, :, lo:lo + HALF] = cif
        y_ref[:, j * LANES:(j + 1) * LANES] = jnp.dot(
            bu_ref[...].astype(jnp.bfloat16), wc_ref[j], preferred_element_type=jnp.float32)

    y = y_ref[...] + dskip_ref[...] * u
    y = _gelu_tanh(y)
    y = y * _sigmoid(_bdot(y, wglu_ref[...]) + bglu_ref[...])
    g_ssm = proj(O_GS, SSM_WIDTH)
    y = y * (g_ssm * _sigmoid(g_ssm))
    y_a = _bdot(y, wso_ref[...])
    merged = _sigmoid(proj(O_RS, D_MODEL)) * y_a

    v = proj(O_CC, CONV_WIDTH) * proj(O_XC, CONV_WIDTH)
    v_ref[0:SUBLANES, :] = vtail_ref[...]
    v_ref[SUBLANES:SUBLANES + tb, :] = v
    vtail_ref[...] = v[tb - SUBLANES:tb, :]
    cw = convw_ref[...]
    conv = (cw[0:1, :] * v_ref[SUBLANES - 2:SUBLANES - 2 + tb, :]
            + cw[1:2, :] * v_ref[SUBLANES - 1:SUBLANES - 1 + tb, :]
            + cw[2:3, :] * v)
    z = proj(O_BC, CONV_WIDTH) * conv
    g_conv = proj(O_GC, CONV_WIDTH)
    z = z * (g_conv * _sigmoid(g_conv))
    y_b = _bdot(z, wco_ref[...])
    merged = merged + _sigmoid(proj(O_RC, D_MODEL)) * y_b

    o = _bdot(merged, wo_ref[...])
    x1 = x + _rms(o, post_ref[...])
    e = _bdot(p_ref[...], wple_ref[...]) * _sigmoid(_bdot(x1, wpg_ref[...]))
    out_ref[...] = x1 + _rms(e, plen_ref[...])


def _ssm_tables(a_re, a_im, log_step, b_re, b_im, c_re, c_im):
    f32 = jnp.float32
    dt = jnp.exp(log_step.astype(f32))[:, None]
    mag = jnp.exp(a_re * dt)
    abar_r = mag * jnp.cos(a_im * dt)
    abar_i = mag * jnp.sin(a_im * dt)
    den = a_re * a_re + a_im * a_im
    nr = abar_r - 1.0
    fr = (nr * a_re + abar_i * a_im) / den
    fi = (abar_i * a_re - nr * a_im) / den
    bbar_r = fr[..., None] * b_re - fi[..., None] * b_im
    bbar_i = fr[..., None] * b_im + fi[..., None] * b_re

    eye = jnp.eye(GROUPS_PER_TILE, dtype=f32)

    def in_tile(b):
        b = b.reshape(N_TILES, GROUPS_PER_TILE, SSM_STATE, SSM_GROUP)
        w = jnp.einsum('jgph,gk->jghkp', b, eye)
        return w.reshape(N_TILES, LANES, HALF)

    def out_tile(c):
        c = c.reshape(N_TILES, GROUPS_PER_TILE, SSM_GROUP, SSM_STATE)
        w = jnp.einsum('jghp,gk->jgpkh', c, eye)
        return w.reshape(N_TILES, HALF, LANES)

    wb = jnp.concatenate([in_tile(bbar_r), in_tile(bbar_i)], axis=2)
    wc = jnp.concatenate([out_tile(c_re), -out_tile(c_im)], axis=1)

    ar = abar_r.reshape(1, -1)
    ai = abar_i.reshape(1, -1)

    def cmul(xr, xi, yr, yi):
        return xr * yr - xi * yi, xr * yi + xi * yr

    pows = [(ar, ai)]
    for _ in range(SUBLANES - 1):
        pows.append(cmul(*pows[-1], ar, ai))
    row = jnp.arange(SUBLANES, dtype=jnp.int32)[:, None]
    coef = []
    for k in (1, 2, 4):
        m = (row >= k).astype(f32)
        coef += [m * pows[k - 1][0], m * pows[k - 1][1]]
    coef += [jnp.concatenate([pw[0] for pw in pows], axis=0),
             jnp.concatenate([pw[1] for pw in pows], axis=0)]
    coef = jnp.stack(coef, axis=0)
    return wb.astype(jnp.bfloat16), wc.astype(jnp.bfloat16), coef


def _resident(shape):
    return pl.BlockSpec(shape, lambda i: (0,) * len(shape), pipeline_mode=pl.Buffered(1))


@jax.jit
def kernel(x, p, pre_norm, w_in, a_re, a_im, log_step, b_re, b_im, c_re, c_im, d_skip,
           w_glu, b_glu, w_ssm_out, conv_w, w_conv_out, w_o, post_norm,
           w_ple, w_ple_gate, ple_norm):
    bsz, seqlen, _ = x.shape
    depth = p.shape[0]
    assert bsz == 1 and depth == 1 and seqlen % ROW_BLOCK == 0
    bf16 = jnp.bfloat16
    tb = ROW_BLOCK

    wb, wc, coef = _ssm_tables(a_re[0], a_im[0], log_step[0], b_re[0], b_im[0], c_re[0], c_im[0])
    operands = [
        (x[0], pl.BlockSpec((tb, D_MODEL), lambda i: (i, 0))),
        (p[0, 0], pl.BlockSpec((tb, PLE_DIM), lambda i: (i, 0))),
        (pre_norm, _resident((1, D_MODEL))),
        (w_in[0].astype(bf16), _resident((D_MODEL, IN_WIDTH))),
        (wb, _resident(wb.shape)),
        (wc, _resident(wc.shape)),
        (coef, _resident(coef.shape)),
        (d_skip, _resident((1, SSM_WIDTH))),
        (w_glu[0].astype(bf16), _resident((SSM_WIDTH, SSM_WIDTH))),
        (b_glu, _resident((1, SSM_WIDTH))),
        (w_ssm_out[0].astype(bf16), _resident((SSM_WIDTH, D_MODEL))),
        (conv_w[0], _resident((CONV_K, CONV_WIDTH))),
        (w_conv_out[0].astype(bf16), _resident((CONV_WIDTH, D_MODEL))),
        (w_o[0].astype(bf16), _resident((D_MODEL, D_MODEL))),
        (post_norm, _resident((1, D_MODEL))),
        (w_ple[0].astype(bf16), _resident((PLE_DIM, D_MODEL))),
        (w_ple_gate[0].astype(bf16), _resident((D_MODEL, D_MODEL))),
        (ple_norm, _resident((1, D_MODEL))),
    ]
    out = pl.pallas_call(
        _block_kernel,
        grid=(seqlen // tb,),
        in_specs=[spec for _, spec in operands],
        out_specs=pl.BlockSpec((tb, D_MODEL), lambda i: (i, 0)),
        out_shape=jax.ShapeDtypeStruct((seqlen, D_MODEL), jnp.float32),
        scratch_shapes=[
            pltpu.VMEM((2, SUBLANES, N_TILES * HALF), jnp.float32),
            pltpu.VMEM((SUBLANES, CONV_WIDTH), jnp.float32),
            pltpu.VMEM((tb, TILE_STATE), jnp.float32),
            pltpu.VMEM((tb, SSM_WIDTH), jnp.float32),
            pltpu.VMEM((tb + SUBLANES, CONV_WIDTH), jnp.float32),
        ],
        compiler_params=pltpu.CompilerParams(
            dimension_semantics=("arbitrary",),
            vmem_limit_bytes=VMEM_LIMIT_BYTES),
        name="hybrid_s5_shortconv_block",
    )(*[a for a, _ in operands])
    return out[None]
```

```python
import math

import jax
import jax.numpy as jnp
from jax import lax
from jax.experimental import pallas as pl
from jax.experimental.pallas import tpu as pltpu

D_MODEL = 1024
PLE_DIM = 256
SSM_WIDTH = 512
SSM_GROUP = 16
SSM_GROUPS = 32
SSM_STATE = 64
CONV_WIDTH = 1024
CONV_K = 3
RMS_EPS = 1e-6
IN_WIDTH = 7168

LANES = 128
SUBLANES = 8
GROUPS_PER_TILE = LANES // SSM_GROUP
N_TILES = SSM_WIDTH // LANES
HALF = GROUPS_PER_TILE * SSM_STATE
TILE_STATE = 2 * HALF
N_STATE = SSM_GROUPS * SSM_STATE
ROW_BLOCK = 256
SEG_ROWS = ROW_BLOCK // SUBLANES
VMEM_LIMIT_BYTES = 56 * 1024 * 1024

O_U, O_GS, O_XC, O_BC, O_CC, O_GC, O_RS, O_RC = 0, 512, 1024, 2048, 3072, 4096, 5120, 6144


def _rms(v, gain):
    return v * lax.rsqrt(jnp.mean(v * v, axis=-1, keepdims=True) + RMS_EPS) * gain


def _sigmoid(v):
    return 1.0 / (1.0 + jnp.exp(-v))


def _gelu_tanh(v):
    c = math.sqrt(2.0 / math.pi)
    return v * (0.5 * (1.0 + jnp.tanh(c * (v + 0.044715 * (v * v * v)))))


def _bdot(a, b):
    return jnp.dot(a.astype(jnp.bfloat16), b, preferred_element_type=jnp.float32)


def _cmul(ar, ai, xr, xi):
    return ar * xr - ai * xi, ar * xi + ai * xr


def _block_kernel(x_ref, p_ref, pre_ref, win_ref, wb_ref, wc_ref, arow_ref, seg_ref, pw_ref,
                  dskip_ref, wglu_ref, bglu_ref, wso_ref, convw_ref, wco_ref, wo_ref, post_ref,
                  wple_ref, wpg_ref, plen_ref, out_ref,
                  state_ref, vtail_ref, up_ref, bu_ref, yp_ref, y_ref, v_ref):
    tb = x_ref.shape[0]
    ns = tb // SUBLANES

    @pl.when(pl.program_id(0) == 0)
    def _():
        state_ref[...] = jnp.zeros_like(state_ref)
        vtail_ref[...] = jnp.zeros_like(vtail_ref)

    x = x_ref[...]
    h = _rms(x, pre_ref[...]).astype(jnp.bfloat16)

    def proj(off, width):
        return jnp.dot(h, win_ref[:, off:off + width], preferred_element_type=jnp.float32)

    u = proj(O_U, SSM_WIDTH)
    for t in range(N_TILES):
        for j in range(SUBLANES):
            for i0 in range(0, ns, SUBLANES):
                r0 = j * ns + i0
                up_ref[t, pl.ds(SUBLANES * i0 + j, SUBLANES, stride=SUBLANES), :] = (
                    u[r0:r0 + SUBLANES, t * LANES:(t + 1) * LANES])
    for t in range(N_TILES):
        bu_ref[t] = jnp.dot(up_ref[t].astype(jnp.bfloat16), wb_ref[t],
                            preferred_element_type=jnp.float32)
    row0 = lax.broadcasted_iota(jnp.int32, (SUBLANES, HALF), 0) == 0
    for t in range(N_TILES):
        sl = slice(t * HALF, (t + 1) * HALF)
        ar = arow_ref[0, :, sl]
        ai = arow_ref[1, :, sl]
        sr = bu_ref[t, 0:SUBLANES, 0:HALF]
        si = bu_ref[t, 0:SUBLANES, HALF:TILE_STATE]
        for i in range(1, ns):
            rows = slice(i * SUBLANES, (i + 1) * SUBLANES)
            mr, mi = _cmul(ar, ai, sr, si)
            sr = mr + bu_ref[t, rows, 0:HALF]
            si = mi + bu_ref[t, rows, HALF:TILE_STATE]
            bu_ref[t, rows, 0:HALF] = sr
            bu_ref[t, rows, HALF:TILE_STATE] = si
        gr = jnp.where(row0, state_ref[0, :, sl], pltpu.roll(sr, 1, 0))
        gi = jnp.where(row0, state_ref[1, :, sl], pltpu.roll(si, 1, 0))
        for n, k in enumerate((1, 2, 4)):
            fr, fi = _cmul(seg_ref[2 * n, :, sl], seg_ref[2 * n + 1, :, sl],
                           pltpu.roll(gr, k, 0), pltpu.roll(gi, k, 0))
            gr, gi = gr + fr, gi + fi
        fr, fi = _cmul(seg_ref[6, :, sl], seg_ref[7, :, sl], gr, gi)
        nr, ni = fr + sr, fi + si
        state_ref[0, :, sl] = jnp.broadcast_to(nr[SUBLANES - 1:SUBLANES, :], nr.shape)
        state_ref[1, :, sl] = jnp.broadcast_to(ni[SUBLANES - 1:SUBLANES, :], ni.shape)
        for i in range(ns):
            rows = slice(i * SUBLANES, (i + 1) * SUBLANES)
            fr, fi = _cmul(pw_ref[0, i, :, sl], pw_ref[1, i, :, sl], gr, gi)
            bu_ref[t, rows, 0:HALF] = bu_ref[t, rows, 0:HALF] + fr
            bu_ref[t, rows, HALF:TILE_STATE] = bu_ref[t, rows, HALF:TILE_STATE] + fi
        yp_ref[t] = jnp.dot(bu_ref[t].astype(jnp.bfloat16), wc_ref[t],
                            preferred_element_type=jnp.float32)
    for t in range(N_TILES):
        for j in range(SUBLANES):
            for i0 in range(0, ns, SUBLANES):
                r0 = j * ns + i0
                y_ref[r0:r0 + SUBLANES, t * LANES:(t + 1) * LANES] = (
                    yp_ref[t, pl.ds(SUBLANES * i0 + j, SUBLANES, stride=SUBLANES), :])

    y = y_ref[...] + dskip_ref[...] * u
    y = _gelu_tanh(y)
    y = y * _sigmoid(_bdot(y, wglu_ref[...]) + bglu_ref[...])
    g_ssm = proj(O_GS, SSM_WIDTH)
    y = y * (g_ssm * _sigmoid(g_ssm))
    y_a = _bdot(y, wso_ref[...])
    merged = _sigmoid(proj(O_RS, D_MODEL)) * y_a

    v = proj(O_CC, CONV_WIDTH) * proj(O_XC, CONV_WIDTH)
    v_ref[0:SUBLANES, :] = vtail_ref[...]
    v_ref[SUBLANES:SUBLANES + tb, :] = v
    vtail_ref[...] = v[tb - SUBLANES:tb, :]
    cw = convw_ref[...]
    conv = (cw[0:1, :] * v_ref[SUBLANES - 2:SUBLANES - 2 + tb, :]
            + cw[1:2, :] * v_ref[SUBLANES - 1:SUBLANES - 1 + tb, :]
            + cw[2:3, :] * v)
    z = proj(O_BC, CONV_WIDTH) * conv
    g_conv = proj(O_GC, CONV_WIDTH)
    z = z * (g_conv * _sigmoid(g_conv))
    y_b = _bdot(z, wco_ref[...])
    merged = merged + _sigmoid(proj(O_RC, D_MODEL)) * y_b

    o = _bdot(merged, wo_ref[...])
    x1 = x + _rms(o, post_ref[...])
    e = _bdot(p_ref[...], wple_ref[...]) * _sigmoid(_bdot(x1, wpg_ref[...]))
    out_ref[...] = x1 + _rms(e, plen_ref[...])


def _ssm_tables(a_re, a_im, log_step, b_re, b_im, c_re, c_im):
    f32 = jnp.float32
    dt = jnp.exp(log_step.astype(f32))[:, None]
    mag = jnp.exp(a_re * dt)
    abar_r = mag * jnp.cos(a_im * dt)
    abar_i = mag * jnp.sin(a_im * dt)
    den = a_re * a_re + a_im * a_im
    nr = abar_r - 1.0
    fr = (nr * a_re + abar_i * a_im) / den
    fi = (abar_i * a_re - nr * a_im) / den
    bbar_r = fr[..., None] * b_re - fi[..., None] * b_im
    bbar_i = fr[..., None] * b_im + fi[..., None] * b_re

    eye = jnp.eye(GROUPS_PER_TILE, dtype=f32)

    def in_tile(b):
        b = b.reshape(N_TILES, GROUPS_PER_TILE, SSM_STATE, SSM_GROUP)
        w = jnp.einsum('jgph,gk->jghkp', b, eye)
        return w.reshape(N_TILES, LANES, HALF)

    def out_tile(c):
        c = c.reshape(N_TILES, GROUPS_PER_TILE, SSM_GROUP, SSM_STATE)
        w = jnp.einsum('jghp,gk->jgpkh', c, eye)
        return w.reshape(N_TILES, HALF, LANES)

    wb = jnp.concatenate([in_tile(bbar_r), in_tile(bbar_i)], axis=2)
    wc = jnp.concatenate([out_tile(c_re), -out_tile(c_im)], axis=1)

    pr = abar_r.reshape(1, N_STATE)
    pi = abar_i.reshape(1, N_STATE)
    while pr.shape[0] < SEG_ROWS:
        tr, ti = _cmul(pr[-1:], pi[-1:], pr, pi)
        pr = jnp.concatenate([pr, tr], axis=0)
        pi = jnp.concatenate([pi, ti], axis=0)
    ones = jnp.ones((SUBLANES, 1), f32)
    arow = jnp.stack([ones * pr[0:1], ones * pi[0:1]])
    pw = jnp.stack([pr[:, None, :] * ones, pi[:, None, :] * ones])
    a1 = (pr[-1:], pi[-1:])
    a2 = _cmul(*a1, *a1)
    a4 = _cmul(*a2, *a2)
    row = jnp.arange(SUBLANES, dtype=jnp.int32)[:, None]
    seg = []
    for k, (qr, qi) in ((1, a1), (2, a2), (4, a4)):
        m = (row >= k).astype(f32)
        seg += [m * qr, m * qi]
    seg += [ones * a1[0], ones * a1[1]]
    seg = jnp.stack(seg)
    return wb.astype(jnp.bfloat16), wc.astype(jnp.bfloat16), arow, seg, pw


def _resident(shape):
    return pl.BlockSpec(shape, lambda i: (0,) * len(shape), pipeline_mode=pl.Buffered(1))


@jax.jit
def kernel(x, p, pre_norm, w_in, a_re, a_im, log_step, b_re, b_im, c_re, c_im, d_skip,
           w_glu, b_glu, w_ssm_out, conv_w, w_conv_out, w_o, post_norm,
           w_ple, w_ple_gate, ple_norm):
    bsz, seqlen, _ = x.shape
    depth = p.shape[0]
    assert bsz == 1 and depth == 1 and seqlen % ROW_BLOCK == 0
    bf16 = jnp.bfloat16
    tb = ROW_BLOCK

    wb, wc, arow, seg, pw = _ssm_tables(
        a_re[0], a_im[0], log_step[0], b_re[0], b_im[0], c_re[0], c_im[0])
    operands = [
        (x[0], pl.BlockSpec((tb, D_MODEL), lambda i: (i, 0))),
        (p[0, 0], pl.BlockSpec((tb, PLE_DIM), lambda i: (i, 0))),
        (pre_norm, _resident((1, D_MODEL))),
        (w_in[0].astype(bf16), _resident((D_MODEL, IN_WIDTH))),
        (wb, _resident(wb.shape)),
        (wc, _resident(wc.shape)),
        (arow, _resident(arow.shape)),
        (seg, _resident(seg.shape)),
        (pw, _resident(pw.shape)),
        (d_skip, _resident((1, SSM_WIDTH))),
        (w_glu[0].astype(bf16), _resident((SSM_WIDTH, SSM_WIDTH))),
        (b_glu, _resident((1, SSM_WIDTH))),
        (w_ssm_out[0].astype(bf16), _resident((SSM_WIDTH, D_MODEL))),
        (conv_w[0], _resident((CONV_K, CONV_WIDTH))),
        (w_conv_out[0].astype(bf16), _resident((CONV_WIDTH, D_MODEL))),
        (w_o[0].astype(bf16), _resident((D_MODEL, D_MODEL))),
        (post_norm, _resident((1, D_MODEL))),
        (w_ple[0].astype(bf16), _resident((PLE_DIM, D_MODEL))),
        (w_ple_gate[0].astype(bf16), _resident((D_MODEL, D_MODEL))),
        (ple_norm, _resident((1, D_MODEL))),
    ]
    out = pl.pallas_call(
        _block_kernel,
        grid=(seqlen // tb,),
        in_specs=[spec for _, spec in operands],
        out_specs=pl.BlockSpec((tb, D_MODEL), lambda i: (i, 0)),
        out_shape=jax.ShapeDtypeStruct((seqlen, D_MODEL), jnp.float32),
        scratch_shapes=[
            pltpu.VMEM((2, SUBLANES, N_STATE), jnp.float32),
            pltpu.VMEM((SUBLANES, CONV_WIDTH), jnp.float32),
            pltpu.VMEM((N_TILES, tb, LANES), jnp.float32),
            pltpu.VMEM((N_TILES, tb, TILE_STATE), jnp.float32),
            pltpu.VMEM((N_TILES, tb, LANES), jnp.float32),
            pltpu.VMEM((tb, SSM_WIDTH), jnp.float32),
            pltpu.VMEM((tb + SUBLANES, CONV_WIDTH), jnp.float32),
        ],
        compiler_params=pltpu.CompilerParams(
            dimension_semantics=("arbitrary",),
            vmem_limit_bytes=VMEM_LIMIT_BYTES),
        name="hybrid_s5_shortconv_block",
    )(*[a for a, _ in operands])
    return out[None]
```

```python
import math

import jax
import jax.numpy as jnp
from jax import lax
from jax.experimental import pallas as pl
from jax.experimental.pallas import tpu as pltpu

D_MODEL = 1024
PLE_DIM = 256
SSM_WIDTH = 512
SSM_GROUP = 16
SSM_GROUPS = 32
SSM_STATE = 64
CONV_WIDTH = 1024
CONV_K = 3
RMS_EPS = 1e-6
IN_WIDTH = 7168

LANES = 128
SUBLANES = 8
GROUPS_PER_TILE = LANES // SSM_GROUP
N_TILES = SSM_WIDTH // LANES
HALF = GROUPS_PER_TILE * SSM_STATE
TILE_STATE = 2 * HALF
N_COLS = HALF // LANES
N_STATE = SSM_GROUPS * SSM_STATE
ROW_BLOCK = 256
SEG_ROWS = ROW_BLOCK // SUBLANES
VMEM_LIMIT_BYTES = 56 * 1024 * 1024

O_U, O_GS, O_XC, O_BC, O_CC, O_GC, O_RS, O_RC = 0, 512, 1024, 2048, 3072, 4096, 5120, 6144


def _rms(v, gain):
    return v * lax.rsqrt(jnp.mean(v * v, axis=-1, keepdims=True) + RMS_EPS) * gain


def _sigmoid(v):
    return 1.0 / (1.0 + jnp.exp(-v))


def _gelu_tanh(v):
    c = math.sqrt(2.0 / math.pi)
    return v * (0.5 * (1.0 + jnp.tanh(c * (v + 0.044715 * (v * v * v)))))


def _bdot(a, b):
    return jnp.dot(a.astype(jnp.bfloat16), b, preferred_element_type=jnp.float32)


def _cmul(ar, ai, xr, xi):
    return ar * xr - ai * xi, ar * xi + ai * xr


def _block_kernel(x_ref, p_ref, pre_ref, win_ref, wb_ref, wc_ref, arow_ref, seg_ref, pw_ref,
                  dskip_ref, wglu_ref, bglu_ref, wso_ref, convw_ref, wco_ref, wo_ref, post_ref,
                  wple_ref, wpg_ref, plen_ref, out_ref,
                  state_ref, vtail_ref, up_ref, sloc_ref, s_ref, yp_ref, y_ref, v_ref):
    tb = x_ref.shape[0]
    ns = tb // SUBLANES

    @pl.when(pl.program_id(0) == 0)
    def _():
        state_ref[...] = jnp.zeros_like(state_ref)
        vtail_ref[...] = jnp.zeros_like(vtail_ref)

    x = x_ref[...]
    h = _rms(x, pre_ref[...]).astype(jnp.bfloat16)

    def proj(off, width):
        return jnp.dot(h, win_ref[:, off:off + width], preferred_element_type=jnp.float32)


    u = proj(O_U, SSM_WIDTH)
    for t in range(N_TILES):
        for j in range(SUBLANES):
            for i0 in range(0, ns, SUBLANES):
                r0 = j * ns + i0
                up_ref[t, pl.ds(SUBLANES * i0 + j, SUBLANES, stride=SUBLANES), :] = (
                    u[r0:r0 + SUBLANES, t * LANES:(t + 1) * LANES])

    v = proj(O_CC, CONV_WIDTH) * proj(O_XC, CONV_WIDTH)
    v_ref[0:SUBLANES, :] = vtail_ref[...]
    v_ref[SUBLANES:SUBLANES + tb, :] = v
    vtail_ref[...] = v[tb - SUBLANES:tb, :]
    cw = convw_ref[...]
    conv = (cw[0:1, :] * v_ref[SUBLANES - 2:SUBLANES - 2 + tb, :]
            + cw[1:2, :] * v_ref[SUBLANES - 1:SUBLANES - 1 + tb, :]
            + cw[2:3, :] * v)

    row0 = lax.broadcasted_iota(jnp.int32, (SUBLANES, LANES), 0) == 0

    def scan_tile(t):
        bu = jnp.dot(up_ref[t].astype(jnp.bfloat16), wb_ref[t],
                     preferred_element_type=jnp.float32)
        for q in range(N_COLS):
            re = slice(q * 2 * LANES, q * 2 * LANES + LANES)
            im = slice(q * 2 * LANES + LANES, (q + 1) * 2 * LANES)
            sl = slice(t * HALF + q * LANES, t * HALF + (q + 1) * LANES)
            ar = arow_ref[0, :, sl]
            ai = arow_ref[1, :, sl]
            sr = bu[0:SUBLANES, re]
            si = bu[0:SUBLANES, im]
            sloc_ref[t, 0:SUBLANES, re] = sr
            sloc_ref[t, 0:SUBLANES, im] = si
            for i in range(1, ns):
                rows = slice(i * SUBLANES, (i + 1) * SUBLANES)
                mr, mi = _cmul(ar, ai, sr, si)
                sr = mr + bu[rows, re]
                si = mi + bu[rows, im]
                sloc_ref[t, rows, re] = sr
                sloc_ref[t, rows, im] = si
            gr = jnp.where(row0, state_ref[0, :, sl], pltpu.roll(sr, 1, 0))
            gi = jnp.where(row0, state_ref[1, :, sl], pltpu.roll(si, 1, 0))
            for n, k in enumerate((1, 2, 4)):
                fr, fi = _cmul(seg_ref[2 * n, :, sl], seg_ref[2 * n + 1, :, sl],
                               pltpu.roll(gr, k, 0), pltpu.roll(gi, k, 0))
                gr, gi = gr + fr, gi + fi
            fr, fi = _cmul(seg_ref[6, :, sl], seg_ref[7, :, sl], gr, gi)
            nr, ni = fr + sr, fi + si
            state_ref[0, :, sl] = jnp.broadcast_to(nr[SUBLANES - 1:SUBLANES, :], nr.shape)
            state_ref[1, :, sl] = jnp.broadcast_to(ni[SUBLANES - 1:SUBLANES, :], ni.shape)
            for m in range(ns // 2):
                parts_r, parts_i = [], []
                for i in (2 * m, 2 * m + 1):
                    rows = slice(i * SUBLANES, (i + 1) * SUBLANES)
                    fr, fi = _cmul(pw_ref[0, i, :, sl], pw_ref[1, i, :, sl], gr, gi)
                    parts_r.append(sloc_ref[t, rows, re] + fr)
                    parts_i.append(sloc_ref[t, rows, im] + fi)
                rows2 = slice(2 * m * SUBLANES, (2 * m + 2) * SUBLANES)
                s_ref[t, rows2, re] = jnp.concatenate(parts_r, axis=0).astype(jnp.bfloat16)
                s_ref[t, rows2, im] = jnp.concatenate(parts_i, axis=0).astype(jnp.bfloat16)

    scan_tile(0)
    z = proj(O_BC, CONV_WIDTH) * conv
    scan_tile(1)
    g_conv = proj(O_GC, CONV_WIDTH)
    z = (z * (g_conv * _sigmoid(g_conv))).astype(jnp.bfloat16)
    scan_tile(2)
    g_ssm = proj(O_GS, SSM_WIDTH)
    g_ssm = g_ssm * _sigmoid(g_ssm)
    e_in = _bdot(p_ref[...], wple_ref[...])
    scan_tile(3)
    gate_a = _sigmoid(proj(O_RS, D_MODEL))
    y_b = jnp.dot(z, wco_ref[...], preferred_element_type=jnp.float32)

    for t in range(N_TILES):
        yp_ref[t] = jnp.dot(s_ref[t], wc_ref[t], preferred_element_type=jnp.float32)
    gate_b = _sigmoid(proj(O_RC, D_MODEL))
    for t in range(N_TILES):
        for j in range(SUBLANES):
            for i0 in range(0, ns, SUBLANES):
                r0 = j * ns + i0
                y_ref[r0:r0 + SUBLANES, t * LANES:(t + 1) * LANES] = (
                    yp_ref[t, pl.ds(SUBLANES * i0 + j, SUBLANES, stride=SUBLANES), :])
    y = _gelu_tanh(y_ref[...] + dskip_ref[...] * u)
    glu = _bdot(y, wglu_ref[...])
    y = y * _sigmoid(glu + bglu_ref[...]) * g_ssm
    y_a = _bdot(y, wso_ref[...])
    merged = gate_a * y_a + gate_b * y_b

    o = _bdot(merged, wo_ref[...])
    x1 = x + _rms(o, post_ref[...])
    e = e_in * _sigmoid(_bdot(x1, wpg_ref[...]))
    out_ref[...] = x1 + _rms(e, plen_ref[...])


def _ssm_tables(a_re, a_im, log_step, b_re, b_im, c_re, c_im):
    f32 = jnp.float32
    dt = jnp.exp(log_step.astype(f32))[:, None]
    mag = jnp.exp(a_re * dt)
    abar_r = mag * jnp.cos(a_im * dt)
    abar_i = mag * jnp.sin(a_im * dt)
    den = a_re * a_re + a_im * a_im
    nr = abar_r - 1.0
    fr = (nr * a_re + abar_i * a_im) / den
    fi = (abar_i * a_re - nr * a_im) / den
    bbar_r = fr[..., None] * b_re - fi[..., None] * b_im
    bbar_i = fr[..., None] * b_im + fi[..., None] * b_re

    eye = jnp.eye(GROUPS_PER_TILE, dtype=f32)

    def in_tile(b):
        b = b.reshape(N_TILES, GROUPS_PER_TILE, SSM_STATE, SSM_GROUP)
        w = jnp.einsum('jgph,gk->jghkp', b, eye)
        return w.reshape(N_TILES, LANES, HALF)

    def out_tile(c):
        c = c.reshape(N_TILES, GROUPS_PER_TILE, SSM_GROUP, SSM_STATE)
        w = jnp.einsum('jghp,gk->jgpkh', c, eye)
        return w.reshape(N_TILES, HALF, LANES)

    wb = jnp.stack([in_tile(bbar_r).reshape(N_TILES, LANES, N_COLS, LANES),
                    in_tile(bbar_i).reshape(N_TILES, LANES, N_COLS, LANES)], axis=3)
    wb = wb.reshape(N_TILES, LANES, TILE_STATE)
    wc = jnp.stack([out_tile(c_re).reshape(N_TILES, N_COLS, LANES, LANES),
                    -out_tile(c_im).reshape(N_TILES, N_COLS, LANES, LANES)], axis=2)
    wc = wc.reshape(N_TILES, TILE_STATE, LANES)

    pr = abar_r.reshape(1, N_STATE)
    pi = abar_i.reshape(1, N_STATE)
    while pr.shape[0] < SEG_ROWS:
        tr, ti = _cmul(pr[-1:], pi[-1:], pr, pi)
        pr = jnp.concatenate([pr, tr], axis=0)
        pi = jnp.concatenate([pi, ti], axis=0)
    ones = jnp.ones((SUBLANES, 1), f32)
    arow = jnp.stack([ones * pr[0:1], ones * pi[0:1]])
    pw = jnp.stack([pr[:, None, :] * ones, pi[:, None, :] * ones])
    a1 = (pr[-1:], pi[-1:])
    a2 = _cmul(*a1, *a1)
    a4 = _cmul(*a2, *a2)
    row = jnp.arange(SUBLANES, dtype=jnp.int32)[:, None]
    seg = []
    for k, (qr, qi) in ((1, a1), (2, a2), (4, a4)):
        m = (row >= k).astype(f32)
        seg += [m * qr, m * qi]
    seg += [ones * a1[0], ones * a1[1]]
    seg = jnp.stack(seg)
    return wb.astype(jnp.bfloat16), wc.astype(jnp.bfloat16), arow, seg, pw


def _resident(shape):
    return pl.BlockSpec(shape, lambda i: (0,) * len(shape), pipeline_mode=pl.Buffered(1))


@jax.jit
def kernel(x, p, pre_norm, w_in, a_re, a_im, log_step, b_re, b_im, c_re, c_im, d_skip,
           w_glu, b_glu, w_ssm_out, conv_w, w_conv_out, w_o, post_norm,
           w_ple, w_ple_gate, ple_norm):
    bsz, seqlen, _ = x.shape
    depth = p.shape[0]
    assert bsz == 1 and depth == 1 and seqlen % ROW_BLOCK == 0
    bf16 = jnp.bfloat16
    tb = ROW_BLOCK

    wb, wc, arow, seg, pw = _ssm_tables(
        a_re[0], a_im[0], log_step[0], b_re[0], b_im[0], c_re[0], c_im[0])
    operands = [
        (x[0], pl.BlockSpec((tb, D_MODEL), lambda i: (i, 0))),
        (p[0, 0], pl.BlockSpec((tb, PLE_DIM), lambda i: (i, 0))),
        (pre_norm, _resident((1, D_MODEL))),
        (w_in[0].astype(bf16), _resident((D_MODEL, IN_WIDTH))),
        (wb, _resident(wb.shape)),
        (wc, _resident(wc.shape)),
        (arow, _resident(arow.shape)),
        (seg, _resident(seg.shape)),
        (pw, _resident(pw.shape)),
        (d_skip, _resident((1, SSM_WIDTH))),
        (w_glu[0].astype(bf16), _resident((SSM_WIDTH, SSM_WIDTH))),
        (b_glu, _resident((1, SSM_WIDTH))),
        (w_ssm_out[0].astype(bf16), _resident((SSM_WIDTH, D_MODEL))),
        (conv_w[0], _resident((CONV_K, CONV_WIDTH))),
        (w_conv_out[0].astype(bf16), _resident((CONV_WIDTH, D_MODEL))),
        (w_o[0].astype(bf16), _resident((D_MODEL, D_MODEL))),
        (post_norm, _resident((1, D_MODEL))),
        (w_ple[0].astype(bf16), _resident((PLE_DIM, D_MODEL))),
        (w_ple_gate[0].astype(bf16), _resident((D_MODEL, D_MODEL))),
        (ple_norm, _resident((1, D_MODEL))),
    ]
    out = pl.pallas_call(
        _block_kernel,
        grid=(seqlen // tb,),
        in_specs=[spec for _, spec in operands],
        out_specs=pl.BlockSpec((tb, D_MODEL), lambda i: (i, 0)),
        out_shape=jax.ShapeDtypeStruct((seqlen, D_MODEL), jnp.float32),
        scratch_shapes=[
            pltpu.VMEM((2, SUBLANES, N_STATE), jnp.float32),
            pltpu.VMEM((SUBLANES, CONV_WIDTH), jnp.float32),
            pltpu.VMEM((N_TILES, tb, LANES), jnp.float32),
            pltpu.VMEM((N_TILES, tb, TILE_STATE), jnp.float32),
            pltpu.VMEM((N_TILES, tb, TILE_STATE), jnp.bfloat16),
            pltpu.VMEM((N_TILES, tb, LANES), jnp.float32),
            pltpu.VMEM((tb, SSM_WIDTH), jnp.float32),
            pltpu.VMEM((tb + SUBLANES, CONV_WIDTH), jnp.float32),
        ],
        compiler_params=pltpu.CompilerParams(
            dimension_semantics=("arbitrary",),
            vmem_limit_bytes=VMEM_LIMIT_BYTES),
        name="hybrid_s5_shortconv_block",
    )(*[a for a, _ in operands])
    return out[None]
```

```python
import math

import jax
import jax.numpy as jnp
from jax import lax
from jax.experimental import pallas as pl
from jax.experimental.pallas import tpu as pltpu

D_MODEL = 1024
PLE_DIM = 256
SSM_WIDTH = 512
SSM_GROUP = 16
SSM_GROUPS = 32
SSM_STATE = 64
CONV_WIDTH = 1024
CONV_K = 3
RMS_EPS = 1e-6
IN_WIDTH = 7168

LANES = 128
SUBLANES = 8
GROUPS_PER_TILE = LANES // SSM_GROUP
N_TILES = SSM_WIDTH // LANES
HALF = GROUPS_PER_TILE * SSM_STATE
TILE_STATE = 2 * HALF
N_COLS = HALF // LANES
N_STATE = SSM_GROUPS * SSM_STATE
ROW_BLOCK = 256
SEG_ROWS = ROW_BLOCK // SUBLANES
VMEM_LIMIT_BYTES = 56 * 1024 * 1024

O_U, O_GS, O_XC, O_BC, O_CC, O_GC, O_RS, O_RC = 0, 512, 1024, 2048, 3072, 4096, 5120, 6144


def _rms(v, gain):
    return v * lax.rsqrt(jnp.mean(v * v, axis=-1, keepdims=True) + RMS_EPS) * gain


def _sigmoid(v):
    return 1.0 / (1.0 + jnp.exp(-v))


def _gelu_tanh(v):
    c = math.sqrt(2.0 / math.pi)
    return v * (0.5 * (1.0 + jnp.tanh(c * (v + 0.044715 * (v * v * v)))))


def _bdot(a, b):
    return jnp.dot(a.astype(jnp.bfloat16), b, preferred_element_type=jnp.float32)


def _cmul(ar, ai, xr, xi):
    return ar * xr - ai * xi, ar * xi + ai * xr


def _block_kernel(x_ref, p_ref, pre_ref, win_ref, wb_ref, wc_ref, arow_ref, seg_ref, pw_ref,
                  dskip_ref, wglu_ref, bglu_ref, wso_ref, convw_ref, wco_ref, wo_ref, post_ref,
                  wple_ref, wpg_ref, plen_ref, out_ref,
                  state_ref, vtail_ref, up_ref, sloc_ref, s_ref, yp_ref, y_ref, v_ref,
                  mprev_ref, xprev_ref):
    tb = x_ref.shape[0]
    ns = tb // SUBLANES

    @pl.when(pl.program_id(0) == 0)
    def _():
        state_ref[...] = jnp.zeros_like(state_ref)
        vtail_ref[...] = jnp.zeros_like(vtail_ref)
        mprev_ref[...] = jnp.zeros_like(mprev_ref)
        xprev_ref[...] = jnp.zeros_like(xprev_ref)


    o = jnp.dot(mprev_ref[...], wo_ref[...], preferred_element_type=jnp.float32)

    x = x_ref[...]
    h = _rms(x, pre_ref[...]).astype(jnp.bfloat16)

    def proj(off, width):
        return jnp.dot(h, win_ref[:, off:off + width], preferred_element_type=jnp.float32)

    u = proj(O_U, SSM_WIDTH)
    for t in range(N_TILES):
        for j in range(SUBLANES):
            for i0 in range(0, ns, SUBLANES):
                r0 = j * ns + i0
                up_ref[t, pl.ds(SUBLANES * i0 + j, SUBLANES, stride=SUBLANES), :] = (
                    u[r0:r0 + SUBLANES, t * LANES:(t + 1) * LANES])

    v = proj(O_CC, CONV_WIDTH) * proj(O_XC, CONV_WIDTH)
    v_ref[0:SUBLANES, :] = vtail_ref[...]
    v_ref[SUBLANES:SUBLANES + tb, :] = v
    vtail_ref[...] = v[tb - SUBLANES:tb, :]
    cw = convw_ref[...]
    conv = (cw[0:1, :] * v_ref[SUBLANES - 2:SUBLANES - 2 + tb, :]
            + cw[1:2, :] * v_ref[SUBLANES - 1:SUBLANES - 1 + tb, :]
            + cw[2:3, :] * v)

    row0 = lax.broadcasted_iota(jnp.int32, (SUBLANES, LANES), 0) == 0

    def scan_tile(t):
        bu = jnp.dot(up_ref[t].astype(jnp.bfloat16), wb_ref[t],
                     preferred_element_type=jnp.float32)
        for q in range(N_COLS):
            re = slice(q * 2 * LANES, q * 2 * LANES + LANES)
            im = slice(q * 2 * LANES + LANES, (q + 1) * 2 * LANES)
            sl = slice(t * HALF + q * LANES, t * HALF + (q + 1) * LANES)
            ar = arow_ref[0, :, sl]
            ai = arow_ref[1, :, sl]
            sr = bu[0:SUBLANES, re]
            si = bu[0:SUBLANES, im]
            sloc_ref[t, 0:SUBLANES, re] = sr
            sloc_ref[t, 0:SUBLANES, im] = si
            for i in range(1, ns):
                rows = slice(i * SUBLANES, (i + 1) * SUBLANES)
                mr, mi = _cmul(ar, ai, sr, si)
                sr = mr + bu[rows, re]
                si = mi + bu[rows, im]
                sloc_ref[t, rows, re] = sr
                sloc_ref[t, rows, im] = si
            gr = jnp.where(row0, state_ref[0, :, sl], pltpu.roll(sr, 1, 0))
            gi = jnp.where(row0, state_ref[1, :, sl], pltpu.roll(si, 1, 0))
            for n, k in enumerate((1, 2, 4)):
                fr, fi = _cmul(seg_ref[2 * n, :, sl], seg_ref[2 * n + 1, :, sl],
                               pltpu.roll(gr, k, 0), pltpu.roll(gi, k, 0))
                gr, gi = gr + fr, gi + fi
            fr, fi = _cmul(seg_ref[6, :, sl], seg_ref[7, :, sl], gr, gi)
            nr, ni = fr + sr, fi + si
            state_ref[0, :, sl] = jnp.broadcast_to(nr[SUBLANES - 1:SUBLANES, :], nr.shape)
            state_ref[1, :, sl] = jnp.broadcast_to(ni[SUBLANES - 1:SUBLANES, :], ni.shape)
            for m in range(ns // 2):
                parts_r, parts_i = [], []
                for i in (2 * m, 2 * m + 1):
                    rows = slice(i * SUBLANES, (i + 1) * SUBLANES)
                    fr, fi = _cmul(pw_ref[0, i, :, sl], pw_ref[1, i, :, sl], gr, gi)
                    parts_r.append(sloc_ref[t, rows, re] + fr)
                    parts_i.append(sloc_ref[t, rows, im] + fi)
                rows2 = slice(2 * m * SUBLANES, (2 * m + 2) * SUBLANES)
                s_ref[t, rows2, re] = jnp.concatenate(parts_r, axis=0).astype(jnp.bfloat16)
                s_ref[t, rows2, im] = jnp.concatenate(parts_i, axis=0).astype(jnp.bfloat16)

    scan_tile(0)
    z = proj(O_BC, CONV_WIDTH) * conv
    x1 = xprev_ref[...] + _rms(o, post_ref[...])
    scan_tile(1)
    g_conv = proj(O_GC, CONV_WIDTH)
    z = (z * (g_conv * _sigmoid(g_conv))).astype(jnp.bfloat16)
    scan_tile(2)
    e_gate = _bdot(x1, wpg_ref[...])
    e_in = _bdot(p_ref[...], wple_ref[...])
    scan_tile(3)
    g_ssm = proj(O_GS, SSM_WIDTH)
    g_ssm = g_ssm * _sigmoid(g_ssm)
    out_ref[...] = x1 + _rms(e_in * _sigmoid(e_gate), plen_ref[...])
    y_b = jnp.dot(z, wco_ref[...], preferred_element_type=jnp.float32)

    for t in range(N_TILES):
        yp_ref[t] = jnp.dot(s_ref[t], wc_ref[t], preferred_element_type=jnp.float32)
    gate_a = _sigmoid(proj(O_RS, D_MODEL))
    for t in range(N_TILES):
        for j in range(SUBLANES):
            for i0 in range(0, ns, SUBLANES):
                r0 = j * ns + i0
                y_ref[r0:r0 + SUBLANES, t * LANES:(t + 1) * LANES] = (
                    yp_ref[t, pl.ds(SUBLANES * i0 + j, SUBLANES, stride=SUBLANES), :])
    y = _gelu_tanh(y_ref[...] + dskip_ref[...] * u)
    glu = _bdot(y, wglu_ref[...])
    gate_b = _sigmoid(proj(O_RC, D_MODEL))
    y = y * _sigmoid(glu + bglu_ref[...]) * g_ssm
    y_a = _bdot(y, wso_ref[...])
    mprev_ref[...] = (gate_a * y_a + gate_b * y_b).astype(jnp.bfloat16)
    xprev_ref[...] = x


def _ssm_tables(a_re, a_im, log_step, b_re, b_im, c_re, c_im):
    f32 = jnp.float32
    dt = jnp.exp(log_step.astype(f32))[:, None]
    mag = jnp.exp(a_re * dt)
    abar_r = mag * jnp.cos(a_im * dt)
    abar_i = mag * jnp.sin(a_im * dt)
    den = a_re * a_re + a_im * a_im
    nr = abar_r - 1.0
    fr = (nr * a_re + abar_i * a_im) / den
    fi = (abar_i * a_re - nr * a_im) / den
    bbar_r = fr[..., None] * b_re - fi[..., None] * b_im
    bbar_i = fr[..., None] * b_im + fi[..., None] * b_re

    eye = jnp.eye(GROUPS_PER_TILE, dtype=f32)

    def in_tile(b):
        b = b.reshape(N_TILES, GROUPS_PER_TILE, SSM_STATE, SSM_GROUP)
        w = jnp.einsum('jgph,gk->jghkp', b, eye)
        return w.reshape(N_TILES, LANES, HALF)

    def out_tile(c):
        c = c.reshape(N_TILES, GROUPS_PER_TILE, SSM_GROUP, SSM_STATE)
        w = jnp.einsum('jghp,gk->jgpkh', c, eye)
        return w.reshape(N_TILES, HALF, LANES)

    wb = jnp.stack([in_tile(bbar_r).reshape(N_TILES, LANES, N_COLS, LANES),
                    in_tile(bbar_i).reshape(N_TILES, LANES, N_COLS, LANES)], axis=3)
    wb = wb.reshape(N_TILES, LANES, TILE_STATE)
    wc = jnp.stack([out_tile(c_re).reshape(N_TILES, N_COLS, LANES, LANES),
                    -out_tile(c_im).reshape(N_TILES, N_COLS, LANES, LANES)], axis=2)
    wc = wc.reshape(N_TILES, TILE_STATE, LANES)

    pr = abar_r.reshape(1, N_STATE)
    pi = abar_i.reshape(1, N_STATE)
    while pr.shape[0] < SEG_ROWS:
        tr, ti = _cmul(pr[-1:], pi[-1:], pr, pi)
        pr = jnp.concatenate([pr, tr], axis=0)
        pi = jnp.concatenate([pi, ti], axis=0)
    ones = jnp.ones((SUBLANES, 1), f32)
    arow = jnp.stack([ones * pr[0:1], ones * pi[0:1]])
    pw = jnp.stack([pr[:, None, :] * ones, pi[:, None, :] * ones])
    a1 = (pr[-1:], pi[-1:])
    a2 = _cmul(*a1, *a1)
    a4 = _cmul(*a2, *a2)
    row = jnp.arange(SUBLANES, dtype=jnp.int32)[:, None]
    seg = []
    for k, (qr, qi) in ((1, a1), (2, a2), (4, a4)):
        m = (row >= k).astype(f32)
        seg += [m * qr, m * qi]
    seg += [ones * a1[0], ones * a1[1]]
    seg = jnp.stack(seg)
    return wb.astype(jnp.bfloat16), wc.astype(jnp.bfloat16), arow, seg, pw


def _resident(shape):
    return pl.BlockSpec(shape, lambda i: (0,) * len(shape), pipeline_mode=pl.Buffered(1))


@jax.jit
def kernel(x, p, pre_norm, w_in, a_re, a_im, log_step, b_re, b_im, c_re, c_im, d_skip,
           w_glu, b_glu, w_ssm_out, conv_w, w_conv_out, w_o, post_norm,
           w_ple, w_ple_gate, ple_norm):
    bsz, seqlen, _ = x.shape
    depth = p.shape[0]
    assert bsz == 1 and depth == 1 and seqlen % ROW_BLOCK == 0
    bf16 = jnp.bfloat16
    tb = ROW_BLOCK

    wb, wc, arow, seg, pw = _ssm_tables(
        a_re[0], a_im[0], log_step[0], b_re[0], b_im[0], c_re[0], c_im[0])
    nb = seqlen // tb
    cur = lambda i: (jnp.minimum(i, nb - 1), 0)
    prev = lambda i: (jnp.maximum(i - 1, 0), 0)
    operands = [
        (x[0], pl.BlockSpec((tb, D_MODEL), cur)),
        (p[0, 0], pl.BlockSpec((tb, PLE_DIM), prev)),
        (pre_norm, _resident((1, D_MODEL))),
        (w_in[0].astype(bf16), _resident((D_MODEL, IN_WIDTH))),
        (wb, _resident(wb.shape)),
        (wc, _resident(wc.shape)),
        (arow, _resident(arow.shape)),
        (seg, _resident(seg.shape)),
        (pw, _resident(pw.shape)),
        (d_skip, _resident((1, SSM_WIDTH))),
        (w_glu[0].astype(bf16), _resident((SSM_WIDTH, SSM_WIDTH))),
        (b_glu, _resident((1, SSM_WIDTH))),
        (w_ssm_out[0].astype(bf16), _resident((SSM_WIDTH, D_MODEL))),
        (conv_w[0], _resident((CONV_K, CONV_WIDTH))),
        (w_conv_out[0].astype(bf16), _resident((CONV_WIDTH, D_MODEL))),
        (w_o[0].astype(bf16), _resident((D_MODEL, D_MODEL))),
        (post_norm, _resident((1, D_MODEL))),
        (w_ple[0].astype(bf16), _resident((PLE_DIM, D_MODEL))),
        (w_ple_gate[0].astype(bf16), _resident((D_MODEL, D_MODEL))),
        (ple_norm, _resident((1, D_MODEL))),
    ]
    out = pl.pallas_call(
        _block_kernel,
        grid=(nb + 1,),
        in_specs=[spec for _, spec in operands],
        out_specs=pl.BlockSpec((tb, D_MODEL), prev),
        out_shape=jax.ShapeDtypeStruct((seqlen, D_MODEL), jnp.float32),
        scratch_shapes=[
            pltpu.VMEM((2, SUBLANES, N_STATE), jnp.float32),
            pltpu.VMEM((SUBLANES, CONV_WIDTH), jnp.float32),
            pltpu.VMEM((N_TILES, tb, LANES), jnp.float32),
            pltpu.VMEM((N_TILES, tb, TILE_STATE), jnp.float32),
            pltpu.VMEM((N_TILES, tb, TILE_STATE), jnp.bfloat16),
            pltpu.VMEM((N_TILES, tb, LANES), jnp.float32),
            pltpu.VMEM((tb, SSM_WIDTH), jnp.float32),
            pltpu.VMEM((tb + SUBLANES, CONV_WIDTH), jnp.float32),
            pltpu.VMEM((tb, D_MODEL), jnp.bfloat16),
            pltpu.VMEM((tb, D_MODEL), jnp.float32),
        ],
        compiler_params=pltpu.CompilerParams(
            dimension_semantics=("arbitrary",),
            vmem_limit_bytes=VMEM_LIMIT_BYTES),
        name="hybrid_s5_shortconv_block",
    )(*[a for a, _ in operands])
    return out[None]
```

```python
import math

import jax
import jax.numpy as jnp
from jax import lax
from jax.experimental import pallas as pl
from jax.experimental.pallas import tpu as pltpu

D_MODEL = 1024
PLE_DIM = 256
SSM_WIDTH = 512
SSM_GROUP = 16
SSM_GROUPS = 32
SSM_STATE = 64
CONV_WIDTH = 1024
CONV_K = 3
RMS_EPS = 1e-6
IN_WIDTH = 7168

LANES = 128
MXU_N = 256
SUBLANES = 8
GROUPS_PER_TILE = LANES // SSM_GROUP
N_TILES = SSM_WIDTH // LANES
HALF = GROUPS_PER_TILE * SSM_STATE
TILE_STATE = 2 * HALF
N_COLS = HALF // LANES
N_STATE = SSM_GROUPS * SSM_STATE
ROW_BLOCK = 256
SEG_ROWS = ROW_BLOCK // SUBLANES
VMEM_LIMIT_BYTES = 56 * 1024 * 1024

O_U, O_GS, O_XC, O_BC, O_CC, O_GC, O_RS, O_RC = 0, 512, 1024, 2048, 3072, 4096, 5120, 6144


def _rms(v, gain):
    return v * lax.rsqrt(jnp.mean(v * v, axis=-1, keepdims=True) + RMS_EPS) * gain


def _sigmoid(v):
    return 1.0 / (1.0 + jnp.exp(-v))


def _gelu_tanh(v):
    c = math.sqrt(2.0 / math.pi)
    return v * (0.5 * (1.0 + jnp.tanh(c * (v + 0.044715 * (v * v * v)))))


def _cmul(ar, ai, xr, xi):
    return ar * xr - ai * xi, ar * xi + ai * xr


def _block_kernel(x_ref, p_ref, pre_ref, win_ref, wb_ref, wc_ref, arow_ref, seg_ref, pw_ref,
                  dskip_ref, wglu_ref, bglu_ref, wso_ref, convw_ref, wco_ref, wo_ref, post_ref,
                  wple_ref, wpg_ref, plen_ref, out_ref,
                  state_ref, h_ref, up_ref, v_ref, sloc_ref, s_ref, z_ref, gs_ref, mg_ref,
                  yp_ref, y_ref, y2_ref, o_ref, x1_ref, x1b_ref, e_ref, mprev_ref, xprev_ref):
    tb = x_ref.shape[0]
    ns = tb // SUBLANES
    f32 = jnp.float32
    bf16 = jnp.bfloat16

    @pl.when(pl.program_id(0) == 0)
    def _():
        state_ref[...] = jnp.zeros_like(state_ref)
        v_ref[tb:tb + SUBLANES, :] = jnp.zeros((SUBLANES, CONV_WIDTH), f32)
        mprev_ref[...] = jnp.zeros_like(mprev_ref)
        xprev_ref[...] = jnp.zeros_like(xprev_ref)

    def cols(n):
        return slice(n * MXU_N, (n + 1) * MXU_N)

    def tile_dot(lhs_ref, w_ref, off, n):
        return jnp.dot(lhs_ref[...], w_ref[:, off + n * MXU_N:off + (n + 1) * MXU_N],
                       preferred_element_type=f32)


    for n in range(D_MODEL // MXU_N):
        o_ref[:, cols(n)] = tile_dot(mprev_ref, wo_ref, 0, n)

    h_ref[...] = _rms(x_ref[...], pre_ref[...]).astype(bf16)

    for n in range(SSM_WIDTH // MXU_N):
        u = tile_dot(h_ref, win_ref, O_U, n)
        for c in range(MXU_N // LANES):
            t = n * (MXU_N // LANES) + c
            for j in range(SUBLANES):
                for i0 in range(0, ns, SUBLANES):
                    r0 = j * ns + i0
                    up_ref[t, pl.ds(SUBLANES * i0 + j, SUBLANES, stride=SUBLANES), :] = (
                        u[r0:r0 + SUBLANES, c * LANES:(c + 1) * LANES])

    v_ref[0:SUBLANES, :] = v_ref[tb:tb + SUBLANES, :]
    for n in range(CONV_WIDTH // MXU_N):
        v_ref[SUBLANES:SUBLANES + tb, cols(n)] = (
            tile_dot(h_ref, win_ref, O_CC, n) * tile_dot(h_ref, win_ref, O_XC, n))

    x1 = xprev_ref[...] + _rms(o_ref[...], post_ref[...])
    x1_ref[...] = x1
    x1b_ref[...] = x1.astype(bf16)

    row0 = lax.broadcasted_iota(jnp.int32, (SUBLANES, LANES), 0) == 0

    def scan_tile(t):
        ub = up_ref[t].astype(bf16)
        for q in range(N_COLS):
            bu = jnp.dot(ub, wb_ref[t, :, cols(q)],
                         preferred_element_type=f32)
            re = slice(q * 2 * LANES, q * 2 * LANES + LANES)
            im = slice(q * 2 * LANES + LANES, (q + 1) * 2 * LANES)
            sl = slice(t * HALF + q * LANES, t * HALF + (q + 1) * LANES)
            ar = arow_ref[0, :, sl]
            ai = arow_ref[1, :, sl]
            sr = bu[0:SUBLANES, 0:LANES]
            si = bu[0:SUBLANES, LANES:2 * LANES]
            sloc_ref[t, 0:SUBLANES, re] = sr
            sloc_ref[t, 0:SUBLANES, im] = si
            for i in range(1, ns):
                rows = slice(i * SUBLANES, (i + 1) * SUBLANES)
                mr, mi = _cmul(ar, ai, sr, si)
                sr = mr + bu[rows, 0:LANES]
                si = mi + bu[rows, LANES:2 * LANES]
                sloc_ref[t, rows, re] = sr
                sloc_ref[t, rows, im] = si
            gr = jnp.where(row0, state_ref[0, :, sl], pltpu.roll(sr, 1, 0))
            gi = jnp.where(row0, state_ref[1, :, sl], pltpu.roll(si, 1, 0))
            for m, k in enumerate((1, 2, 4)):
                fr, fi = _cmul(seg_ref[2 * m, :, sl], seg_ref[2 * m + 1, :, sl],
                               pltpu.roll(gr, k, 0), pltpu.roll(gi, k, 0))
                gr, gi = gr + fr, gi + fi
            fr, fi = _cmul(seg_ref[6, :, sl], seg_ref[7, :, sl], gr, gi)
            nr, ni = fr + sr, fi + si
            state_ref[0, :, sl] = jnp.broadcast_to(nr[SUBLANES - 1:SUBLANES, :], nr.shape)
            state_ref[1, :, sl] = jnp.broadcast_to(ni[SUBLANES - 1:SUBLANES, :], ni.shape)
            for m in range(ns // 2):
                parts_r, parts_i = [], []
                for i in (2 * m, 2 * m + 1):
                    rows = slice(i * SUBLANES, (i + 1) * SUBLANES)
                    fr, fi = _cmul(pw_ref[0, i, :, sl], pw_ref[1, i, :, sl], gr, gi)
                    parts_r.append(sloc_ref[t, rows, re] + fr)
                    parts_i.append(sloc_ref[t, rows, im] + fi)
                rows2 = slice(2 * m * SUBLANES, (2 * m + 2) * SUBLANES)
                s_ref[t, rows2, re] = jnp.concatenate(parts_r, axis=0).astype(bf16)
                s_ref[t, rows2, im] = jnp.concatenate(parts_i, axis=0).astype(bf16)

    scan_tile(0)

    for n in range(CONV_WIDTH // MXU_N):
        bc = tile_dot(h_ref, win_ref, O_BC, n)
        gc = tile_dot(h_ref, win_ref, O_GC, n)
        conv = (convw_ref[0:1, cols(n)] * v_ref[SUBLANES - 2:SUBLANES - 2 + tb, cols(n)]
                + convw_ref[1:2, cols(n)] * v_ref[SUBLANES - 1:SUBLANES - 1 + tb, cols(n)]
                + convw_ref[2:3, cols(n)] * v_ref[SUBLANES:SUBLANES + tb, cols(n)])
        z_ref[:, cols(n)] = (bc * conv * (gc * _sigmoid(gc))).astype(bf16)
        if n == 1:
            scan_tile(1)

    pb = p_ref[...].astype(bf16)
    for n in range(D_MODEL // MXU_N):
        eg = tile_dot(x1b_ref, wpg_ref, 0, n)
        ei = jnp.dot(pb, wple_ref[:, cols(n)], preferred_element_type=f32)
        e_ref[:, cols(n)] = ei * _sigmoid(eg)
        if n == 1:
            scan_tile(2)

    for n in range(SSM_WIDTH // MXU_N):
        gs = tile_dot(h_ref, win_ref, O_GS, n)
        gs_ref[:, cols(n)] = gs * _sigmoid(gs)
    scan_tile(3)

    out_ref[...] = x1_ref[...] + _rms(e_ref[...], plen_ref[...])

    def gated_b(n):
        rc = tile_dot(h_ref, win_ref, O_RC, n)
        yb = tile_dot(z_ref, wco_ref, 0, n)
        mg_ref[:, cols(n)] = _sigmoid(rc) * yb

    gated_b(0)
    gated_b(1)

    for t in range(N_TILES):
        yp_ref[t] = (jnp.dot(s_ref[t], wc_ref[t], preferred_element_type=f32)
                     + dskip_ref[:, t * LANES:(t + 1) * LANES] * up_ref[t])
    gated_b(2)
    gated_b(3)
    for t in range(N_TILES):
        for j in range(SUBLANES):
            for i0 in range(0, ns, SUBLANES):
                r0 = j * ns + i0
                y_ref[r0:r0 + SUBLANES, t * LANES:(t + 1) * LANES] = _gelu_tanh(
                    yp_ref[t, pl.ds(SUBLANES * i0 + j, SUBLANES, stride=SUBLANES), :])
    yb16 = y_ref[...].astype(bf16)
    for n in range(SSM_WIDTH // MXU_N):
        glu = jnp.dot(yb16, wglu_ref[:, cols(n)], preferred_element_type=f32)
        y2_ref[:, cols(n)] = (y_ref[:, cols(n)] * _sigmoid(glu + bglu_ref[:, cols(n)])
                              * gs_ref[:, cols(n)]).astype(bf16)

    for n in range(D_MODEL // MXU_N):
        rs = tile_dot(h_ref, win_ref, O_RS, n)
        ya = tile_dot(y2_ref, wso_ref, 0, n)
        mprev_ref[:, cols(n)] = (_sigmoid(rs) * ya + mg_ref[:, cols(n)]).astype(bf16)
    xprev_ref[...] = x_ref[...]


def _ssm_tables(a_re, a_im, log_step, b_re, b_im, c_re, c_im):
    f32 = jnp.float32
    dt = jnp.exp(log_step.astype(f32))[:, None]
    mag = jnp.exp(a_re * dt)
    abar_r = mag * jnp.cos(a_im * dt)
    abar_i = mag * jnp.sin(a_im * dt)
    den = a_re * a_re + a_im * a_im
    nr = abar_r - 1.0
    fr = (nr * a_re + abar_i * a_im) / den
    fi = (abar_i * a_re - nr * a_im) / den
    bbar_r = fr[..., None] * b_re - fi[..., None] * b_im
    bbar_i = fr[..., None] * b_im + fi[..., None] * b_re

    eye = jnp.eye(GROUPS_PER_TILE, dtype=f32)

    def in_tile(b):
        b = b.reshape(N_TILES, GROUPS_PER_TILE, SSM_STATE, SSM_GROUP)
        w = jnp.einsum('jgph,gk->jghkp', b, eye)
        return w.reshape(N_TILES, LANES, HALF)

    def out_tile(c):
        c = c.reshape(N_TILES, GROUPS_PER_TILE, SSM_GROUP, SSM_STATE)
        w = jnp.einsum('jghp,gk->jgpkh', c, eye)
        return w.reshape(N_TILES, HALF, LANES)

    wb = jnp.stack([in_tile(bbar_r).reshape(N_TILES, LANES, N_COLS, LANES),
                    in_tile(bbar_i).reshape(N_TILES, LANES, N_COLS, LANES)], axis=3)
    wb = wb.reshape(N_TILES, LANES, TILE_STATE)
    wc = jnp.stack([out_tile(c_re).reshape(N_TILES, N_COLS, LANES, LANES),
                    -out_tile(c_im).reshape(N_TILES, N_COLS, LANES, LANES)], axis=2)
    wc = wc.reshape(N_TILES, TILE_STATE, LANES)

    pr = abar_r.reshape(1, N_STATE)
    pi = abar_i.reshape(1, N_STATE)
    while pr.shape[0] < SEG_ROWS:
        tr, ti = _cmul(pr[-1:], pi[-1:], pr, pi)
        pr = jnp.concatenate([pr, tr], axis=0)
        pi = jnp.concatenate([pi, ti], axis=0)
    ones = jnp.ones((SUBLANES, 1), f32)
    arow = jnp.stack([ones * pr[0:1], ones * pi[0:1]])
    pw = jnp.stack([pr[:, None, :] * ones, pi[:, None, :] * ones])
    a1 = (pr[-1:], pi[-1:])
    a2 = _cmul(*a1, *a1)
    a4 = _cmul(*a2, *a2)
    row = jnp.arange(SUBLANES, dtype=jnp.int32)[:, None]
    seg = []
    for k, (qr, qi) in ((1, a1), (2, a2), (4, a4)):
        m = (row >= k).astype(f32)
        seg += [m * qr, m * qi]
    seg += [ones * a1[0], ones * a1[1]]
    seg = jnp.stack(seg)
    return wb.astype(jnp.bfloat16), wc.astype(jnp.bfloat16), arow, seg, pw


def _resident(shape):
    return pl.BlockSpec(shape, lambda i: (0,) * len(shape), pipeline_mode=pl.Buffered(1))


@jax.jit
def kernel(x, p, pre_norm, w_in, a_re, a_im, log_step, b_re, b_im, c_re, c_im, d_skip,
           w_glu, b_glu, w_ssm_out, conv_w, w_conv_out, w_o, post_norm,
           w_ple, w_ple_gate, ple_norm):
    bsz, seqlen, _ = x.shape
    depth = p.shape[0]
    assert bsz == 1 and depth == 1 and seqlen % ROW_BLOCK == 0
    bf16 = jnp.bfloat16
    tb = ROW_BLOCK

    wb, wc, arow, seg, pw = _ssm_tables(
        a_re[0], a_im[0], log_step[0], b_re[0], b_im[0], c_re[0], c_im[0])
    nb = seqlen // tb
    cur = lambda i: (jnp.minimum(i, nb - 1), 0)
    prev = lambda i: (jnp.maximum(i - 1, 0), 0)
    operands = [
        (x[0], pl.BlockSpec((tb, D_MODEL), cur)),
        (p[0, 0], pl.BlockSpec((tb, PLE_DIM), prev)),
        (pre_norm, _resident((1, D_MODEL))),
        (w_in[0].astype(bf16), _resident((D_MODEL, IN_WIDTH))),
        (wb, _resident(wb.shape)),
        (wc, _resident(wc.shape)),
        (arow, _resident(arow.shape)),
        (seg, _resident(seg.shape)),
        (pw, _resident(pw.shape)),
        (d_skip, _resident((1, SSM_WIDTH))),
        (w_glu[0].astype(bf16), _resident((SSM_WIDTH, SSM_WIDTH))),
        (b_glu, _resident((1, SSM_WIDTH))),
        (w_ssm_out[0].astype(bf16), _resident((SSM_WIDTH, D_MODEL))),
        (conv_w[0], _resident((CONV_K, CONV_WIDTH))),
        (w_conv_out[0].astype(bf16), _resident((CONV_WIDTH, D_MODEL))),
        (w_o[0].astype(bf16), _resident((D_MODEL, D_MODEL))),
        (post_norm, _resident((1, D_MODEL))),
        (w_ple[0].astype(bf16), _resident((PLE_DIM, D_MODEL))),
        (w_ple_gate[0].astype(bf16), _resident((D_MODEL, D_MODEL))),
        (ple_norm, _resident((1, D_MODEL))),
    ]
    out = pl.pallas_call(
        _block_kernel,
        grid=(nb + 1,),
        in_specs=[spec for _, spec in operands],
        out_specs=pl.BlockSpec((tb, D_MODEL), prev),
        out_shape=jax.ShapeDtypeStruct((seqlen, D_MODEL), jnp.float32),
        scratch_shapes=[
            pltpu.VMEM((2, SUBLANES, N_STATE), jnp.float32),
            pltpu.VMEM((tb, D_MODEL), jnp.bfloat16),
            pltpu.VMEM((N_TILES, tb, LANES), jnp.float32),
            pltpu.VMEM((tb + SUBLANES, CONV_WIDTH), jnp.float32),
            pltpu.VMEM((N_TILES, tb, TILE_STATE), jnp.float32),
            pltpu.VMEM((N_TILES, tb, TILE_STATE), jnp.bfloat16),
            pltpu.VMEM((tb, CONV_WIDTH), jnp.bfloat16),
            pltpu.VMEM((tb, SSM_WIDTH), jnp.float32),
            pltpu.VMEM((tb, D_MODEL), jnp.float32),
            pltpu.VMEM((N_TILES, tb, LANES), jnp.float32),
            pltpu.VMEM((tb, SSM_WIDTH), jnp.float32),
            pltpu.VMEM((tb, SSM_WIDTH), jnp.bfloat16),
            pltpu.VMEM((tb, D_MODEL), jnp.float32),
            pltpu.VMEM((tb, D_MODEL), jnp.float32),
            pltpu.VMEM((tb, D_MODEL), jnp.bfloat16),
            pltpu.VMEM((tb, D_MODEL), jnp.float32),
            pltpu.VMEM((tb, D_MODEL), jnp.bfloat16),
            pltpu.VMEM((tb, D_MODEL), jnp.float32),
        ],
        compiler_params=pltpu.CompilerParams(
            dimension_semantics=("arbitrary",),
            vmem_limit_bytes=VMEM_LIMIT_BYTES),
        name="hybrid_s5_shortconv_block",
    )(*[a for a, _ in operands])
    return out[None]
```

```python
import math

import jax
import jax.numpy as jnp
from jax import lax
from jax.experimental import pallas as pl
from jax.experimental.pallas import tpu as pltpu

D_MODEL = 1024
PLE_DIM = 256
SSM_WIDTH = 512
SSM_GROUP = 16
SSM_GROUPS = 32
SSM_STATE = 64
CONV_WIDTH = 1024
CONV_K = 3
RMS_EPS = 1e-6
IN_WIDTH = 7168

LANES = 128
MXU_N = 256
SUBLANES = 8
GROUPS_PER_TILE = LANES // SSM_GROUP
N_TILES = SSM_WIDTH // LANES
HALF = GROUPS_PER_TILE * SSM_STATE
TILE_STATE = 2 * HALF
N_COLS = HALF // LANES
N_STATE = SSM_GROUPS * SSM_STATE
ROW_BLOCK = 256
SEG_PAIRS = ROW_BLOCK // SUBLANES // 2
VMEM_LIMIT_BYTES = 56 * 1024 * 1024

O_U, O_GS, O_XC, O_BC, O_CC, O_GC, O_RS, O_RC = 0, 512, 1024, 2048, 3072, 4096, 5120, 6144


def _rms(v, gain):
    return v * lax.rsqrt(jnp.mean(v * v, axis=-1, keepdims=True) + RMS_EPS) * gain


def _sigmoid(v):
    return 1.0 / (1.0 + jnp.exp(-v))


def _gelu_tanh(v):
    c = math.sqrt(2.0 / math.pi)
    return v * (0.5 * (1.0 + jnp.tanh(c * (v + 0.044715 * (v * v * v)))))


def _cmul(ar, ai, xr, xi):
    return ar * xr - ai * xi, ar * xi + ai * xr


def _block_kernel(x_ref, p_ref, pre_ref, win_ref, wb_ref, wc_ref, arow_ref, seg_ref, pw_ref,
                  wglu_ref, bglu_ref, wso_ref, convw_ref, wco_ref, wo_ref, post_ref,
                  wple_ref, wpg_ref, plen_ref, out_ref,
                  state_ref, h_ref, un_ref, up_ref, v_ref, sloc_ref, s_ref, z_ref, gs_ref, mg_ref,
                  yp_ref, y_ref, y2_ref, o_ref, x1_ref, x1b_ref, e_ref, mprev_ref, xprev_ref):
    tb = x_ref.shape[0]
    ns = tb // SUBLANES
    npair = ns // 2
    f32 = jnp.float32
    bf16 = jnp.bfloat16

    @pl.when(pl.program_id(0) == 0)
    def _():
        state_ref[...] = jnp.zeros_like(state_ref)
        v_ref[tb:tb + SUBLANES, :] = jnp.zeros((SUBLANES, CONV_WIDTH), f32)
        mprev_ref[...] = jnp.zeros_like(mprev_ref)
        xprev_ref[...] = jnp.zeros_like(xprev_ref)

    def cols(n):
        return slice(n * MXU_N, (n + 1) * MXU_N)

    def tile_dot(lhs_ref, w_ref, off, n):
        return jnp.dot(lhs_ref[...], w_ref[:, off + n * MXU_N:off + (n + 1) * MXU_N],
                       preferred_element_type=f32)


    for n in range(D_MODEL // MXU_N):
        o_ref[:, cols(n)] = tile_dot(mprev_ref, wo_ref, 0, n)

    h_ref[...] = _rms(x_ref[...], pre_ref[...]).astype(bf16)

    for n in range(SSM_WIDTH // MXU_N):
        u = tile_dot(h_ref, win_ref, O_U, n)
        for c in range(MXU_N // LANES):
            un_ref[n * (MXU_N // LANES) + c] = u[:, c * LANES:(c + 1) * LANES]
    for t in range(N_TILES):
        for par in range(2):
            for j in range(SUBLANES):
                for mh in range(npair // SUBLANES):
                    up_ref[par, t, pl.ds(SUBLANES * SUBLANES * mh + j, SUBLANES,
                                         stride=SUBLANES), :] = (
                        un_ref[t, pl.ds(j * ns + 2 * SUBLANES * mh + par, SUBLANES, stride=2), :])

    v_ref[0:SUBLANES, :] = v_ref[tb:tb + SUBLANES, :]
    for n in range(CONV_WIDTH // MXU_N):
        v_ref[SUBLANES:SUBLANES + tb, cols(n)] = (
            tile_dot(h_ref, win_ref, O_CC, n) * tile_dot(h_ref, win_ref, O_XC, n))

    x1 = xprev_ref[...] + _rms(o_ref[...], post_ref[...])
    x1_ref[...] = x1
    x1b_ref[...] = x1.astype(bf16)

    row0 = lax.broadcasted_iota(jnp.int32, (SUBLANES, LANES), 0) == 0

    def scan_tile(t):
        ub = jnp.concatenate([up_ref[0, t], up_ref[1, t]], axis=1).astype(bf16)
        for q in range(N_COLS):
            bu = jnp.dot(ub, wb_ref[t, :, cols(q)],
                         preferred_element_type=f32)
            re = slice(q * 2 * LANES, q * 2 * LANES + LANES)
            im = slice(q * 2 * LANES + LANES, (q + 1) * 2 * LANES)
            sl = slice(t * HALF + q * LANES, t * HALF + (q + 1) * LANES)
            ar = arow_ref[0, :, sl]
            ai = arow_ref[1, :, sl]
            sr = bu[0:SUBLANES, 0:LANES]
            si = bu[0:SUBLANES, LANES:2 * LANES]
            for m in range(1, npair):
                rows = slice(m * SUBLANES, (m + 1) * SUBLANES)
                sloc_ref[t, rows, re] = sr
                sloc_ref[t, rows, im] = si
                mr, mi = _cmul(ar, ai, sr, si)
                sr = mr + bu[rows, 0:LANES]
                si = mi + bu[rows, LANES:2 * LANES]
            gr = jnp.where(row0, state_ref[0, :, sl], pltpu.roll(sr, 1, 0))
            gi = jnp.where(row0, state_ref[1, :, sl], pltpu.roll(si, 1, 0))
            for k, shift in enumerate((1, 2, 4)):
                fr, fi = _cmul(seg_ref[2 * k, :, sl], seg_ref[2 * k + 1, :, sl],
                               pltpu.roll(gr, shift, 0), pltpu.roll(gi, shift, 0))
                gr, gi = gr + fr, gi + fi
            fr, fi = _cmul(seg_ref[6, :, sl], seg_ref[7, :, sl], gr, gi)
            nr, ni = fr + sr, fi + si
            state_ref[0, :, sl] = jnp.broadcast_to(nr[SUBLANES - 1:SUBLANES, :], nr.shape)
            state_ref[1, :, sl] = jnp.broadcast_to(ni[SUBLANES - 1:SUBLANES, :], ni.shape)
            for m2 in range(npair // 2):
                parts_r, parts_i = [], []
                for m in (2 * m2, 2 * m2 + 1):
                    if m == 0:
                        parts_r.append(gr)
                        parts_i.append(gi)
                        continue
                    rows = slice(m * SUBLANES, (m + 1) * SUBLANES)
                    fr, fi = _cmul(pw_ref[0, m, :, sl], pw_ref[1, m, :, sl], gr, gi)
                    parts_r.append(sloc_ref[t, rows, re] + fr)
                    parts_i.append(sloc_ref[t, rows, im] + fi)
                rows2 = slice(2 * m2 * SUBLANES, (2 * m2 + 2) * SUBLANES)
                s_ref[t, rows2, re] = jnp.concatenate(parts_r, axis=0).astype(bf16)
                s_ref[t, rows2, im] = jnp.concatenate(parts_i, axis=0).astype(bf16)

    scan_tile(0)

    for n in range(CONV_WIDTH // MXU_N):
        bc = tile_dot(h_ref, win_ref, O_BC, n)
        gc = tile_dot(h_ref, win_ref, O_GC, n)
        conv = (convw_ref[0:1, cols(n)] * v_ref[SUBLANES - 2:SUBLANES - 2 + tb, cols(n)]
                + convw_ref[1:2, cols(n)] * v_ref[SUBLANES - 1:SUBLANES - 1 + tb, cols(n)]
                + convw_ref[2:3, cols(n)] * v_ref[SUBLANES:SUBLANES + tb, cols(n)])
        z_ref[:, cols(n)] = (bc * conv * (gc * _sigmoid(gc))).astype(bf16)
        if n == 1:
            scan_tile(1)

    pb = p_ref[...].astype(bf16)
    for n in range(D_MODEL // MXU_N):
        eg = tile_dot(x1b_ref, wpg_ref, 0, n)
        ei = jnp.dot(pb, wple_ref[:, cols(n)], preferred_element_type=f32)
        e_ref[:, cols(n)] = ei * _sigmoid(eg)
        if n == 1:
            scan_tile(2)

    for n in range(SSM_WIDTH // MXU_N):
        gs = tile_dot(h_ref, win_ref, O_GS, n)
        gs_ref[:, cols(n)] = gs * _sigmoid(gs)
    scan_tile(3)

    out_ref[...] = x1_ref[...] + _rms(e_ref[...], plen_ref[...])

    def gated_b(n):
        rc = tile_dot(h_ref, win_ref, O_RC, n)
        yb = tile_dot(z_ref, wco_ref, 0, n)
        mg_ref[:, cols(n)] = _sigmoid(rc) * yb

    gated_b(0)
    gated_b(1)

    for t in range(N_TILES):
        lhs = jnp.concatenate(
            [s_ref[t], up_ref[0, t].astype(bf16), up_ref[1, t].astype(bf16)], axis=1)
        yy = jnp.dot(lhs, wc_ref[t], preferred_element_type=f32)
        yp_ref[0, t] = yy[:, 0:LANES]
        yp_ref[1, t] = yy[:, LANES:2 * LANES]
    gated_b(2)
    gated_b(3)
    for t in range(N_TILES):
        for par in range(2):
            for j in range(SUBLANES):
                for mh in range(npair // SUBLANES):
                    y_ref[t, pl.ds(j * ns + 2 * SUBLANES * mh + par, SUBLANES, stride=2), :] = (
                        _gelu_tanh(yp_ref[par, t, pl.ds(SUBLANES * SUBLANES * mh + j, SUBLANES,
                                                        stride=SUBLANES), :]))
    y = jnp.concatenate([y_ref[t] for t in range(N_TILES)], axis=1)
    yb16 = y.astype(bf16)
    for n in range(SSM_WIDTH // MXU_N):
        glu = jnp.dot(yb16, wglu_ref[:, cols(n)], preferred_element_type=f32)
        y2_ref[:, cols(n)] = (y[:, cols(n)] * _sigmoid(glu + bglu_ref[:, cols(n)])
                              * gs_ref[:, cols(n)]).astype(bf16)

    for n in range(D_MODEL // MXU_N):
        rs = tile_dot(h_ref, win_ref, O_RS, n)
        ya = tile_dot(y2_ref, wso_ref, 0, n)
        mprev_ref[:, cols(n)] = (_sigmoid(rs) * ya + mg_ref[:, cols(n)]).astype(bf16)
    xprev_ref[...] = x_ref[...]


def _ssm_tables(a_re, a_im, log_step, b_re, b_im, c_re, c_im, d_skip):
    f32 = jnp.float32
    hi = lax.Precision.HIGHEST
    dt = jnp.exp(log_step.astype(f32))[:, None]
    mag = jnp.exp(a_re * dt)
    abar_r = mag * jnp.cos(a_im * dt)
    abar_i = mag * jnp.sin(a_im * dt)
    den = a_re * a_re + a_im * a_im
    nr = abar_r - 1.0
    fr = (nr * a_re + abar_i * a_im) / den
    fi = (abar_i * a_re - nr * a_im) / den
    bbar_r = fr[..., None] * b_re - fi[..., None] * b_im
    bbar_i = fr[..., None] * b_im + fi[..., None] * b_re
    ab_r, ab_i = _cmul(abar_r[..., None], abar_i[..., None], bbar_r, bbar_i)
    ca_r, ca_i = _cmul(abar_r[:, None, :], abar_i[:, None, :], c_re, c_im)
    caa_r, caa_i = _cmul(abar_r[:, None, :], abar_i[:, None, :], ca_r, ca_i)

    def re_prod(xr, xi, yr, yi):
        return (jnp.einsum('ghp,gpk->ghk', xr, yr, precision=hi)
                - jnp.einsum('ghp,gpk->ghk', xi, yi, precision=hi))

    cb_d = (re_prod(c_re, c_im, bbar_r, bbar_i)
            + jnp.eye(SSM_GROUP, dtype=f32) * d_skip.reshape(SSM_GROUPS, SSM_GROUP, 1))
    cab = re_prod(ca_r, ca_i, bbar_r, bbar_i)

    eye = jnp.eye(GROUPS_PER_TILE, dtype=f32)

    def in_tile(b):
        b = b.reshape(N_TILES, GROUPS_PER_TILE, SSM_STATE, SSM_GROUP)
        w = jnp.einsum('jgph,gk->jghkp', b, eye)
        return w.reshape(N_TILES, LANES, HALF)

    def out_tile(c):
        c = c.reshape(N_TILES, GROUPS_PER_TILE, SSM_GROUP, SSM_STATE)
        w = jnp.einsum('jghp,gk->jgpkh', c, eye)
        return w.reshape(N_TILES, HALF, LANES)

    def skip_tile(m):
        m = m.reshape(N_TILES, GROUPS_PER_TILE, SSM_GROUP, SSM_GROUP)
        w = jnp.einsum('jghk,gl->jgklh', m, eye)
        return w.reshape(N_TILES, LANES, LANES)

    def state_cols(w_r, w_i):
        w = jnp.stack([w_r.reshape(N_TILES, LANES, N_COLS, LANES),
                       w_i.reshape(N_TILES, LANES, N_COLS, LANES)], axis=3)
        return w.reshape(N_TILES, LANES, TILE_STATE)

    def state_rows(w_r, w_i):
        w = jnp.stack([w_r.reshape(N_TILES, N_COLS, LANES, LANES),
                       -w_i.reshape(N_TILES, N_COLS, LANES, LANES)], axis=2)
        return w.reshape(N_TILES, TILE_STATE, LANES)

    wb = jnp.concatenate([state_cols(in_tile(ab_r), in_tile(ab_i)),
                          state_cols(in_tile(bbar_r), in_tile(bbar_i))], axis=1)
    zero = jnp.zeros((N_TILES, LANES, LANES), f32)
    wc = jnp.concatenate([
        jnp.concatenate([state_rows(out_tile(ca_r), out_tile(ca_i)),
                         state_rows(out_tile(caa_r), out_tile(caa_i))], axis=2),
        jnp.concatenate([skip_tile(cb_d), skip_tile(cab)], axis=2),
        jnp.concatenate([zero, skip_tile(cb_d)], axis=2)], axis=1)

    a_r = abar_r.reshape(1, N_STATE)
    a_i = abar_i.reshape(1, N_STATE)
    q_r, q_i = _cmul(a_r, a_i, a_r, a_i)
    pr, pi = jnp.ones_like(q_r), jnp.zeros_like(q_i)
    step_r, step_i = q_r, q_i
    while pr.shape[0] < SEG_PAIRS:
        tr, ti = _cmul(step_r, step_i, pr, pi)
        pr = jnp.concatenate([pr, tr], axis=0)
        pi = jnp.concatenate([pi, ti], axis=0)
        step_r, step_i = _cmul(step_r, step_i, step_r, step_i)
    ones = jnp.ones((SUBLANES, 1), f32)
    arow = jnp.stack([ones * q_r, ones * q_i])
    pw = jnp.stack([pr[:, None, :] * ones, pi[:, None, :] * ones])
    a1 = (step_r, step_i)
    a2 = _cmul(*a1, *a1)
    a4 = _cmul(*a2, *a2)
    row = jnp.arange(SUBLANES, dtype=jnp.int32)[:, None]
    seg = []
    for k, (qr, qi) in ((1, a1), (2, a2), (4, a4)):
        m = (row >= k).astype(f32)
        seg += [m * qr, m * qi]
    seg += [ones * a1[0], ones * a1[1]]
    seg = jnp.stack(seg)
    return wb.astype(jnp.bfloat16), wc.astype(jnp.bfloat16), arow, seg, pw


def _resident(shape):
    return pl.BlockSpec(shape, lambda i: (0,) * len(shape), pipeline_mode=pl.Buffered(1))


@jax.jit
def kernel(x, p, pre_norm, w_in, a_re, a_im, log_step, b_re, b_im, c_re, c_im, d_skip,
           w_glu, b_glu, w_ssm_out, conv_w, w_conv_out, w_o, post_norm,
           w_ple, w_ple_gate, ple_norm):
    bsz, seqlen, _ = x.shape
    depth = p.shape[0]
    assert bsz == 1 and depth == 1 and seqlen % ROW_BLOCK == 0
    bf16 = jnp.bfloat16
    tb = ROW_BLOCK

    wb, wc, arow, seg, pw = _ssm_tables(
        a_re[0], a_im[0], log_step[0], b_re[0], b_im[0], c_re[0], c_im[0], d_skip[0])
    nb = seqlen // tb
    cur = lambda i: (jnp.minimum(i, nb - 1), 0)
    prev = lambda i: (jnp.maximum(i - 1, 0), 0)
    operands = [
        (x[0], pl.BlockSpec((tb, D_MODEL), cur)),
        (p[0, 0], pl.BlockSpec((tb, PLE_DIM), prev)),
        (pre_norm, _resident((1, D_MODEL))),
        (w_in[0].astype(bf16), _resident((D_MODEL, IN_WIDTH))),
        (wb, _resident(wb.shape)),
        (wc, _resident(wc.shape)),
        (arow, _resident(arow.shape)),
        (seg, _resident(seg.shape)),
        (pw, _resident(pw.shape)),
        (w_glu[0].astype(bf16), _resident((SSM_WIDTH, SSM_WIDTH))),
        (b_glu, _resident((1, SSM_WIDTH))),
        (w_ssm_out[0].astype(bf16), _resident((SSM_WIDTH, D_MODEL))),
        (conv_w[0], _resident((CONV_K, CONV_WIDTH))),
        (w_conv_out[0].astype(bf16), _resident((CONV_WIDTH, D_MODEL))),
        (w_o[0].astype(bf16), _resident((D_MODEL, D_MODEL))),
        (post_norm, _resident((1, D_MODEL))),
        (w_ple[0].astype(bf16), _resident((PLE_DIM, D_MODEL))),
        (w_ple_gate[0].astype(bf16), _resident((D_MODEL, D_MODEL))),
        (ple_norm, _resident((1, D_MODEL))),
    ]
    out = pl.pallas_call(
        _block_kernel,
        grid=(nb + 1,),
        in_specs=[spec for _, spec in operands],
        out_specs=pl.BlockSpec((tb, D_MODEL), prev),
        out_shape=jax.ShapeDtypeStruct((seqlen, D_MODEL), jnp.float32),
        scratch_shapes=[
            pltpu.VMEM((2, SUBLANES, N_STATE), jnp.float32),
            pltpu.VMEM((tb, D_MODEL), jnp.bfloat16),
            pltpu.VMEM((N_TILES, tb, LANES), jnp.float32),
            pltpu.VMEM((2, N_TILES, tb // 2, LANES), jnp.float32),
            pltpu.VMEM((tb + SUBLANES, CONV_WIDTH), jnp.float32),
            pltpu.VMEM((N_TILES, tb // 2, TILE_STATE), jnp.float32),
            pltpu.VMEM((N_TILES, tb // 2, TILE_STATE), jnp.bfloat16),
            pltpu.VMEM((tb, CONV_WIDTH), jnp.bfloat16),
            pltpu.VMEM((tb, SSM_WIDTH), jnp.float32),
            pltpu.VMEM((tb, D_MODEL), jnp.float32),
            pltpu.VMEM((2, N_TILES, tb // 2, LANES), jnp.float32),
            pltpu.VMEM((N_TILES, tb, LANES), jnp.float32),
            pltpu.VMEM((tb, SSM_WIDTH), jnp.bfloat16),
            pltpu.VMEM((tb, D_MODEL), jnp.float32),
            pltpu.VMEM((tb, D_MODEL), jnp.float32),
            pltpu.VMEM((tb, D_MODEL), jnp.bfloat16),
            pltpu.VMEM((tb, D_MODEL), jnp.float32),
            pltpu.VMEM((tb, D_MODEL), jnp.bfloat16),
            pltpu.VMEM((tb, D_MODEL), jnp.float32),
        ],
        compiler_params=pltpu.CompilerParams(
            dimension_semantics=("arbitrary",),
            vmem_limit_bytes=VMEM_LIMIT_BYTES),
        name="hybrid_s5_shortconv_block",
    )(*[a for a, _ in operands])
    return out[None]
```

```python
import math

import jax
import jax.numpy as jnp
from jax import lax
from jax.experimental import pallas as pl
from jax.experimental.pallas import tpu as pltpu

D_MODEL = 1024
PLE_DIM = 256
SSM_WIDTH = 512
SSM_GROUP = 16
SSM_GROUPS = 32
SSM_STATE = 64
CONV_WIDTH = 1024
CONV_K = 3
RMS_EPS = 1e-6
IN_WIDTH = 7168

LANES = 128
MXU_N = 256
SUBLANES = 8
GROUPS_PER_TILE = LANES // SSM_GROUP
N_TILES = SSM_WIDTH // LANES
HALF = GROUPS_PER_TILE * SSM_STATE
TILE_STATE = 2 * HALF
N_COLS = HALF // LANES
N_STATE = SSM_GROUPS * SSM_STATE
ROW_BLOCK = 256
SEG_PAIRS = ROW_BLOCK // SUBLANES // 2
VMEM_LIMIT_BYTES = 56 * 1024 * 1024

O_U, O_GS, O_XC, O_BC, O_CC, O_GC, O_RS, O_RC = 0, 512, 1024, 2048, 3072, 4096, 5120, 6144


def _rms(v, gain):
    return v * lax.rsqrt(jnp.mean(v * v, axis=-1, keepdims=True) + RMS_EPS) * gain


def _sigmoid(v):
    return 1.0 / (1.0 + jnp.exp(-v))


def _gelu_tanh(v):
    c = math.sqrt(2.0 / math.pi)
    return v * (0.5 * (1.0 + jnp.tanh(c * (v + 0.044715 * (v * v * v)))))


def _cmul(ar, ai, xr, xi):
    return ar * xr - ai * xi, ar * xi + ai * xr


def _block_kernel(x_ref, p_ref, pre_ref, win_ref, wb_ref, wc_ref, arow_ref, seg_ref, pw_ref,
                  wglu_ref, bglu_ref, wso_ref, convw_ref, wco_ref, wo_ref, post_ref,
                  wple_ref, wpg_ref, plen_ref, out_ref,
                  state_ref, h_ref, un_ref, up_ref, v_ref, sloc_ref, s_ref, z_ref, gs_ref, mg_ref,
                  yp_ref, y_ref, y2_ref, o_ref, x1_ref, x1b_ref, e_ref, mprev_ref, xprev_ref):
    tb = x_ref.shape[0]
    ns = tb // SUBLANES
    npair = ns // 2
    f32 = jnp.float32
    bf16 = jnp.bfloat16

    @pl.when(pl.program_id(0) == 0)
    def _():
        state_ref[...] = jnp.zeros_like(state_ref)
        v_ref[tb:tb + SUBLANES, :] = jnp.zeros((SUBLANES, CONV_WIDTH), f32)
        mprev_ref[...] = jnp.zeros_like(mprev_ref)
        xprev_ref[...] = jnp.zeros_like(xprev_ref)

    def cols(n):
        return slice(n * MXU_N, (n + 1) * MXU_N)

    def tile_dot(lhs_ref, w_ref, off, n):
        return jnp.dot(lhs_ref[...], w_ref[:, off + n * MXU_N:off + (n + 1) * MXU_N],
                       preferred_element_type=f32)


    for n in range(D_MODEL // MXU_N):
        o_ref[:, cols(n)] = tile_dot(mprev_ref, wo_ref, 0, n)

    h_ref[...] = _rms(x_ref[...], pre_ref[...]).astype(bf16)

    for n in range(SSM_WIDTH // MXU_N):
        u = tile_dot(h_ref, win_ref, O_U, n)
        for c in range(MXU_N // LANES):
            un_ref[n * (MXU_N // LANES) + c] = u[:, c * LANES:(c + 1) * LANES]
    for t in range(N_TILES):
        for par in range(2):
            for j in range(SUBLANES):
                for mh in range(npair // SUBLANES):
                    up_ref[par, t, pl.ds(SUBLANES * SUBLANES * mh + j, SUBLANES,
                                         stride=SUBLANES), :] = (
                        un_ref[t, pl.ds(j * ns + 2 * SUBLANES * mh + par, SUBLANES, stride=2), :])

    v_ref[0:SUBLANES, :] = v_ref[tb:tb + SUBLANES, :]
    for n in range(CONV_WIDTH // MXU_N):
        v_ref[SUBLANES:SUBLANES + tb, cols(n)] = (
            tile_dot(h_ref, win_ref, O_CC, n) * tile_dot(h_ref, win_ref, O_XC, n))

    x1 = xprev_ref[...] + _rms(o_ref[...], post_ref[...])
    x1_ref[...] = x1
    x1b_ref[...] = x1.astype(bf16)

    row0 = lax.broadcasted_iota(jnp.int32, (SUBLANES, LANES), 0) == 0

    def scan_tile(t):
        ub = jnp.concatenate([up_ref[0, t], up_ref[1, t]], axis=1).astype(bf16)
        for q in range(N_COLS):
            bu = jnp.dot(ub, wb_ref[t, :, cols(q)],
                         preferred_element_type=f32)
            re = slice(q * 2 * LANES, q * 2 * LANES + LANES)
            im = slice(q * 2 * LANES + LANES, (q + 1) * 2 * LANES)
            sl = slice(t * HALF + q * LANES, t * HALF + (q + 1) * LANES)
            ar = arow_ref[0, :, sl]
            ai = arow_ref[1, :, sl]
            sr = bu[0:SUBLANES, 0:LANES]
            si = bu[0:SUBLANES, LANES:2 * LANES]
            for m in range(1, npair):
                rows = slice(m * SUBLANES, (m + 1) * SUBLANES)
                sloc_ref[t, rows, re] = sr
                sloc_ref[t, rows, im] = si
                mr, mi = _cmul(ar, ai, sr, si)
                sr = mr + bu[rows, 0:LANES]
                si = mi + bu[rows, LANES:2 * LANES]
            gr = jnp.where(row0, state_ref[0, :, sl], pltpu.roll(sr, 1, 0))
            gi = jnp.where(row0, state_ref[1, :, sl], pltpu.roll(si, 1, 0))
            for k, shift in enumerate((1, 2, 4)):
                fr, fi = _cmul(seg_ref[2 * k, :, sl], seg_ref[2 * k + 1, :, sl],
                               pltpu.roll(gr, shift, 0), pltpu.roll(gi, shift, 0))
                gr, gi = gr + fr, gi + fi
            fr, fi = _cmul(seg_ref[6, :, sl], seg_ref[7, :, sl], gr, gi)
            nr, ni = fr + sr, fi + si
            state_ref[0, :, sl] = jnp.broadcast_to(nr[SUBLANES - 1:SUBLANES, :], nr.shape)
            state_ref[1, :, sl] = jnp.broadcast_to(ni[SUBLANES - 1:SUBLANES, :], ni.shape)
            for m2 in range(npair // 2):
                parts_r, parts_i = [], []
                for m in (2 * m2, 2 * m2 + 1):
                    if m == 0:
                        parts_r.append(gr)
                        parts_i.append(gi)
                        continue
                    rows = slice(m * SUBLANES, (m + 1) * SUBLANES)
                    fr, fi = _cmul(pw_ref[0, m, :, sl], pw_ref[1, m, :, sl], gr, gi)
                    parts_r.append(sloc_ref[t, rows, re] + fr)
                    parts_i.append(sloc_ref[t, rows, im] + fi)
                rows2 = slice(2 * m2 * SUBLANES, (2 * m2 + 2) * SUBLANES)
                s_ref[t, rows2, re] = jnp.concatenate(parts_r, axis=0).astype(bf16)
                s_ref[t, rows2, im] = jnp.concatenate(parts_i, axis=0).astype(bf16)

    scan_tile(0)

    for n in range(CONV_WIDTH // MXU_N):
        bc = tile_dot(h_ref, win_ref, O_BC, n)
        gc = tile_dot(h_ref, win_ref, O_GC, n)
        conv = (convw_ref[0:1, cols(n)] * v_ref[SUBLANES - 2:SUBLANES - 2 + tb, cols(n)]
                + convw_ref[1:2, cols(n)] * v_ref[SUBLANES - 1:SUBLANES - 1 + tb, cols(n)]
                + convw_ref[2:3, cols(n)] * v_ref[SUBLANES:SUBLANES + tb, cols(n)])
        z_ref[:, cols(n)] = (bc * conv * (gc * _sigmoid(gc))).astype(bf16)
        if n == 1:
            scan_tile(1)

    pb = p_ref[...].astype(bf16)
    for n in range(D_MODEL // MXU_N):
        eg = tile_dot(x1b_ref, wpg_ref, 0, n)
        ei = jnp.dot(pb, wple_ref[:, cols(n)], preferred_element_type=f32)
        e_ref[:, cols(n)] = ei * _sigmoid(eg)
        if n == 1:
            scan_tile(2)

    for n in range(SSM_WIDTH // MXU_N):
        gs = tile_dot(h_ref, win_ref, O_GS, n)
        gs_ref[:, cols(n)] = gs * _sigmoid(gs)
    scan_tile(3)

    out_ref[...] = x1_ref[...] + _rms(e_ref[...], plen_ref[...])

    def gated_b(n):
        rc = tile_dot(h_ref, win_ref, O_RC, n)
        yb = tile_dot(z_ref, wco_ref, 0, n)
        mg_ref[:, cols(n)] = _sigmoid(rc) * yb

    gated_b(0)
    gated_b(1)

    for t in range(N_TILES):
        lhs = jnp.concatenate(
            [s_ref[t], up_ref[0, t].astype(bf16), up_ref[1, t].astype(bf16)], axis=1)
        yy = jnp.dot(lhs, wc_ref[t], preferred_element_type=f32)
        yp_ref[0, t] = yy[:, 0:LANES]
        yp_ref[1, t] = yy[:, LANES:2 * LANES]
    gated_b(2)
    gated_b(3)
    for t in range(N_TILES):
        for par in range(2):
            for j in range(SUBLANES):
                for mh in range(npair // SUBLANES):
                    y_ref[t, pl.ds(j * ns + 2 * SUBLANES * mh + par, SUBLANES, stride=2), :] = (
                        _gelu_tanh(yp_ref[par, t, pl.ds(SUBLANES * SUBLANES * mh + j, SUBLANES,
                                                        stride=SUBLANES), :]))
    y = jnp.concatenate([y_ref[t] for t in range(N_TILES)], axis=1)
    yb16 = y.astype(bf16)
    for n in range(SSM_WIDTH // MXU_N):
        glu = jnp.dot(yb16, wglu_ref[:, cols(n)], preferred_element_type=f32)
        y2_ref[:, cols(n)] = (y[:, cols(n)] * _sigmoid(glu + bglu_ref[:, cols(n)])
                              * gs_ref[:, cols(n)]).astype(bf16)

    for n in range(D_MODEL // MXU_N):
        rs = tile_dot(h_ref, win_ref, O_RS, n)
        ya = tile_dot(y2_ref, wso_ref, 0, n)
        mprev_ref[:, cols(n)] = (_sigmoid(rs) * ya + mg_ref[:, cols(n)]).astype(bf16)
    xprev_ref[...] = x_ref[...]


def _ssm_tables(a_re, a_im, log_step, b_re, b_im, c_re, c_im, d_skip):
    f32 = jnp.float32
    G, P, H, T, GT = SSM_GROUPS, SSM_STATE, SSM_GROUP, N_TILES, GROUPS_PER_TILE
    dt = jnp.exp(log_step.astype(f32))[:, None]
    mag = jnp.exp(a_re * dt)
    abar_r = mag * jnp.cos(a_im * dt)
    abar_i = mag * jnp.sin(a_im * dt)
    den = a_re * a_re + a_im * a_im
    nr = abar_r - 1.0
    fr = (nr * a_re + abar_i * a_im) / den
    fi = (abar_i * a_re - nr * a_im) / den
    bbar_r = fr[..., None] * b_re - fi[..., None] * b_im
    bbar_i = fr[..., None] * b_im + fi[..., None] * b_re
    ab_r, ab_i = _cmul(abar_r[..., None], abar_i[..., None], bbar_r, bbar_i)
    ca_r, ca_i = _cmul(abar_r[:, None, :], abar_i[:, None, :], c_re, c_im)
    caa_r, caa_i = _cmul(abar_r[:, None, :], abar_i[:, None, :], ca_r, ca_i)

    def re_prod(xr, xi, yr, yi):
        yr_t = jnp.swapaxes(yr, 1, 2)[:, None]
        yi_t = jnp.swapaxes(yi, 1, 2)[:, None]
        return jnp.sum(xr[:, :, None, :] * yr_t - xi[:, :, None, :] * yi_t, axis=-1)

    cb_d = (re_prod(c_re, c_im, bbar_r, bbar_i)
            + jnp.eye(H, dtype=f32) * d_skip.reshape(G, H, 1))
    cab = re_prod(ca_r, ca_i, bbar_r, bbar_i)

    g_of = 2 * jnp.arange(N_COLS)[:, None] + jnp.arange(2)[None, :]
    own = (jnp.arange(GT)[:, None, None] == g_of[None]).astype(f32)
    same = jnp.eye(GT, dtype=f32)

    xin = jnp.stack([jnp.stack([ab_r, ab_i]), jnp.stack([bbar_r, bbar_i])])
    xin = xin.reshape(2, 2, T, GT, P, H).transpose(2, 0, 3, 5, 1, 4)
    wb = xin[:, :, :, :, None, :, None, :] * own[None, None, :, None, :, None, :, None]
    wb = wb.reshape(T, 2 * LANES, TILE_STATE)

    yout = jnp.stack([jnp.stack([ca_r, -ca_i]), jnp.stack([caa_r, -caa_i])])
    yout = yout.reshape(2, 2, T, GT, H, P).transpose(2, 1, 5, 0, 3, 4)
    own_t = own.transpose(1, 2, 0)
    w_state = (yout[:, None, :, None, :, :, :, :]
               * own_t[None, :, None, :, None, None, :, None])
    w_state = w_state.reshape(T, TILE_STATE, 2 * LANES)
    zero = jnp.zeros_like(cab)
    skip = jnp.stack([jnp.stack([cb_d, cab]), jnp.stack([zero, cb_d])])
    skip = skip.reshape(2, 2, T, GT, H, H).transpose(2, 0, 5, 1, 3, 4)
    w_skip = skip[:, :, None, :, :, :, :] * same[None, None, :, None, None, :, None]
    w_skip = w_skip.reshape(T, 2 * LANES, 2 * LANES)
    wc = jnp.concatenate([w_state, w_skip], axis=1)

    sq = [_cmul(abar_r.reshape(1, N_STATE), abar_i.reshape(1, N_STATE),
                abar_r.reshape(1, N_STATE), abar_i.reshape(1, N_STATE))]
    while len(sq) < 7:
        sq.append(_cmul(*sq[-1], *sq[-1]))
    rows_r, rows_i = [], []
    for m in range(SEG_PAIRS):
        acc = None
        for k in range(SEG_PAIRS.bit_length() - 1):
            if (m >> k) & 1:
                acc = sq[k] if acc is None else _cmul(*acc, *sq[k])
        if acc is None:
            acc = (jnp.ones((1, N_STATE), f32), jnp.zeros((1, N_STATE), f32))
        rows_r.append(acc[0])
        rows_i.append(acc[1])
    ones = jnp.ones((SUBLANES, 1), f32)
    arow = jnp.stack([ones * sq[0][0], ones * sq[0][1]])
    pw = jnp.stack([jnp.stack(rows_r) * ones, jnp.stack(rows_i) * ones])
    k0 = SEG_PAIRS.bit_length() - 1
    row = jnp.arange(SUBLANES, dtype=jnp.int32)[:, None]
    seg = []
    for k, (qr, qi) in ((1, sq[k0]), (2, sq[k0 + 1]), (4, sq[k0 + 2])):
        m = (row >= k).astype(f32)
        seg += [m * qr, m * qi]
    seg += [ones * sq[k0][0], ones * sq[k0][1]]
    seg = jnp.stack(seg)
    return wb.astype(jnp.bfloat16), wc.astype(jnp.bfloat16), arow, seg, pw


def _resident(shape):
    return pl.BlockSpec(shape, lambda i: (0,) * len(shape), pipeline_mode=pl.Buffered(1))


@jax.jit
def kernel(x, p, pre_norm, w_in, a_re, a_im, log_step, b_re, b_im, c_re, c_im, d_skip,
           w_glu, b_glu, w_ssm_out, conv_w, w_conv_out, w_o, post_norm,
           w_ple, w_ple_gate, ple_norm):
    bsz, seqlen, _ = x.shape
    depth = p.shape[0]
    assert bsz == 1 and depth == 1 and seqlen % ROW_BLOCK == 0
    bf16 = jnp.bfloat16
    tb = ROW_BLOCK

    wb, wc, arow, seg, pw = _ssm_tables(
        a_re[0], a_im[0], log_step[0], b_re[0], b_im[0], c_re[0], c_im[0], d_skip[0])
    nb = seqlen // tb
    cur = lambda i: (jnp.minimum(i, nb - 1), 0)
    prev = lambda i: (jnp.maximum(i - 1, 0), 0)
    operands = [
        (x[0], pl.BlockSpec((tb, D_MODEL), cur)),
        (p[0, 0], pl.BlockSpec((tb, PLE_DIM), prev)),
        (pre_norm, _resident((1, D_MODEL))),
        (w_in[0].astype(bf16), _resident((D_MODEL, IN_WIDTH))),
        (wb, _resident(wb.shape)),
        (wc, _resident(wc.shape)),
        (arow, _resident(arow.shape)),
        (seg, _resident(seg.shape)),
        (pw, _resident(pw.shape)),
        (w_glu[0].astype(bf16), _resident((SSM_WIDTH, SSM_WIDTH))),
        (b_glu, _resident((1, SSM_WIDTH))),
        (w_ssm_out[0].astype(bf16), _resident((SSM_WIDTH, D_MODEL))),
        (conv_w[0], _resident((CONV_K, CONV_WIDTH))),
        (w_conv_out[0].astype(bf16), _resident((CONV_WIDTH, D_MODEL))),
        (w_o[0].astype(bf16), _resident((D_MODEL, D_MODEL))),
        (post_norm, _resident((1, D_MODEL))),
        (w_ple[0].astype(bf16), _resident((PLE_DIM, D_MODEL))),
        (w_ple_gate[0].astype(bf16), _resident((D_MODEL, D_MODEL))),
        (ple_norm, _resident((1, D_MODEL))),
    ]
    out = pl.pallas_call(
        _block_kernel,
        grid=(nb + 1,),
        in_specs=[spec for _, spec in operands],
        out_specs=pl.BlockSpec((tb, D_MODEL), prev),
        out_shape=jax.ShapeDtypeStruct((seqlen, D_MODEL), jnp.float32),
        scratch_shapes=[
            pltpu.VMEM((2, SUBLANES, N_STATE), jnp.float32),
            pltpu.VMEM((tb, D_MODEL), jnp.bfloat16),
            pltpu.VMEM((N_TILES, tb, LANES), jnp.float32),
            pltpu.VMEM((2, N_TILES, tb // 2, LANES), jnp.float32),
            pltpu.VMEM((tb + SUBLANES, CONV_WIDTH), jnp.float32),
            pltpu.VMEM((N_TILES, tb // 2, TILE_STATE), jnp.float32),
            pltpu.VMEM((N_TILES, tb // 2, TILE_STATE), jnp.bfloat16),
            pltpu.VMEM((tb, CONV_WIDTH), jnp.bfloat16),
            pltpu.VMEM((tb, SSM_WIDTH), jnp.float32),
            pltpu.VMEM((tb, D_MODEL), jnp.float32),
            pltpu.VMEM((2, N_TILES, tb // 2, LANES), jnp.float32),
            pltpu.VMEM((N_TILES, tb, LANES), jnp.float32),
            pltpu.VMEM((tb, SSM_WIDTH), jnp.bfloat16),
            pltpu.VMEM((tb, D_MODEL), jnp.float32),
            pltpu.VMEM((tb, D_MODEL), jnp.float32),
            pltpu.VMEM((tb, D_MODEL), jnp.bfloat16),
            pltpu.VMEM((tb, D_MODEL), jnp.float32),
            pltpu.VMEM((tb, D_MODEL), jnp.bfloat16),
            pltpu.VMEM((tb, D_MODEL), jnp.float32),
        ],
        compiler_params=pltpu.CompilerParams(
            dimension_semantics=("arbitrary",),
            vmem_limit_bytes=VMEM_LIMIT_BYTES),
        name="hybrid_s5_shortconv_block",
    )(*[a for a, _ in operands])
    return out[None]
```

```python
import math

import jax
import jax.numpy as jnp
from jax import lax
from jax.experimental import pallas as pl
from jax.experimental.pallas import tpu as pltpu

D_MODEL = 1024
PLE_DIM = 256
SSM_WIDTH = 512
SSM_GROUP = 16
SSM_GROUPS = 32
SSM_STATE = 64
CONV_WIDTH = 1024
CONV_K = 3
RMS_EPS = 1e-6
IN_WIDTH = 7168

LANES = 128
MXU_N = 256
SUBLANES = 8
GROUPS_PER_TILE = LANES // SSM_GROUP
N_TILES = SSM_WIDTH // LANES
HALF = GROUPS_PER_TILE * SSM_STATE
TILE_STATE = 2 * HALF
N_COLS = HALF // LANES
N_STATE = SSM_GROUPS * SSM_STATE
ROW_BLOCK = 256
SEG_PAIRS = ROW_BLOCK // SUBLANES // 2
VMEM_LIMIT_BYTES = 56 * 1024 * 1024

O_U, O_GS, O_XC, O_BC, O_CC, O_GC, O_RS, O_RC = 0, 512, 1024, 2048, 3072, 4096, 5120, 6144


def _rms(v, gain):
    return v * lax.rsqrt(jnp.mean(v * v, axis=-1, keepdims=True) + RMS_EPS) * gain


def _sigmoid(v):
    return 1.0 / (1.0 + jnp.exp(-v))


def _gelu_tanh(v):
    c = math.sqrt(2.0 / math.pi)
    return v * (0.5 * (1.0 + jnp.tanh(c * (v + 0.044715 * (v * v * v)))))


def _cmul(ar, ai, xr, xi):
    return ar * xr - ai * xi, ar * xi + ai * xr


def _expand_tables(xb_ref, yc_ref, sk_ref, a_ref, wb_ref, wc_ref, arow_ref, seg_ref, pw_ref):
    f32 = jnp.float32
    bf16 = jnp.bfloat16
    gmask = GROUPS_PER_TILE - 1

    def iota(shape, dim):
        return lax.broadcasted_iota(jnp.int32, shape, dim)

    def onehot(cond):
        return jnp.where(cond, 1.0, 0.0).astype(bf16)

    r = iota((LANES, TILE_STATE), 0)
    c = iota((LANES, TILE_STATE), 1)
    rep_b = onehot(((c & (SSM_STATE - 1)) == (r & (SSM_STATE - 1))) & (((c >> 7) & 1) == (r >> 6)))
    r = iota((2 * LANES, TILE_STATE), 0)
    c = iota((2 * LANES, TILE_STATE), 1)
    own_b = ((r >> 4) & gmask) == (((c >> 8) << 1) | ((c >> 6) & 1))
    r = iota((2 * SSM_GROUP, 2 * LANES), 0)
    c = iota((2 * SSM_GROUP, 2 * LANES), 1)
    rep_c = onehot(((c >> 7) == (r >> 4)) & ((c & (SSM_GROUP - 1)) == (r & (SSM_GROUP - 1))))
    r = iota((TILE_STATE, 2 * LANES), 0)
    c = iota((TILE_STATE, 2 * LANES), 1)
    own_c = (((r >> 8) << 1) | ((r >> 6) & 1)) == ((c >> 4) & gmask)
    r = iota((2 * LANES, 2 * LANES), 0)
    c = iota((2 * LANES, 2 * LANES), 1)
    own_s = ((r >> 4) & gmask) == ((c >> 4) & gmask)
    for t in range(N_TILES):
        wide = jnp.dot(xb_ref[t].astype(bf16), rep_b, preferred_element_type=f32)
        wb_ref[t] = jnp.where(own_b, wide, 0.0).astype(bf16)
        wide = jnp.dot(yc_ref[t].astype(bf16), rep_c, preferred_element_type=f32)
        wc_ref[t, 0:TILE_STATE, :] = jnp.where(own_c, wide, 0.0).astype(bf16)
        wide = jnp.dot(sk_ref[t].astype(bf16), rep_c, preferred_element_type=f32)
        wc_ref[t, TILE_STATE:TILE_STATE + 2 * LANES, :] = jnp.where(own_s, wide, 0.0).astype(bf16)

    npair = pw_ref.shape[1]
    nbits = npair.bit_length() - 1
    sq = [_cmul(a_ref[0:1, :], a_ref[1:2, :], a_ref[0:1, :], a_ref[1:2, :])]
    while len(sq) < nbits + 3:
        sq.append(_cmul(*sq[-1], *sq[-1]))
    shape = (SUBLANES, N_STATE)
    arow_ref[0] = jnp.broadcast_to(sq[0][0], shape)
    arow_ref[1] = jnp.broadcast_to(sq[0][1], shape)
    pw_ref[0, 0] = jnp.ones(shape, f32)
    pw_ref[1, 0] = jnp.zeros(shape, f32)
    for m in range(1, npair):
        acc = None
        for k in range(nbits):
            if (m >> k) & 1:
                acc = sq[k] if acc is None else _cmul(*acc, *sq[k])
        pw_ref[0, m] = jnp.broadcast_to(acc[0], shape)
        pw_ref[1, m] = jnp.broadcast_to(acc[1], shape)
    row = iota(shape, 0)
    for k, shift in enumerate((1, 2, 4)):
        seg_ref[2 * k] = jnp.where(row >= shift, jnp.broadcast_to(sq[nbits + k][0], shape), 0.0)
        seg_ref[2 * k + 1] = jnp.where(row >= shift, jnp.broadcast_to(sq[nbits + k][1], shape), 0.0)
    seg_ref[6] = jnp.broadcast_to(sq[nbits][0], shape)
    seg_ref[7] = jnp.broadcast_to(sq[nbits][1], shape)


def _block_kernel(x_ref, p_ref, pre_ref, win_ref, xb_ref, yc_ref, sk_ref, a_ref,
                  wglu_ref, bglu_ref, wso_ref, convw_ref, wco_ref, wo_ref, post_ref,
                  wple_ref, wpg_ref, plen_ref, out_ref,
                  wb_ref, wc_ref, arow_ref, seg_ref, pw_ref, state_ref, h_ref, un_ref, up_ref, v_ref, sloc_ref, s_ref, z_ref, gs_ref, mg_ref,
                  yp_ref, y_ref, y2_ref, o_ref, x1_ref, x1b_ref, e_ref, mprev_ref, xprev_ref):
    tb = x_ref.shape[0]
    ns = tb // SUBLANES
    npair = ns // 2
    f32 = jnp.float32
    bf16 = jnp.bfloat16

    @pl.when(pl.program_id(0) == 0)
    def _():
        _expand_tables(xb_ref, yc_ref, sk_ref, a_ref, wb_ref, wc_ref, arow_ref, seg_ref, pw_ref)
        state_ref[...] = jnp.zeros_like(state_ref)
        v_ref[tb:tb + SUBLANES, :] = jnp.zeros((SUBLANES, CONV_WIDTH), f32)
        mprev_ref[...] = jnp.zeros_like(mprev_ref)
        xprev_ref[...] = jnp.zeros_like(xprev_ref)

    def cols(n):
        return slice(n * MXU_N, (n + 1) * MXU_N)

    def tile_dot(lhs_ref, w_ref, off, n):
        return jnp.dot(lhs_ref[...], w_ref[:, off + n * MXU_N:off + (n + 1) * MXU_N],
                       preferred_element_type=f32)


    for n in range(D_MODEL // MXU_N):
        o_ref[:, cols(n)] = tile_dot(mprev_ref, wo_ref, 0, n)

    h_ref[...] = _rms(x_ref[...], pre_ref[...]).astype(bf16)

    for n in range(SSM_WIDTH // MXU_N):
        u = tile_dot(h_ref, win_ref, O_U, n)
        for c in range(MXU_N // LANES):
            un_ref[n * (MXU_N // LANES) + c] = u[:, c * LANES:(c + 1) * LANES]
    for t in range(N_TILES):
        for par in range(2):
            for j in range(SUBLANES):
                for mh in range(npair // SUBLANES):
                    up_ref[par, t, pl.ds(SUBLANES * SUBLANES * mh + j, SUBLANES,
                                         stride=SUBLANES), :] = (
                        un_ref[t, pl.ds(j * ns + 2 * SUBLANES * mh + par, SUBLANES, stride=2), :])

    v_ref[0:SUBLANES, :] = v_ref[tb:tb + SUBLANES, :]
    for n in range(CONV_WIDTH // MXU_N):
        v_ref[SUBLANES:SUBLANES + tb, cols(n)] = (
            tile_dot(h_ref, win_ref, O_CC, n) * tile_dot(h_ref, win_ref, O_XC, n))

    x1 = xprev_ref[...] + _rms(o_ref[...], post_ref[...])
    x1_ref[...] = x1
    x1b_ref[...] = x1.astype(bf16)

    row0 = lax.broadcasted_iota(jnp.int32, (SUBLANES, LANES), 0) == 0

    def scan_tile(t):
        ub = jnp.concatenate([up_ref[0, t], up_ref[1, t]], axis=1).astype(bf16)
        for q in range(N_COLS):
            bu = jnp.dot(ub, wb_ref[t, :, cols(q)],
                         preferred_element_type=f32)
            re = slice(q * 2 * LANES, q * 2 * LANES + LANES)
            im = slice(q * 2 * LANES + LANES, (q + 1) * 2 * LANES)
            sl = slice(t * HALF + q * LANES, t * HALF + (q + 1) * LANES)
            ar = arow_ref[0, :, sl]
            ai = arow_ref[1, :, sl]
            sr = bu[0:SUBLANES, 0:LANES]
            si = bu[0:SUBLANES, LANES:2 * LANES]
            for m in range(1, npair):
                rows = slice(m * SUBLANES, (m + 1) * SUBLANES)
                sloc_ref[t, rows, re] = sr
                sloc_ref[t, rows, im] = si
                mr, mi = _cmul(ar, ai, sr, si)
                sr = mr + bu[rows, 0:LANES]
                si = mi + bu[rows, LANES:2 * LANES]
            gr = jnp.where(row0, state_ref[0, :, sl], pltpu.roll(sr, 1, 0))
            gi = jnp.where(row0, state_ref[1, :, sl], pltpu.roll(si, 1, 0))
            for k, shift in enumerate((1, 2, 4)):
                fr, fi = _cmul(seg_ref[2 * k, :, sl], seg_ref[2 * k + 1, :, sl],
                               pltpu.roll(gr, shift, 0), pltpu.roll(gi, shift, 0))
                gr, gi = gr + fr, gi + fi
            fr, fi = _cmul(seg_ref[6, :, sl], seg_ref[7, :, sl], gr, gi)
            nr, ni = fr + sr, fi + si
            state_ref[0, :, sl] = jnp.broadcast_to(nr[SUBLANES - 1:SUBLANES, :], nr.shape)
            state_ref[1, :, sl] = jnp.broadcast_to(ni[SUBLANES - 1:SUBLANES, :], ni.shape)
            for m2 in range(npair // 2):
                parts_r, parts_i = [], []
                for m in (2 * m2, 2 * m2 + 1):
                    if m == 0:
                        parts_r.append(gr)
                        parts_i.append(gi)
                        continue
                    rows = slice(m * SUBLANES, (m + 1) * SUBLANES)
                    fr, fi = _cmul(pw_ref[0, m, :, sl], pw_ref[1, m, :, sl], gr, gi)
                    parts_r.append(sloc_ref[t, rows, re] + fr)
                    parts_i.append(sloc_ref[t, rows, im] + fi)
                rows2 = slice(2 * m2 * SUBLANES, (2 * m2 + 2) * SUBLANES)
                s_ref[t, rows2, re] = jnp.concatenate(parts_r, axis=0).astype(bf16)
                s_ref[t, rows2, im] = jnp.concatenate(parts_i, axis=0).astype(bf16)

    scan_tile(0)

    for n in range(CONV_WIDTH // MXU_N):
        bc = tile_dot(h_ref, win_ref, O_BC, n)
        gc = tile_dot(h_ref, win_ref, O_GC, n)
        conv = (convw_ref[0:1, cols(n)] * v_ref[SUBLANES - 2:SUBLANES - 2 + tb, cols(n)]
                + convw_ref[1:2, cols(n)] * v_ref[SUBLANES - 1:SUBLANES - 1 + tb, cols(n)]
                + convw_ref[2:3, cols(n)] * v_ref[SUBLANES:SUBLANES + tb, cols(n)])
        z_ref[:, cols(n)] = (bc * conv * (gc * _sigmoid(gc))).astype(bf16)
        if n == 1:
            scan_tile(1)

    pb = p_ref[...].astype(bf16)
    for n in range(D_MODEL // MXU_N):
        eg = tile_dot(x1b_ref, wpg_ref, 0, n)
        ei = jnp.dot(pb, wple_ref[:, cols(n)], preferred_element_type=f32)
        e_ref[:, cols(n)] = ei * _sigmoid(eg)
        if n == 1:
            scan_tile(2)

    for n in range(SSM_WIDTH // MXU_N):
        gs = tile_dot(h_ref, win_ref, O_GS, n)
        gs_ref[:, cols(n)] = gs * _sigmoid(gs)
    scan_tile(3)

    out_ref[...] = x1_ref[...] + _rms(e_ref[...], plen_ref[...])

    def gated_b(n):
        rc = tile_dot(h_ref, win_ref, O_RC, n)
        yb = tile_dot(z_ref, wco_ref, 0, n)
        mg_ref[:, cols(n)] = _sigmoid(rc) * yb

    gated_b(0)
    gated_b(1)

    for t in range(N_TILES):
        lhs = jnp.concatenate(
            [s_ref[t], up_ref[0, t].astype(bf16), up_ref[1, t].astype(bf16)], axis=1)
        yy = jnp.dot(lhs, wc_ref[t], preferred_element_type=f32)
        yp_ref[0, t] = yy[:, 0:LANES]
        yp_ref[1, t] = yy[:, LANES:2 * LANES]
    gated_b(2)
    gated_b(3)
    for t in range(N_TILES):
        for par in range(2):
            for j in range(SUBLANES):
                for mh in range(npair // SUBLANES):
                    y_ref[t, pl.ds(j * ns + 2 * SUBLANES * mh + par, SUBLANES, stride=2), :] = (
                        _gelu_tanh(yp_ref[par, t, pl.ds(SUBLANES * SUBLANES * mh + j, SUBLANES,
                                                        stride=SUBLANES), :]))
    y = jnp.concatenate([y_ref[t] for t in range(N_TILES)], axis=1)
    yb16 = y.astype(bf16)
    for n in range(SSM_WIDTH // MXU_N):
        glu = jnp.dot(yb16, wglu_ref[:, cols(n)], preferred_element_type=f32)
        y2_ref[:, cols(n)] = (y[:, cols(n)] * _sigmoid(glu + bglu_ref[:, cols(n)])
                              * gs_ref[:, cols(n)]).astype(bf16)

    for n in range(D_MODEL // MXU_N):
        rs = tile_dot(h_ref, win_ref, O_RS, n)
        ya = tile_dot(y2_ref, wso_ref, 0, n)
        mprev_ref[:, cols(n)] = (_sigmoid(rs) * ya + mg_ref[:, cols(n)]).astype(bf16)
    xprev_ref[...] = x_ref[...]


def _ssm_params(a_re, a_im, log_step, b_re, b_im, c_re, c_im, d_skip):
    f32 = jnp.float32
    G, P, H, T, GT = SSM_GROUPS, SSM_STATE, SSM_GROUP, N_TILES, GROUPS_PER_TILE
    dt = jnp.exp(log_step.astype(f32))[:, None]
    mag = jnp.exp(a_re * dt)
    abar_r = mag * jnp.cos(a_im * dt)
    abar_i = mag * jnp.sin(a_im * dt)
    den = a_re * a_re + a_im * a_im
    nr = abar_r - 1.0
    fr = (nr * a_re + abar_i * a_im) / den
    fi = (abar_i * a_re - nr * a_im) / den
    bbar_r = fr[..., None] * b_re - fi[..., None] * b_im
    bbar_i = fr[..., None] * b_im + fi[..., None] * b_re
    ab_r, ab_i = _cmul(abar_r[..., None], abar_i[..., None], bbar_r, bbar_i)
    ca_r, ca_i = _cmul(abar_r[:, None, :], abar_i[:, None, :], c_re, c_im)
    caa_r, caa_i = _cmul(abar_r[:, None, :], abar_i[:, None, :], ca_r, ca_i)

    def re_prod(xr, xi, yr, yi):
        yr_t = jnp.swapaxes(yr, 1, 2)[:, None]
        yi_t = jnp.swapaxes(yi, 1, 2)[:, None]
        return jnp.sum(xr[:, :, None, :] * yr_t - xi[:, :, None, :] * yi_t, axis=-1)

    cb_d = (re_prod(c_re, c_im, bbar_r, bbar_i)
            + jnp.eye(H, dtype=f32) * d_skip.reshape(G, H, 1))
    cab = re_prod(ca_r, ca_i, bbar_r, bbar_i)

    xb = jnp.stack([jnp.stack([ab_r, ab_i]), jnp.stack([bbar_r, bbar_i])])
    xb = xb.reshape(2, 2, T, GT, P, H).transpose(2, 0, 3, 5, 1, 4).reshape(T, 2 * LANES, 2 * P)
    yc = jnp.stack([jnp.stack([ca_r, -ca_i]), jnp.stack([caa_r, -caa_i])])
    yc = yc.reshape(2, 2, T, N_COLS, 2, H, P).transpose(2, 3, 1, 4, 6, 0, 5)
    yc = yc.reshape(T, TILE_STATE, 2 * H)
    zero = jnp.zeros_like(cab)
    sk = jnp.stack([jnp.stack([cb_d, cab]), jnp.stack([zero, cb_d])])
    sk = sk.reshape(2, 2, T, GT, H, H).transpose(2, 0, 3, 5, 1, 4).reshape(T, 2 * LANES, 2 * H)
    a = jnp.stack([abar_r.reshape(N_STATE), abar_i.reshape(N_STATE)])
    return xb, yc, sk, a


def _resident(shape):
    return pl.BlockSpec(shape, lambda i: (0,) * len(shape), pipeline_mode=pl.Buffered(1))


@jax.jit
def kernel(x, p, pre_norm, w_in, a_re, a_im, log_step, b_re, b_im, c_re, c_im, d_skip,
           w_glu, b_glu, w_ssm_out, conv_w, w_conv_out, w_o, post_norm,
           w_ple, w_ple_gate, ple_norm):
    bsz, seqlen, _ = x.shape
    depth = p.shape[0]
    assert bsz == 1 and depth == 1 and seqlen % ROW_BLOCK == 0
    bf16 = jnp.bfloat16
    tb = ROW_BLOCK

    xb, yc, sk, a_bar = _ssm_params(
        a_re[0], a_im[0], log_step[0], b_re[0], b_im[0], c_re[0], c_im[0], d_skip[0])
    nb = seqlen // tb
    cur = lambda i: (jnp.minimum(i, nb - 1), 0)
    prev = lambda i: (jnp.maximum(i - 1, 0), 0)
    operands = [
        (x[0], pl.BlockSpec((tb, D_MODEL), cur)),
        (p[0, 0], pl.BlockSpec((tb, PLE_DIM), prev)),
        (pre_norm, _resident((1, D_MODEL))),
        (w_in[0].astype(bf16), _resident((D_MODEL, IN_WIDTH))),
        (xb, _resident(xb.shape)),
        (yc, _resident(yc.shape)),
        (sk, _resident(sk.shape)),
        (a_bar, _resident(a_bar.shape)),
        (w_glu[0].astype(bf16), _resident((SSM_WIDTH, SSM_WIDTH))),
        (b_glu, _resident((1, SSM_WIDTH))),
        (w_ssm_out[0].astype(bf16), _resident((SSM_WIDTH, D_MODEL))),
        (conv_w[0], _resident((CONV_K, CONV_WIDTH))),
        (w_conv_out[0].astype(bf16), _resident((CONV_WIDTH, D_MODEL))),
        (w_o[0].astype(bf16), _resident((D_MODEL, D_MODEL))),
        (post_norm, _resident((1, D_MODEL))),
        (w_ple[0].astype(bf16), _resident((PLE_DIM, D_MODEL))),
        (w_ple_gate[0].astype(bf16), _resident((D_MODEL, D_MODEL))),
        (ple_norm, _resident((1, D_MODEL))),
    ]
    out = pl.pallas_call(
        _block_kernel,
        grid=(nb + 1,),
        in_specs=[spec for _, spec in operands],
        out_specs=pl.BlockSpec((tb, D_MODEL), prev),
        out_shape=jax.ShapeDtypeStruct((seqlen, D_MODEL), jnp.float32),
        scratch_shapes=[
            pltpu.VMEM((N_TILES, 2 * LANES, TILE_STATE), jnp.bfloat16),
            pltpu.VMEM((N_TILES, TILE_STATE + 2 * LANES, 2 * LANES), jnp.bfloat16),
            pltpu.VMEM((2, SUBLANES, N_STATE), jnp.float32),
            pltpu.VMEM((8, SUBLANES, N_STATE), jnp.float32),
            pltpu.VMEM((2, SEG_PAIRS, SUBLANES, N_STATE), jnp.float32),
            pltpu.VMEM((2, SUBLANES, N_STATE), jnp.float32),
            pltpu.VMEM((tb, D_MODEL), jnp.bfloat16),
            pltpu.VMEM((N_TILES, tb, LANES), jnp.float32),
            pltpu.VMEM((2, N_TILES, tb // 2, LANES), jnp.float32),
            pltpu.VMEM((tb + SUBLANES, CONV_WIDTH), jnp.float32),
            pltpu.VMEM((N_TILES, tb // 2, TILE_STATE), jnp.float32),
            pltpu.VMEM((N_TILES, tb // 2, TILE_STATE), jnp.bfloat16),
            pltpu.VMEM((tb, CONV_WIDTH), jnp.bfloat16),
            pltpu.VMEM((tb, SSM_WIDTH), jnp.float32),
            pltpu.VMEM((tb, D_MODEL), jnp.float32),
            pltpu.VMEM((2, N_TILES, tb // 2, LANES), jnp.float32),
            pltpu.VMEM((N_TILES, tb, LANES), jnp.float32),
            pltpu.VMEM((tb, SSM_WIDTH), jnp.bfloat16),
            pltpu.VMEM((tb, D_MODEL), jnp.float32),
            pltpu.VMEM((tb, D_MODEL), jnp.float32),
            pltpu.VMEM((tb, D_MODEL), jnp.bfloat16),
            pltpu.VMEM((tb, D_MODEL), jnp.float32),
            pltpu.VMEM((tb, D_MODEL), jnp.bfloat16),
            pltpu.VMEM((tb, D_MODEL), jnp.float32),
        ],
        compiler_params=pltpu.CompilerParams(
            dimension_semantics=("arbitrary",),
            vmem_limit_bytes=VMEM_LIMIT_BYTES),
        name="hybrid_s5_shortconv_block",
    )(*[a for a, _ in operands])
    return out[None]
```

```python
import math

import jax
import jax.numpy as jnp
from jax import lax
from jax.experimental import pallas as pl
from jax.experimental.pallas import tpu as pltpu

D_MODEL = 1024
PLE_DIM = 256
SSM_WIDTH = 512
SSM_GROUP = 16
SSM_GROUPS = 32
SSM_STATE = 64
CONV_WIDTH = 1024
CONV_K = 3
RMS_EPS = 1e-6
IN_WIDTH = 7168

LANES = 128
MXU_N = 256
SUBLANES = 8
GROUPS_PER_TILE = LANES // SSM_GROUP
N_TILES = SSM_WIDTH // LANES
HALF = GROUPS_PER_TILE * SSM_STATE
TILE_STATE = 2 * HALF
N_COLS = HALF // LANES
N_STATE = SSM_GROUPS * SSM_STATE
ROW_BLOCK = 256
SEG_PAIRS = ROW_BLOCK // SUBLANES // 2
STAGE_ROWS = 64
VMEM_LIMIT_BYTES = 56 * 1024 * 1024

O_U, O_GS, O_XC, O_BC, O_CC, O_GC, O_RS, O_RC = 0, 512, 1024, 2048, 3072, 4096, 5120, 6144


def _rms(v, gain):
    return v * lax.rsqrt(jnp.mean(v * v, axis=-1, keepdims=True) + RMS_EPS) * gain


def _sigmoid(v):
    return 1.0 / (1.0 + jnp.exp(-v))


def _gelu_tanh(v):
    c = math.sqrt(2.0 / math.pi)
    return v * (0.5 * (1.0 + jnp.tanh(c * (v + 0.044715 * (v * v * v)))))


def _cmul(ar, ai, xr, xi):
    return ar * xr - ai * xi, ar * xi + ai * xr


def _expand_tables(xb_ref, yc_ref, sk_ref, a_ref, wb_ref, wc_ref, arow_ref, seg_ref, pw_ref):
    f32 = jnp.float32
    bf16 = jnp.bfloat16
    gmask = GROUPS_PER_TILE - 1

    def iota(shape, dim):
        return lax.broadcasted_iota(jnp.int32, shape, dim)

    def onehot(cond):
        return jnp.where(cond, 1.0, 0.0).astype(bf16)

    r = iota((LANES, TILE_STATE), 0)
    c = iota((LANES, TILE_STATE), 1)
    rep_b = onehot(((c & (SSM_STATE - 1)) == (r & (SSM_STATE - 1))) & (((c >> 7) & 1) == (r >> 6)))
    r = iota((2 * LANES, TILE_STATE), 0)
    c = iota((2 * LANES, TILE_STATE), 1)
    own_b = ((r >> 4) & gmask) == (((c >> 8) << 1) | ((c >> 6) & 1))
    r = iota((2 * SSM_GROUP, 2 * LANES), 0)
    c = iota((2 * SSM_GROUP, 2 * LANES), 1)
    rep_c = onehot(((c >> 7) == (r >> 4)) & ((c & (SSM_GROUP - 1)) == (r & (SSM_GROUP - 1))))
    r = iota((TILE_STATE, 2 * LANES), 0)
    c = iota((TILE_STATE, 2 * LANES), 1)
    own_c = (((r >> 8) << 1) | ((r >> 6) & 1)) == ((c >> 4) & gmask)
    r = iota((2 * LANES, 2 * LANES), 0)
    c = iota((2 * LANES, 2 * LANES), 1)
    own_s = ((r >> 4) & gmask) == ((c >> 4) & gmask)
    for t in range(N_TILES):
        wide = jnp.dot(xb_ref[t].astype(bf16), rep_b, preferred_element_type=f32)
        wb_ref[t] = jnp.where(own_b, wide, 0.0).astype(bf16)
        wide = jnp.dot(yc_ref[t].astype(bf16), rep_c, preferred_element_type=f32)
        wc_ref[t, 0:TILE_STATE, :] = jnp.where(own_c, wide, 0.0).astype(bf16)
        wide = jnp.dot(sk_ref[t].astype(bf16), rep_c, preferred_element_type=f32)
        wc_ref[t, TILE_STATE:TILE_STATE + 2 * LANES, :] = jnp.where(own_s, wide, 0.0).astype(bf16)

    npair = pw_ref.shape[1]
    nbits = npair.bit_length() - 1
    sq = [_cmul(a_ref[0:1, :], a_ref[1:2, :], a_ref[0:1, :], a_ref[1:2, :])]
    while len(sq) < nbits + 3:
        sq.append(_cmul(*sq[-1], *sq[-1]))
    shape = (SUBLANES, N_STATE)
    arow_ref[0] = jnp.broadcast_to(sq[0][0], shape)
    arow_ref[1] = jnp.broadcast_to(sq[0][1], shape)
    pw_ref[0, 0] = jnp.ones(shape, f32)
    pw_ref[1, 0] = jnp.zeros(shape, f32)
    for m in range(1, npair):
        acc = None
        for k in range(nbits):
            if (m >> k) & 1:
                acc = sq[k] if acc is None else _cmul(*acc, *sq[k])
        pw_ref[0, m] = jnp.broadcast_to(acc[0], shape)
        pw_ref[1, m] = jnp.broadcast_to(acc[1], shape)
    row = iota(shape, 0)
    for k, shift in enumerate((1, 2, 4)):
        seg_ref[2 * k] = jnp.where(row >= shift, jnp.broadcast_to(sq[nbits + k][0], shape), 0.0)
        seg_ref[2 * k + 1] = jnp.where(row >= shift, jnp.broadcast_to(sq[nbits + k][1], shape), 0.0)
    seg_ref[6] = jnp.broadcast_to(sq[nbits][0], shape)
    seg_ref[7] = jnp.broadcast_to(sq[nbits][1], shape)


def _load_weights(pairs, stage_ref, sem_ref):
    chunk_rows = stage_ref.shape[1]
    jobs = []
    for src_ref, dst_ref in pairs:
        rows, width = src_ref.shape
        assert rows % chunk_rows == 0 and width <= stage_ref.shape[2]
        for r0 in range(0, rows, chunk_rows):
            jobs.append((src_ref, dst_ref, r0, width))

    def copy(k):
        src_ref, _, r0, width = jobs[k]
        slot = k % 2
        return pltpu.make_async_copy(src_ref.at[pl.ds(r0, chunk_rows), :],
                                     stage_ref.at[slot, :, pl.ds(0, width)],
                                     sem_ref.at[slot])

    copy(0).start()
    for k, (_, dst_ref, r0, width) in enumerate(jobs):
        if k + 1 < len(jobs):
            copy(k + 1).start()
        copy(k).wait()
        dst_ref[r0:r0 + chunk_rows, :] = stage_ref[k % 2, :, 0:width].astype(dst_ref.dtype)


def _block_kernel(x_ref, p_ref, pre_ref, win_hbm, xb_ref, yc_ref, sk_ref, a_ref,
                  wglu_hbm, bglu_ref, wso_hbm, convw_ref, wco_hbm, wo_hbm, post_ref,
                  wple_hbm, wpg_hbm, plen_ref, out_ref,
                  win_ref, wglu_ref, wso_ref, wco_ref, wo_ref, wple_ref, wpg_ref, stage_ref, sem_ref,
                  wb_ref, wc_ref, arow_ref, seg_ref, pw_ref, state_ref, h_ref, un_ref, up_ref, v_ref,
                  sloc_ref, s_ref, z_ref, gs_ref, mg_ref,
                  yp_ref, y_ref, y2_ref, o_ref, x1_ref, x1b_ref, e_ref, mprev_ref, xprev_ref):
    tb = x_ref.shape[0]
    ns = tb // SUBLANES
    npair = ns // 2
    f32 = jnp.float32
    bf16 = jnp.bfloat16

    @pl.when(pl.program_id(0) == 0)
    def _():
        _load_weights([(wo_hbm, wo_ref), (win_hbm, win_ref), (wpg_hbm, wpg_ref),
                       (wple_hbm, wple_ref), (wco_hbm, wco_ref), (wglu_hbm, wglu_ref),
                       (wso_hbm, wso_ref)], stage_ref, sem_ref)
        _expand_tables(xb_ref, yc_ref, sk_ref, a_ref, wb_ref, wc_ref, arow_ref, seg_ref, pw_ref)
        state_ref[...] = jnp.zeros_like(state_ref)
        v_ref[tb:tb + SUBLANES, :] = jnp.zeros((SUBLANES, CONV_WIDTH), f32)
        mprev_ref[...] = jnp.zeros_like(mprev_ref)
        xprev_ref[...] = jnp.zeros_like(xprev_ref)

    def cols(n):
        return slice(n * MXU_N, (n + 1) * MXU_N)

    def tile_dot(lhs_ref, w_ref, off, n):
        return jnp.dot(lhs_ref[...], w_ref[:, off + n * MXU_N:off + (n + 1) * MXU_N],
                       preferred_element_type=f32)


    for n in range(D_MODEL // MXU_N):
        o_ref[:, cols(n)] = tile_dot(mprev_ref, wo_ref, 0, n)

    h_ref[...] = _rms(x_ref[...], pre_ref[...]).astype(bf16)

    for n in range(SSM_WIDTH // MXU_N):
        u = tile_dot(h_ref, win_ref, O_U, n)
        for c in range(MXU_N // LANES):
            un_ref[n * (MXU_N // LANES) + c] = u[:, c * LANES:(c + 1) * LANES]
    for t in range(N_TILES):
        for par in range(2):
            for j in range(SUBLANES):
                for mh in range(npair // SUBLANES):
                    up_ref[par, t, pl.ds(SUBLANES * SUBLANES * mh + j, SUBLANES,
                                         stride=SUBLANES), :] = (
                        un_ref[t, pl.ds(j * ns + 2 * SUBLANES * mh + par, SUBLANES, stride=2), :])

    v_ref[0:SUBLANES, :] = v_ref[tb:tb + SUBLANES, :]
    for n in range(CONV_WIDTH // MXU_N):
        v_ref[SUBLANES:SUBLANES + tb, cols(n)] = (
            tile_dot(h_ref, win_ref, O_CC, n) * tile_dot(h_ref, win_ref, O_XC, n))

    x1 = xprev_ref[...] + _rms(o_ref[...], post_ref[...])
    x1_ref[...] = x1
    x1b_ref[...] = x1.astype(bf16)

    row0 = lax.broadcasted_iota(jnp.int32, (SUBLANES, LANES), 0) == 0

    def scan_tile(t):
        ub = jnp.concatenate([up_ref[0, t], up_ref[1, t]], axis=1).astype(bf16)
        for q in range(N_COLS):
            bu = jnp.dot(ub, wb_ref[t, :, cols(q)],
                         preferred_element_type=f32)
            re = slice(q * 2 * LANES, q * 2 * LANES + LANES)
            im = slice(q * 2 * LANES + LANES, (q + 1) * 2 * LANES)
            sl = slice(t * HALF + q * LANES, t * HALF + (q + 1) * LANES)
            ar = arow_ref[0, :, sl]
            ai = arow_ref[1, :, sl]
            sr = bu[0:SUBLANES, 0:LANES]
            si = bu[0:SUBLANES, LANES:2 * LANES]
            for m in range(1, npair):
                rows = slice(m * SUBLANES, (m + 1) * SUBLANES)
                sloc_ref[t, rows, re] = sr
                sloc_ref[t, rows, im] = si
                mr, mi = _cmul(ar, ai, sr, si)
                sr = mr + bu[rows, 0:LANES]
                si = mi + bu[rows, LANES:2 * LANES]
            gr = jnp.where(row0, state_ref[0, :, sl], pltpu.roll(sr, 1, 0))
            gi = jnp.where(row0, state_ref[1, :, sl], pltpu.roll(si, 1, 0))
            for k, shift in enumerate((1, 2, 4)):
                fr, fi = _cmul(seg_ref[2 * k, :, sl], seg_ref[2 * k + 1, :, sl],
                               pltpu.roll(gr, shift, 0), pltpu.roll(gi, shift, 0))
                gr, gi = gr + fr, gi + fi
            fr, fi = _cmul(seg_ref[6, :, sl], seg_ref[7, :, sl], gr, gi)
            nr, ni = fr + sr, fi + si
            state_ref[0, :, sl] = jnp.broadcast_to(nr[SUBLANES - 1:SUBLANES, :], nr.shape)
            state_ref[1, :, sl] = jnp.broadcast_to(ni[SUBLANES - 1:SUBLANES, :], ni.shape)
            for m2 in range(npair // 2):
                parts_r, parts_i = [], []
                for m in (2 * m2, 2 * m2 + 1):
                    if m == 0:
                        parts_r.append(gr)
                        parts_i.append(gi)
                        continue
                    rows = slice(m * SUBLANES, (m + 1) * SUBLANES)
                    fr, fi = _cmul(pw_ref[0, m, :, sl], pw_ref[1, m, :, sl], gr, gi)
                    parts_r.append(sloc_ref[t, rows, re] + fr)
                    parts_i.append(sloc_ref[t, rows, im] + fi)
                rows2 = slice(2 * m2 * SUBLANES, (2 * m2 + 2) * SUBLANES)
                s_ref[t, rows2, re] = jnp.concatenate(parts_r, axis=0).astype(bf16)
                s_ref[t, rows2, im] = jnp.concatenate(parts_i, axis=0).astype(bf16)

    scan_tile(0)

    for n in range(CONV_WIDTH // MXU_N):
        bc = tile_dot(h_ref, win_ref, O_BC, n)
        gc = tile_dot(h_ref, win_ref, O_GC, n)
        conv = (convw_ref[0:1, cols(n)] * v_ref[SUBLANES - 2:SUBLANES - 2 + tb, cols(n)]
                + convw_ref[1:2, cols(n)] * v_ref[SUBLANES - 1:SUBLANES - 1 + tb, cols(n)]
                + convw_ref[2:3, cols(n)] * v_ref[SUBLANES:SUBLANES + tb, cols(n)])
        z_ref[:, cols(n)] = (bc * conv * (gc * _sigmoid(gc))).astype(bf16)
        if n == 1:
            scan_tile(1)

    pb = p_ref[...].astype(bf16)
    for n in range(D_MODEL // MXU_N):
        eg = tile_dot(x1b_ref, wpg_ref, 0, n)
        ei = jnp.dot(pb, wple_ref[:, cols(n)], preferred_element_type=f32)
        e_ref[:, cols(n)] = ei * _sigmoid(eg)
        if n == 1:
            scan_tile(2)

    for n in range(SSM_WIDTH // MXU_N):
        gs = tile_dot(h_ref, win_ref, O_GS, n)
        gs_ref[:, cols(n)] = gs * _sigmoid(gs)
    scan_tile(3)

    out_ref[...] = x1_ref[...] + _rms(e_ref[...], plen_ref[...])

    def gated_b(n):
        rc = tile_dot(h_ref, win_ref, O_RC, n)
        yb = tile_dot(z_ref, wco_ref, 0, n)
        mg_ref[:, cols(n)] = _sigmoid(rc) * yb

    gated_b(0)
    gated_b(1)

    for t in range(N_TILES):
        lhs = jnp.concatenate(
            [s_ref[t], up_ref[0, t].astype(bf16), up_ref[1, t].astype(bf16)], axis=1)
        yy = jnp.dot(lhs, wc_ref[t], preferred_element_type=f32)
        yp_ref[0, t] = yy[:, 0:LANES]
        yp_ref[1, t] = yy[:, LANES:2 * LANES]
    gated_b(2)
    gated_b(3)
    for t in range(N_TILES):
        for par in range(2):
            for j in range(SUBLANES):
                for mh in range(npair // SUBLANES):
                    y_ref[t, pl.ds(j * ns + 2 * SUBLANES * mh + par, SUBLANES, stride=2), :] = (
                        _gelu_tanh(yp_ref[par, t, pl.ds(SUBLANES * SUBLANES * mh + j, SUBLANES,
                                                        stride=SUBLANES), :]))
    y = jnp.concatenate([y_ref[t] for t in range(N_TILES)], axis=1)
    yb16 = y.astype(bf16)
    for n in range(SSM_WIDTH // MXU_N):
        glu = jnp.dot(yb16, wglu_ref[:, cols(n)], preferred_element_type=f32)
        y2_ref[:, cols(n)] = (y[:, cols(n)] * _sigmoid(glu + bglu_ref[:, cols(n)])
                              * gs_ref[:, cols(n)]).astype(bf16)

    for n in range(D_MODEL // MXU_N):
        rs = tile_dot(h_ref, win_ref, O_RS, n)
        ya = tile_dot(y2_ref, wso_ref, 0, n)
        mprev_ref[:, cols(n)] = (_sigmoid(rs) * ya + mg_ref[:, cols(n)]).astype(bf16)
    xprev_ref[...] = x_ref[...]


def _ssm_params(a_re, a_im, log_step, b_re, b_im, c_re, c_im, d_skip):
    f32 = jnp.float32
    G, P, H, T, GT = SSM_GROUPS, SSM_STATE, SSM_GROUP, N_TILES, GROUPS_PER_TILE
    dt = jnp.exp(log_step.astype(f32))[:, None]
    mag = jnp.exp(a_re * dt)
    abar_r = mag * jnp.cos(a_im * dt)
    abar_i = mag * jnp.sin(a_im * dt)
    den = a_re * a_re + a_im * a_im
    nr = abar_r - 1.0
    fr = (nr * a_re + abar_i * a_im) / den
    fi = (abar_i * a_re - nr * a_im) / den
    bbar_r = fr[..., None] * b_re - fi[..., None] * b_im
    bbar_i = fr[..., None] * b_im + fi[..., None] * b_re
    ab_r, ab_i = _cmul(abar_r[..., None], abar_i[..., None], bbar_r, bbar_i)
    ca_r, ca_i = _cmul(abar_r[:, None, :], abar_i[:, None, :], c_re, c_im)
    caa_r, caa_i = _cmul(abar_r[:, None, :], abar_i[:, None, :], ca_r, ca_i)

    def re_prod(xr, xi, yr, yi):
        yr_t = jnp.swapaxes(yr, 1, 2)[:, None]
        yi_t = jnp.swapaxes(yi, 1, 2)[:, None]
        return jnp.sum(xr[:, :, None, :] * yr_t - xi[:, :, None, :] * yi_t, axis=-1)

    cb_d = (re_prod(c_re, c_im, bbar_r, bbar_i)
            + jnp.eye(H, dtype=f32) * d_skip.reshape(G, H, 1))
    cab = re_prod(ca_r, ca_i, bbar_r, bbar_i)

    xb = jnp.stack([jnp.stack([ab_r, ab_i]), jnp.stack([bbar_r, bbar_i])])
    xb = xb.reshape(2, 2, T, GT, P, H).transpose(2, 0, 3, 5, 1, 4).reshape(T, 2 * LANES, 2 * P)
    yc = jnp.stack([jnp.stack([ca_r, -ca_i]), jnp.stack([caa_r, -caa_i])])
    yc = yc.reshape(2, 2, T, N_COLS, 2, H, P).transpose(2, 3, 1, 4, 6, 0, 5)
    yc = yc.reshape(T, TILE_STATE, 2 * H)
    zero = jnp.zeros_like(cab)
    sk = jnp.stack([jnp.stack([cb_d, cab]), jnp.stack([zero, cb_d])])
    sk = sk.reshape(2, 2, T, GT, H, H).transpose(2, 0, 3, 5, 1, 4).reshape(T, 2 * LANES, 2 * H)
    a = jnp.stack([abar_r.reshape(N_STATE), abar_i.reshape(N_STATE)])
    return xb, yc, sk, a


def _resident(shape):
    return pl.BlockSpec(shape, lambda i: (0,) * len(shape), pipeline_mode=pl.Buffered(1))


@jax.jit
def kernel(x, p, pre_norm, w_in, a_re, a_im, log_step, b_re, b_im, c_re, c_im, d_skip,
           w_glu, b_glu, w_ssm_out, conv_w, w_conv_out, w_o, post_norm,
           w_ple, w_ple_gate, ple_norm):
    bsz, seqlen, _ = x.shape
    depth = p.shape[0]
    assert bsz == 1 and depth == 1 and seqlen % ROW_BLOCK == 0
    bf16 = jnp.bfloat16
    tb = ROW_BLOCK

    xb, yc, sk, a_bar = _ssm_params(
        a_re[0], a_im[0], log_step[0], b_re[0], b_im[0], c_re[0], c_im[0], d_skip[0])
    nb = seqlen // tb
    cur = lambda i: (jnp.minimum(i, nb - 1), 0)
    prev = lambda i: (jnp.maximum(i - 1, 0), 0)
    hbm = pl.BlockSpec(memory_space=pl.ANY)
    operands = [
        (x[0], pl.BlockSpec((tb, D_MODEL), cur)),
        (p[0, 0], pl.BlockSpec((tb, PLE_DIM), prev)),
        (pre_norm, _resident((1, D_MODEL))),
        (w_in.reshape(D_MODEL, IN_WIDTH), hbm),
        (xb, _resident(xb.shape)),
        (yc, _resident(yc.shape)),
        (sk, _resident(sk.shape)),
        (a_bar, _resident(a_bar.shape)),
        (w_glu.reshape(SSM_WIDTH, SSM_WIDTH), hbm),
        (b_glu, _resident((1, SSM_WIDTH))),
        (w_ssm_out.reshape(SSM_WIDTH, D_MODEL), hbm),
        (conv_w[0], _resident((CONV_K, CONV_WIDTH))),
        (w_conv_out.reshape(CONV_WIDTH, D_MODEL), hbm),
        (w_o.reshape(D_MODEL, D_MODEL), hbm),
        (post_norm, _resident((1, D_MODEL))),
        (w_ple.reshape(PLE_DIM, D_MODEL), hbm),
        (w_ple_gate.reshape(D_MODEL, D_MODEL), hbm),
        (ple_norm, _resident((1, D_MODEL))),
    ]
    out = pl.pallas_call(
        _block_kernel,
        grid=(nb + 1,),
        in_specs=[spec for _, spec in operands],
        out_specs=pl.BlockSpec((tb, D_MODEL), prev),
        out_shape=jax.ShapeDtypeStruct((seqlen, D_MODEL), jnp.float32),
        scratch_shapes=[
            pltpu.VMEM((D_MODEL, IN_WIDTH), bf16),
            pltpu.VMEM((SSM_WIDTH, SSM_WIDTH), bf16),
            pltpu.VMEM((SSM_WIDTH, D_MODEL), bf16),
            pltpu.VMEM((CONV_WIDTH, D_MODEL), bf16),
            pltpu.VMEM((D_MODEL, D_MODEL), bf16),
            pltpu.VMEM((PLE_DIM, D_MODEL), bf16),
            pltpu.VMEM((D_MODEL, D_MODEL), bf16),
            pltpu.VMEM((2, STAGE_ROWS, IN_WIDTH), jnp.float32),
            pltpu.SemaphoreType.DMA((2,)),
            pltpu.VMEM((N_TILES, 2 * LANES, TILE_STATE), jnp.bfloat16),
            pltpu.VMEM((N_TILES, TILE_STATE + 2 * LANES, 2 * LANES), jnp.bfloat16),
            pltpu.VMEM((2, SUBLANES, N_STATE), jnp.float32),
            pltpu.VMEM((8, SUBLANES, N_STATE), jnp.float32),
            pltpu.VMEM((2, SEG_PAIRS, SUBLANES, N_STATE), jnp.float32),
            pltpu.VMEM((2, SUBLANES, N_STATE), jnp.float32),
            pltpu.VMEM((tb, D_MODEL), jnp.bfloat16),
            pltpu.VMEM((N_TILES, tb, LANES), jnp.float32),
            pltpu.VMEM((2, N_TILES, tb // 2, LANES), jnp.float32),
            pltpu.VMEM((tb + SUBLANES, CONV_WIDTH), jnp.float32),
            pltpu.VMEM((N_TILES, tb // 2, TILE_STATE), jnp.float32),
            pltpu.VMEM((N_TILES, tb // 2, TILE_STATE), jnp.bfloat16),
            pltpu.VMEM((tb, CONV_WIDTH), jnp.bfloat16),
            pltpu.VMEM((tb, SSM_WIDTH), jnp.float32),
            pltpu.VMEM((tb, D_MODEL), jnp.float32),
            pltpu.VMEM((2, N_TILES, tb // 2, LANES), jnp.float32),
            pltpu.VMEM((N_TILES, tb, LANES), jnp.float32),
            pltpu.VMEM((tb, SSM_WIDTH), jnp.bfloat16),
            pltpu.VMEM((tb, D_MODEL), jnp.float32),
            pltpu.VMEM((tb, D_MODEL), jnp.float32),
            pltpu.VMEM((tb, D_MODEL), jnp.bfloat16),
            pltpu.VMEM((tb, D_MODEL), jnp.float32),
            pltpu.VMEM((tb, D_MODEL), jnp.bfloat16),
            pltpu.VMEM((tb, D_MODEL), jnp.float32),
        ],
        compiler_params=pltpu.CompilerParams(
            dimension_semantics=("arbitrary",),
            vmem_limit_bytes=VMEM_LIMIT_BYTES),
        name="hybrid_s5_shortconv_block",
    )(*[a for a, _ in operands])
    return out[None]
```

```python
import math

import jax
import jax.numpy as jnp
from jax import lax
from jax.experimental import pallas as pl
from jax.experimental.pallas import tpu as pltpu

D_MODEL = 1024
PLE_DIM = 256
SSM_WIDTH = 512
SSM_GROUP = 16
SSM_GROUPS = 32
SSM_STATE = 64
CONV_WIDTH = 1024
CONV_K = 3
RMS_EPS = 1e-6
IN_WIDTH = 7168

LANES = 128
MXU_N = 256
SUBLANES = 8
GROUPS_PER_TILE = LANES // SSM_GROUP
N_TILES = SSM_WIDTH // LANES
HALF = GROUPS_PER_TILE * SSM_STATE
TILE_STATE = 2 * HALF
N_COLS = HALF // LANES
N_STATE = SSM_GROUPS * SSM_STATE
ROW_BLOCK = 256
SEG_PAIRS = ROW_BLOCK // SUBLANES // 2
STAGE_ROWS = 64
VMEM_LIMIT_BYTES = 56 * 1024 * 1024

O_U, O_GS, O_XC, O_BC, O_CC, O_GC, O_RS, O_RC = 0, 512, 1024, 2048, 3072, 4096, 5120, 6144


def _rms(v, gain):
    return v * lax.rsqrt(jnp.mean(v * v, axis=-1, keepdims=True) + RMS_EPS) * gain


def _sigmoid(v):
    return 1.0 / (1.0 + jnp.exp(-v))


def _gelu_tanh(v):
    c = math.sqrt(2.0 / math.pi)
    return v * (0.5 * (1.0 + jnp.tanh(c * (v + 0.044715 * (v * v * v)))))


def _cmul(ar, ai, xr, xi):
    return ar * xr - ai * xi, ar * xi + ai * xr


def _expand_tables(xb_ref, yc_ref, sk_ref, a_ref, wb_ref, wc_ref, arow_ref, seg_ref, pw_ref):
    f32 = jnp.float32
    bf16 = jnp.bfloat16
    gmask = GROUPS_PER_TILE - 1

    def iota(shape, dim):
        return lax.broadcasted_iota(jnp.int32, shape, dim)

    def onehot(cond):
        return jnp.where(cond, 1.0, 0.0).astype(bf16)

    r = iota((LANES, TILE_STATE), 0)
    c = iota((LANES, TILE_STATE), 1)
    rep_b = onehot(((c & (SSM_STATE - 1)) == (r & (SSM_STATE - 1))) & (((c >> 7) & 1) == (r >> 6)))
    r = iota((2 * LANES, TILE_STATE), 0)
    c = iota((2 * LANES, TILE_STATE), 1)
    own_b = ((r >> 4) & gmask) == (((c >> 8) << 1) | ((c >> 6) & 1))
    r = iota((2 * SSM_GROUP, 2 * LANES), 0)
    c = iota((2 * SSM_GROUP, 2 * LANES), 1)
    rep_c = onehot(((c >> 7) == (r >> 4)) & ((c & (SSM_GROUP - 1)) == (r & (SSM_GROUP - 1))))
    r = iota((TILE_STATE, 2 * LANES), 0)
    c = iota((TILE_STATE, 2 * LANES), 1)
    own_c = (((r >> 8) << 1) | ((r >> 6) & 1)) == ((c >> 4) & gmask)
    r = iota((2 * LANES, 2 * LANES), 0)
    c = iota((2 * LANES, 2 * LANES), 1)
    own_s = ((r >> 4) & gmask) == ((c >> 4) & gmask)
    for t in range(N_TILES):
        wide = jnp.dot(xb_ref[t].astype(bf16), rep_b, preferred_element_type=f32)
        wb_ref[t] = jnp.where(own_b, wide, 0.0).astype(bf16)
        wide = jnp.dot(yc_ref[t].astype(bf16), rep_c, preferred_element_type=f32)
        wc_ref[t, 0:TILE_STATE, :] = jnp.where(own_c, wide, 0.0).astype(bf16)
        wide = jnp.dot(sk_ref[t].astype(bf16), rep_c, preferred_element_type=f32)
        wc_ref[t, TILE_STATE:TILE_STATE + 2 * LANES, :] = jnp.where(own_s, wide, 0.0).astype(bf16)

    npair = pw_ref.shape[1]
    nbits = npair.bit_length() - 1
    sq = [_cmul(a_ref[0:1, :], a_ref[1:2, :], a_ref[0:1, :], a_ref[1:2, :])]
    while len(sq) < nbits + 3:
        sq.append(_cmul(*sq[-1], *sq[-1]))
    shape = (SUBLANES, N_STATE)
    arow_ref[0] = jnp.broadcast_to(sq[0][0], shape)
    arow_ref[1] = jnp.broadcast_to(sq[0][1], shape)
    pw_ref[0, 0] = jnp.ones(shape, f32)
    pw_ref[1, 0] = jnp.zeros(shape, f32)
    for m in range(1, npair):
        acc = None
        for k in range(nbits):
            if (m >> k) & 1:
                acc = sq[k] if acc is None else _cmul(*acc, *sq[k])
        pw_ref[0, m] = jnp.broadcast_to(acc[0], shape)
        pw_ref[1, m] = jnp.broadcast_to(acc[1], shape)
    row = iota(shape, 0)
    for k, shift in enumerate((1, 2, 4)):
        seg_ref[2 * k] = jnp.where(row >= shift, jnp.broadcast_to(sq[nbits + k][0], shape), 0.0)
        seg_ref[2 * k + 1] = jnp.where(row >= shift, jnp.broadcast_to(sq[nbits + k][1], shape), 0.0)
    seg_ref[6] = jnp.broadcast_to(sq[nbits][0], shape)
    seg_ref[7] = jnp.broadcast_to(sq[nbits][1], shape)


def _load_weights(pairs, stage_ref, sem_ref):
    chunk_rows, stage_cols = stage_ref.shape[1:]
    rounds = []
    for src_ref, dst_ref in pairs:
        rows, width = src_ref.shape
        assert rows % chunk_rows == 0 and width <= stage_cols
        per_round = stage_cols // width
        starts = list(range(0, rows, chunk_rows))
        for i in range(0, len(starts), per_round):
            rounds.append([(src_ref, dst_ref, r0, width, j * width)
                           for j, r0 in enumerate(starts[i:i + per_round])])

    def copies(k):
        return [pltpu.make_async_copy(src_ref.at[pl.ds(r0, chunk_rows), :],
                                      stage_ref.at[k % 2, :, pl.ds(c0, width)],
                                      sem_ref.at[k % 2])
                for src_ref, _, r0, width, c0 in rounds[k]]

    for cp in copies(0):
        cp.start()
    for k, chunks in enumerate(rounds):
        if k + 1 < len(rounds):
            for cp in copies(k + 1):
                cp.start()
        for cp in copies(k):
            cp.wait()
        for _, dst_ref, r0, width, c0 in chunks:
            dst_ref[r0:r0 + chunk_rows, :] = stage_ref[k % 2, :, c0:c0 + width].astype(dst_ref.dtype)


def _block_kernel(x_ref, p_ref, pre_ref, win_hbm, xb_ref, yc_ref, sk_ref, a_ref,
                  wglu_hbm, bglu_ref, wso_hbm, convw_ref, wco_hbm, wo_hbm, post_ref,
                  wple_hbm, wpg_hbm, plen_ref, out_ref,
                  win_ref, wglu_ref, wso_ref, wco_ref, wo_ref, wple_ref, wpg_ref, stage_ref, sem_ref,
                  wb_ref, wc_ref, arow_ref, seg_ref, pw_ref, state_ref, h_ref, un_ref, up_ref, v_ref,
                  sloc_ref, s_ref, z_ref, gs_ref, mg_ref,
                  yp_ref, y_ref, y2_ref, o_ref, x1_ref, x1b_ref, e_ref, mprev_ref, xprev_ref):
    tb = x_ref.shape[0]
    ns = tb // SUBLANES
    npair = ns // 2
    f32 = jnp.float32
    bf16 = jnp.bfloat16

    @pl.when(pl.program_id(0) == 0)
    def _():
        _load_weights([(wo_hbm, wo_ref), (win_hbm, win_ref), (wpg_hbm, wpg_ref),
                       (wple_hbm, wple_ref), (wco_hbm, wco_ref), (wglu_hbm, wglu_ref),
                       (wso_hbm, wso_ref)], stage_ref, sem_ref)
        _expand_tables(xb_ref, yc_ref, sk_ref, a_ref, wb_ref, wc_ref, arow_ref, seg_ref, pw_ref)
        state_ref[...] = jnp.zeros_like(state_ref)
        v_ref[tb:tb + SUBLANES, :] = jnp.zeros((SUBLANES, CONV_WIDTH), f32)
        mprev_ref[...] = jnp.zeros_like(mprev_ref)
        xprev_ref[...] = jnp.zeros_like(xprev_ref)

    def cols(n):
        return slice(n * MXU_N, (n + 1) * MXU_N)

    def tile_dot(lhs_ref, w_ref, off, n):
        return jnp.dot(lhs_ref[...], w_ref[:, off + n * MXU_N:off + (n + 1) * MXU_N],
                       preferred_element_type=f32)


    for n in range(D_MODEL // MXU_N):
        o_ref[:, cols(n)] = tile_dot(mprev_ref, wo_ref, 0, n)

    h_ref[...] = _rms(x_ref[...], pre_ref[...]).astype(bf16)

    for n in range(SSM_WIDTH // MXU_N):
        u = tile_dot(h_ref, win_ref, O_U, n)
        for c in range(MXU_N // LANES):
            un_ref[n * (MXU_N // LANES) + c] = u[:, c * LANES:(c + 1) * LANES]
    for t in range(N_TILES):
        for par in range(2):
            for j in range(SUBLANES):
                for mh in range(npair // SUBLANES):
                    up_ref[par, t, pl.ds(SUBLANES * SUBLANES * mh + j, SUBLANES,
                                         stride=SUBLANES), :] = (
                        un_ref[t, pl.ds(j * ns + 2 * SUBLANES * mh + par, SUBLANES, stride=2), :])

    v_ref[0:SUBLANES, :] = v_ref[tb:tb + SUBLANES, :]
    for n in range(CONV_WIDTH // MXU_N):
        v_ref[SUBLANES:SUBLANES + tb, cols(n)] = (
            tile_dot(h_ref, win_ref, O_CC, n) * tile_dot(h_ref, win_ref, O_XC, n))

    x1 = xprev_ref[...] + _rms(o_ref[...], post_ref[...])
    x1_ref[...] = x1
    x1b_ref[...] = x1.astype(bf16)

    row0 = lax.broadcasted_iota(jnp.int32, (SUBLANES, LANES), 0) == 0

    def scan_tile(t):
        ub = jnp.concatenate([up_ref[0, t], up_ref[1, t]], axis=1).astype(bf16)
        for q in range(N_COLS):
            bu = jnp.dot(ub, wb_ref[t, :, cols(q)],
                         preferred_element_type=f32)
            re = slice(q * 2 * LANES, q * 2 * LANES + LANES)
            im = slice(q * 2 * LANES + LANES, (q + 1) * 2 * LANES)
            sl = slice(t * HALF + q * LANES, t * HALF + (q + 1) * LANES)
            ar = arow_ref[0, :, sl]
            ai = arow_ref[1, :, sl]
            sr = bu[0:SUBLANES, 0:LANES]
            si = bu[0:SUBLANES, LANES:2 * LANES]
            for m in range(1, npair):
                rows = slice(m * SUBLANES, (m + 1) * SUBLANES)
                sloc_ref[t, rows, re] = sr
                sloc_ref[t, rows, im] = si
                mr, mi = _cmul(ar, ai, sr, si)
                sr = mr + bu[rows, 0:LANES]
                si = mi + bu[rows, LANES:2 * LANES]
            gr = jnp.where(row0, state_ref[0, :, sl], pltpu.roll(sr, 1, 0))
            gi = jnp.where(row0, state_ref[1, :, sl], pltpu.roll(si, 1, 0))
            for k, shift in enumerate((1, 2, 4)):
                fr, fi = _cmul(seg_ref[2 * k, :, sl], seg_ref[2 * k + 1, :, sl],
                               pltpu.roll(gr, shift, 0), pltpu.roll(gi, shift, 0))
                gr, gi = gr + fr, gi + fi
            fr, fi = _cmul(seg_ref[6, :, sl], seg_ref[7, :, sl], gr, gi)
            nr, ni = fr + sr, fi + si
            state_ref[0, :, sl] = jnp.broadcast_to(nr[SUBLANES - 1:SUBLANES, :], nr.shape)
            state_ref[1, :, sl] = jnp.broadcast_to(ni[SUBLANES - 1:SUBLANES, :], ni.shape)
            for m2 in range(npair // 2):
                parts_r, parts_i = [], []
                for m in (2 * m2, 2 * m2 + 1):
                    if m == 0:
                        parts_r.append(gr)
                        parts_i.append(gi)
                        continue
                    rows = slice(m * SUBLANES, (m + 1) * SUBLANES)
                    fr, fi = _cmul(pw_ref[0, m, :, sl], pw_ref[1, m, :, sl], gr, gi)
                    parts_r.append(sloc_ref[t, rows, re] + fr)
                    parts_i.append(sloc_ref[t, rows, im] + fi)
                rows2 = slice(2 * m2 * SUBLANES, (2 * m2 + 2) * SUBLANES)
                s_ref[t, rows2, re] = jnp.concatenate(parts_r, axis=0).astype(bf16)
                s_ref[t, rows2, im] = jnp.concatenate(parts_i, axis=0).astype(bf16)

    scan_tile(0)

    for n in range(CONV_WIDTH // MXU_N):
        bc = tile_dot(h_ref, win_ref, O_BC, n)
        gc = tile_dot(h_ref, win_ref, O_GC, n)
        conv = (convw_ref[0:1, cols(n)] * v_ref[SUBLANES - 2:SUBLANES - 2 + tb, cols(n)]
                + convw_ref[1:2, cols(n)] * v_ref[SUBLANES - 1:SUBLANES - 1 + tb, cols(n)]
                + convw_ref[2:3, cols(n)] * v_ref[SUBLANES:SUBLANES + tb, cols(n)])
        z_ref[:, cols(n)] = (bc * conv * (gc * _sigmoid(gc))).astype(bf16)
        if n == 1:
            scan_tile(1)

    pb = p_ref[...].astype(bf16)
    for n in range(D_MODEL // MXU_N):
        eg = tile_dot(x1b_ref, wpg_ref, 0, n)
        ei = jnp.dot(pb, wple_ref[:, cols(n)], preferred_element_type=f32)
        e_ref[:, cols(n)] = ei * _sigmoid(eg)
        if n == 1:
            scan_tile(2)

    for n in range(SSM_WIDTH // MXU_N):
        gs = tile_dot(h_ref, win_ref, O_GS, n)
        gs_ref[:, cols(n)] = gs * _sigmoid(gs)
    scan_tile(3)

    out_ref[...] = x1_ref[...] + _rms(e_ref[...], plen_ref[...])

    def gated_b(n):
        rc = tile_dot(h_ref, win_ref, O_RC, n)
        yb = tile_dot(z_ref, wco_ref, 0, n)
        mg_ref[:, cols(n)] = _sigmoid(rc) * yb

    gated_b(0)
    gated_b(1)

    for t in range(N_TILES):
        lhs = jnp.concatenate(
            [s_ref[t], up_ref[0, t].astype(bf16), up_ref[1, t].astype(bf16)], axis=1)
        yy = jnp.dot(lhs, wc_ref[t], preferred_element_type=f32)
        yp_ref[0, t] = yy[:, 0:LANES]
        yp_ref[1, t] = yy[:, LANES:2 * LANES]
    gated_b(2)
    gated_b(3)
    for t in range(N_TILES):
        for par in range(2):
            for j in range(SUBLANES):
                for mh in range(npair // SUBLANES):
                    y_ref[t, pl.ds(j * ns + 2 * SUBLANES * mh + par, SUBLANES, stride=2), :] = (
                        _gelu_tanh(yp_ref[par, t, pl.ds(SUBLANES * SUBLANES * mh + j, SUBLANES,
                                                        stride=SUBLANES), :]))
    y = jnp.concatenate([y_ref[t] for t in range(N_TILES)], axis=1)
    yb16 = y.astype(bf16)
    for n in range(SSM_WIDTH // MXU_N):
        glu = jnp.dot(yb16, wglu_ref[:, cols(n)], preferred_element_type=f32)
        y2_ref[:, cols(n)] = (y[:, cols(n)] * _sigmoid(glu + bglu_ref[:, cols(n)])
                              * gs_ref[:, cols(n)]).astype(bf16)

    for n in range(D_MODEL // MXU_N):
        rs = tile_dot(h_ref, win_ref, O_RS, n)
        ya = tile_dot(y2_ref, wso_ref, 0, n)
        mprev_ref[:, cols(n)] = (_sigmoid(rs) * ya + mg_ref[:, cols(n)]).astype(bf16)
    xprev_ref[...] = x_ref[...]


def _ssm_params(a_re, a_im, log_step, b_re, b_im, c_re, c_im, d_skip):
    f32 = jnp.float32
    G, P, H, T, GT = SSM_GROUPS, SSM_STATE, SSM_GROUP, N_TILES, GROUPS_PER_TILE
    dt = jnp.exp(log_step.astype(f32))[:, None]
    mag = jnp.exp(a_re * dt)
    abar_r = mag * jnp.cos(a_im * dt)
    abar_i = mag * jnp.sin(a_im * dt)
    den = a_re * a_re + a_im * a_im
    nr = abar_r - 1.0
    fr = (nr * a_re + abar_i * a_im) / den
    fi = (abar_i * a_re - nr * a_im) / den
    bbar_r = fr[..., None] * b_re - fi[..., None] * b_im
    bbar_i = fr[..., None] * b_im + fi[..., None] * b_re
    ab_r, ab_i = _cmul(abar_r[..., None], abar_i[..., None], bbar_r, bbar_i)
    ca_r, ca_i = _cmul(abar_r[:, None, :], abar_i[:, None, :], c_re, c_im)
    caa_r, caa_i = _cmul(abar_r[:, None, :], abar_i[:, None, :], ca_r, ca_i)

    def re_prod(xr, xi, yr, yi):
        yr_t = jnp.swapaxes(yr, 1, 2)[:, None]
        yi_t = jnp.swapaxes(yi, 1, 2)[:, None]
        return jnp.sum(xr[:, :, None, :] * yr_t - xi[:, :, None, :] * yi_t, axis=-1)

    cb_d = (re_prod(c_re, c_im, bbar_r, bbar_i)
            + jnp.eye(H, dtype=f32) * d_skip.reshape(G, H, 1))
    cab = re_prod(ca_r, ca_i, bbar_r, bbar_i)

    xb = jnp.stack([jnp.stack([ab_r, ab_i]), jnp.stack([bbar_r, bbar_i])])
    xb = xb.reshape(2, 2, T, GT, P, H).transpose(2, 0, 3, 5, 1, 4).reshape(T, 2 * LANES, 2 * P)
    yc = jnp.stack([jnp.stack([ca_r, -ca_i]), jnp.stack([caa_r, -caa_i])])
    yc = yc.reshape(2, 2, T, N_COLS, 2, H, P).transpose(2, 3, 1, 4, 6, 0, 5)
    yc = yc.reshape(T, TILE_STATE, 2 * H)
    zero = jnp.zeros_like(cab)
    sk = jnp.stack([jnp.stack([cb_d, cab]), jnp.stack([zero, cb_d])])
    sk = sk.reshape(2, 2, T, GT, H, H).transpose(2, 0, 3, 5, 1, 4).reshape(T, 2 * LANES, 2 * H)
    a = jnp.stack([abar_r.reshape(N_STATE), abar_i.reshape(N_STATE)])
    return xb, yc, sk, a


def _resident(shape):
    return pl.BlockSpec(shape, lambda i: (0,) * len(shape), pipeline_mode=pl.Buffered(1))


@jax.jit
def kernel(x, p, pre_norm, w_in, a_re, a_im, log_step, b_re, b_im, c_re, c_im, d_skip,
           w_glu, b_glu, w_ssm_out, conv_w, w_conv_out, w_o, post_norm,
           w_ple, w_ple_gate, ple_norm):
    bsz, seqlen, _ = x.shape
    depth = p.shape[0]
    assert bsz == 1 and depth == 1 and seqlen % ROW_BLOCK == 0
    bf16 = jnp.bfloat16
    tb = ROW_BLOCK

    xb, yc, sk, a_bar = _ssm_params(
        a_re[0], a_im[0], log_step[0], b_re[0], b_im[0], c_re[0], c_im[0], d_skip[0])
    nb = seqlen // tb
    cur = lambda i: (jnp.minimum(i, nb - 1), 0)
    prev = lambda i: (jnp.maximum(i - 1, 0), 0)
    hbm = pl.BlockSpec(memory_space=pl.ANY)
    operands = [
        (x[0], pl.BlockSpec((tb, D_MODEL), cur)),
        (p[0, 0], pl.BlockSpec((tb, PLE_DIM), prev)),
        (pre_norm, _resident((1, D_MODEL))),
        (w_in.reshape(D_MODEL, IN_WIDTH), hbm),
        (xb, _resident(xb.shape)),
        (yc, _resident(yc.shape)),
        (sk, _resident(sk.shape)),
        (a_bar, _resident(a_bar.shape)),
        (w_glu.reshape(SSM_WIDTH, SSM_WIDTH), hbm),
        (b_glu, _resident((1, SSM_WIDTH))),
        (w_ssm_out.reshape(SSM_WIDTH, D_MODEL), hbm),
        (conv_w[0], _resident((CONV_K, CONV_WIDTH))),
        (w_conv_out.reshape(CONV_WIDTH, D_MODEL), hbm),
        (w_o.reshape(D_MODEL, D_MODEL), hbm),
        (post_norm, _resident((1, D_MODEL))),
        (w_ple.reshape(PLE_DIM, D_MODEL), hbm),
        (w_ple_gate.reshape(D_MODEL, D_MODEL), hbm),
        (ple_norm, _resident((1, D_MODEL))),
    ]
    out = pl.pallas_call(
        _block_kernel,
        grid=(nb + 1,),
        in_specs=[spec for _, spec in operands],
        out_specs=pl.BlockSpec((tb, D_MODEL), prev),
        out_shape=jax.ShapeDtypeStruct((seqlen, D_MODEL), jnp.float32),
        scratch_shapes=[
            pltpu.VMEM((D_MODEL, IN_WIDTH), bf16),
            pltpu.VMEM((SSM_WIDTH, SSM_WIDTH), bf16),
            pltpu.VMEM((SSM_WIDTH, D_MODEL), bf16),
            pltpu.VMEM((CONV_WIDTH, D_MODEL), bf16),
            pltpu.VMEM((D_MODEL, D_MODEL), bf16),
            pltpu.VMEM((PLE_DIM, D_MODEL), bf16),
            pltpu.VMEM((D_MODEL, D_MODEL), bf16),
            pltpu.VMEM((2, STAGE_ROWS, IN_WIDTH), jnp.float32),
            pltpu.SemaphoreType.DMA((2,)),
            pltpu.VMEM((N_TILES, 2 * LANES, TILE_STATE), jnp.bfloat16),
            pltpu.VMEM((N_TILES, TILE_STATE + 2 * LANES, 2 * LANES), jnp.bfloat16),
            pltpu.VMEM((2, SUBLANES, N_STATE), jnp.float32),
            pltpu.VMEM((8, SUBLANES, N_STATE), jnp.float32),
            pltpu.VMEM((2, SEG_PAIRS, SUBLANES, N_STATE), jnp.float32),
            pltpu.VMEM((2, SUBLANES, N_STATE), jnp.float32),
            pltpu.VMEM((tb, D_MODEL), jnp.bfloat16),
            pltpu.VMEM((N_TILES, tb, LANES), jnp.float32),
            pltpu.VMEM((2, N_TILES, tb // 2, LANES), jnp.float32),
            pltpu.VMEM((tb + SUBLANES, CONV_WIDTH), jnp.float32),
            pltpu.VMEM((N_TILES, tb // 2, TILE_STATE), jnp.float32),
            pltpu.VMEM((N_TILES, tb // 2, TILE_STATE), jnp.bfloat16),
            pltpu.VMEM((tb, CONV_WIDTH), jnp.bfloat16),
            pltpu.VMEM((tb, SSM_WIDTH), jnp.float32),
            pltpu.VMEM((tb, D_MODEL), jnp.float32),
            pltpu.VMEM((2, N_TILES, tb // 2, LANES), jnp.float32),
            pltpu.VMEM((N_TILES, tb, LANES), jnp.float32),
            pltpu.VMEM((tb, SSM_WIDTH), jnp.bfloat16),
            pltpu.VMEM((tb, D_MODEL), jnp.float32),
            pltpu.VMEM((tb, D_MODEL), jnp.float32),
            pltpu.VMEM((tb, D_MODEL), jnp.bfloat16),
            pltpu.VMEM((tb, D_MODEL), jnp.float32),
            pltpu.VMEM((tb, D_MODEL), jnp.bfloat16),
            pltpu.VMEM((tb, D_MODEL), jnp.float32),
        ],
        compiler_params=pltpu.CompilerParams(
            dimension_semantics=("arbitrary",),
            vmem_limit_bytes=VMEM_LIMIT_BYTES),
        name="hybrid_s5_shortconv_block",
    )(*[a for a, _ in operands])
    return out[None]
```

```python
import math

import jax
import jax.numpy as jnp
from jax import lax
from jax.experimental import pallas as pl
from jax.experimental.pallas import tpu as pltpu

D_MODEL = 1024
PLE_DIM = 256
SSM_WIDTH = 512
SSM_GROUP = 16
SSM_GROUPS = 32
SSM_STATE = 64
CONV_WIDTH = 1024
CONV_K = 3
RMS_EPS = 1e-6
IN_WIDTH = 7168

LANES = 128
MXU_N = 256
SUBLANES = 8
GROUPS_PER_TILE = LANES // SSM_GROUP
N_TILES = SSM_WIDTH // LANES
HALF = GROUPS_PER_TILE * SSM_STATE
TILE_STATE = 2 * HALF
N_COLS = HALF // LANES
N_STATE = SSM_GROUPS * SSM_STATE
ROW_BLOCK = 256
SEG_PAIRS = ROW_BLOCK // SUBLANES // 2
STAGE_ROWS = 64
VMEM_LIMIT_BYTES = 56 * 1024 * 1024

O_U, O_GS, O_XC, O_BC, O_CC, O_GC, O_RS, O_RC = 0, 512, 1024, 2048, 3072, 4096, 5120, 6144


def _rms(v, gain):
    return v * lax.rsqrt(jnp.mean(v * v, axis=-1, keepdims=True) + RMS_EPS) * gain


def _sigmoid(v):
    return 1.0 / (1.0 + jnp.exp(-v))


def _gelu_tanh(v):
    c = math.sqrt(2.0 / math.pi)
    return v * (0.5 * (1.0 + jnp.tanh(c * (v + 0.044715 * (v * v * v)))))


def _cmul(ar, ai, xr, xi):
    return ar * xr - ai * xi, ar * xi + ai * xr


def _expand_tables(xb_ref, yc_ref, sk_ref, a_ref, wb_ref, wc_ref, arow_ref, seg_ref, pw_ref):
    f32 = jnp.float32
    bf16 = jnp.bfloat16
    gmask = GROUPS_PER_TILE - 1

    def iota(shape, dim):
        return lax.broadcasted_iota(jnp.int32, shape, dim)

    def onehot(cond):
        return jnp.where(cond, 1.0, 0.0).astype(bf16)

    r = iota((LANES, TILE_STATE), 0)
    c = iota((LANES, TILE_STATE), 1)
    rep_b = onehot(((c & (SSM_STATE - 1)) == (r & (SSM_STATE - 1))) & (((c >> 7) & 1) == (r >> 6)))
    r = iota((2 * LANES, TILE_STATE), 0)
    c = iota((2 * LANES, TILE_STATE), 1)
    own_b = ((r >> 4) & gmask) == (((c >> 8) << 1) | ((c >> 6) & 1))
    r = iota((2 * SSM_GROUP, 2 * LANES), 0)
    c = iota((2 * SSM_GROUP, 2 * LANES), 1)
    rep_c = onehot(((c >> 7) == (r >> 4)) & ((c & (SSM_GROUP - 1)) == (r & (SSM_GROUP - 1))))
    r = iota((TILE_STATE, 2 * LANES), 0)
    c = iota((TILE_STATE, 2 * LANES), 1)
    own_c = (((r >> 8) << 1) | ((r >> 6) & 1)) == ((c >> 4) & gmask)
    r = iota((2 * LANES, 2 * LANES), 0)
    c = iota((2 * LANES, 2 * LANES), 1)
    own_s = ((r >> 4) & gmask) == ((c >> 4) & gmask)
    for t in range(N_TILES):
        wide = jnp.dot(xb_ref[t].astype(bf16), rep_b, preferred_element_type=f32)
        wb_ref[t] = jnp.where(own_b, wide, 0.0).astype(bf16)
        wide = jnp.dot(yc_ref[t].astype(bf16), rep_c, preferred_element_type=f32)
        wc_ref[t, 0:TILE_STATE, :] = jnp.where(own_c, wide, 0.0).astype(bf16)
        wide = jnp.dot(sk_ref[t].astype(bf16), rep_c, preferred_element_type=f32)
        wc_ref[t, TILE_STATE:TILE_STATE + 2 * LANES, :] = jnp.where(own_s, wide, 0.0).astype(bf16)

    npair = pw_ref.shape[1]
    nbits = npair.bit_length() - 1
    sq = [_cmul(a_ref[0:1, :], a_ref[1:2, :], a_ref[0:1, :], a_ref[1:2, :])]
    while len(sq) < nbits + 3:
        sq.append(_cmul(*sq[-1], *sq[-1]))
    shape = (SUBLANES, N_STATE)
    arow_ref[0] = jnp.broadcast_to(sq[0][0], shape)
    arow_ref[1] = jnp.broadcast_to(sq[0][1], shape)
    pw_ref[0, 0] = jnp.ones(shape, f32)
    pw_ref[1, 0] = jnp.zeros(shape, f32)
    for m in range(1, npair):
        acc = None
        for k in range(nbits):
            if (m >> k) & 1:
                acc = sq[k] if acc is None else _cmul(*acc, *sq[k])
        pw_ref[0, m] = jnp.broadcast_to(acc[0], shape)
        pw_ref[1, m] = jnp.broadcast_to(acc[1], shape)
    row = iota(shape, 0)
    for k, shift in enumerate((1, 2, 4)):
        seg_ref[2 * k] = jnp.where(row >= shift, jnp.broadcast_to(sq[nbits + k][0], shape), 0.0)
        seg_ref[2 * k + 1] = jnp.where(row >= shift, jnp.broadcast_to(sq[nbits + k][1], shape), 0.0)
    seg_ref[6] = jnp.broadcast_to(sq[nbits][0], shape)
    seg_ref[7] = jnp.broadcast_to(sq[nbits][1], shape)


def _load_weights(pairs, stage_ref, sem_ref):
    chunk_rows, stage_cols = stage_ref.shape[1:]
    rounds = []
    for src_ref, dst_ref in pairs:
        rows, width = src_ref.shape
        assert rows % chunk_rows == 0 and width <= stage_cols
        per_round = stage_cols // width
        starts = list(range(0, rows, chunk_rows))
        for i in range(0, len(starts), per_round):
            rounds.append([(src_ref, dst_ref, r0, width, j * width)
                           for j, r0 in enumerate(starts[i:i + per_round])])

    def copies(k):
        return [pltpu.make_async_copy(src_ref.at[pl.ds(r0, chunk_rows), :],
                                      stage_ref.at[k % 2, :, pl.ds(c0, width)],
                                      sem_ref.at[k % 2])
                for src_ref, _, r0, width, c0 in rounds[k]]

    for cp in copies(0):
        cp.start()
    for k, chunks in enumerate(rounds):
        if k + 1 < len(rounds):
            for cp in copies(k + 1):
                cp.start()
        for cp in copies(k):
            cp.wait()
        for _, dst_ref, r0, width, c0 in chunks:
            dst_ref[r0:r0 + chunk_rows, :] = stage_ref[k % 2, :, c0:c0 + width].astype(dst_ref.dtype)


def _block_kernel(x_ref, p_ref, pre_ref, win_hbm, xb_ref, yc_ref, sk_ref, a_ref,
                  wglu_hbm, bglu_ref, wso_hbm, convw_ref, wco_hbm, wo_hbm, post_ref,
                  wple_hbm, wpg_hbm, plen_ref, out_ref,
                  win_ref, wglu_ref, wso_ref, wco_ref, wo_ref, wple_ref, wpg_ref, stage_ref, sem_ref,
                  wb_ref, wc_ref, arow_ref, seg_ref, pw_ref, state_ref, h_ref, un_ref, up_ref, v_ref,
                  sloc_ref, s_ref, z_ref, gs_ref, mg_ref,
                  yp_ref, y_ref, y2_ref, o_ref, x1_ref, x1b_ref, e_ref, mprev_ref, xprev_ref):
    tb = x_ref.shape[0]
    ns = tb // SUBLANES
    npair = ns // 2
    f32 = jnp.float32
    bf16 = jnp.bfloat16

    @pl.when(pl.program_id(0) == 0)
    def _():
        _load_weights([(wo_hbm, wo_ref), (win_hbm, win_ref), (wpg_hbm, wpg_ref),
                       (wple_hbm, wple_ref), (wco_hbm, wco_ref), (wglu_hbm, wglu_ref),
                       (wso_hbm, wso_ref)], stage_ref, sem_ref)
        _expand_tables(xb_ref, yc_ref, sk_ref, a_ref, wb_ref, wc_ref, arow_ref, seg_ref, pw_ref)
        state_ref[...] = jnp.zeros_like(state_ref)
        v_ref[tb:tb + SUBLANES, :] = jnp.zeros((SUBLANES, CONV_WIDTH), f32)
        mprev_ref[...] = jnp.zeros_like(mprev_ref)
        xprev_ref[...] = jnp.zeros_like(xprev_ref)

    def cols(n):
        return slice(n * MXU_N, (n + 1) * MXU_N)

    def tile_dot(lhs_ref, w_ref, off, n):
        return jnp.dot(lhs_ref[...], w_ref[:, off + n * MXU_N:off + (n + 1) * MXU_N],
                       preferred_element_type=f32)


    for n in range(D_MODEL // MXU_N):
        o_ref[:, cols(n)] = tile_dot(mprev_ref, wo_ref, 0, n)

    h_ref[...] = _rms(x_ref[...], pre_ref[...]).astype(bf16)

    for n in range(SSM_WIDTH // MXU_N):
        u = tile_dot(h_ref, win_ref, O_U, n)
        for c in range(MXU_N // LANES):
            un_ref[n * (MXU_N // LANES) + c] = u[:, c * LANES:(c + 1) * LANES]
    for t in range(N_TILES):
        for par in range(2):
            for j in range(SUBLANES):
                for mh in range(npair // SUBLANES):
                    up_ref[par, t, pl.ds(SUBLANES * SUBLANES * mh + j, SUBLANES,
                                         stride=SUBLANES), :] = (
                        un_ref[t, pl.ds(j * ns + 2 * SUBLANES * mh + par, SUBLANES, stride=2), :])

    v_ref[0:SUBLANES, :] = v_ref[tb:tb + SUBLANES, :]
    for n in range(CONV_WIDTH // MXU_N):
        v_ref[SUBLANES:SUBLANES + tb, cols(n)] = (
            tile_dot(h_ref, win_ref, O_CC, n) * tile_dot(h_ref, win_ref, O_XC, n))

    x1 = xprev_ref[...] + _rms(o_ref[...], post_ref[...])
    x1_ref[...] = x1
    x1b_ref[...] = x1.astype(bf16)
    xprev_ref[...] = x_ref[...]

    row0 = lax.broadcasted_iota(jnp.int32, (SUBLANES, LANES), 0) == 0

    def scan_tile(t):
        ub = jnp.concatenate([up_ref[0, t], up_ref[1, t]], axis=1).astype(bf16)
        for q in range(N_COLS):
            bu = jnp.dot(ub, wb_ref[t, :, cols(q)],
                         preferred_element_type=f32)
            re = slice(q * 2 * LANES, q * 2 * LANES + LANES)
            im = slice(q * 2 * LANES + LANES, (q + 1) * 2 * LANES)
            sl = slice(t * HALF + q * LANES, t * HALF + (q + 1) * LANES)
            ar = arow_ref[0, :, sl]
            ai = arow_ref[1, :, sl]
            sr = bu[0:SUBLANES, 0:LANES]
            si = bu[0:SUBLANES, LANES:2 * LANES]
            for m in range(1, npair):
                rows = slice(m * SUBLANES, (m + 1) * SUBLANES)
                sloc_ref[t, rows, re] = sr
                sloc_ref[t, rows, im] = si
                mr, mi = _cmul(ar, ai, sr, si)
                sr = mr + bu[rows, 0:LANES]
                si = mi + bu[rows, LANES:2 * LANES]
            gr = jnp.where(row0, state_ref[0, :, sl], pltpu.roll(sr, 1, 0))
            gi = jnp.where(row0, state_ref[1, :, sl], pltpu.roll(si, 1, 0))
            for k, shift in enumerate((1, 2, 4)):
                fr, fi = _cmul(seg_ref[2 * k, :, sl], seg_ref[2 * k + 1, :, sl],
                               pltpu.roll(gr, shift, 0), pltpu.roll(gi, shift, 0))
                gr, gi = gr + fr, gi + fi
            fr, fi = _cmul(seg_ref[6, :, sl], seg_ref[7, :, sl], gr, gi)
            nr, ni = fr + sr, fi + si
            state_ref[0, :, sl] = jnp.broadcast_to(nr[SUBLANES - 1:SUBLANES, :], nr.shape)
            state_ref[1, :, sl] = jnp.broadcast_to(ni[SUBLANES - 1:SUBLANES, :], ni.shape)
            for m2 in range(npair // 2):
                parts_r, parts_i = [], []
                for m in (2 * m2, 2 * m2 + 1):
                    if m == 0:
                        parts_r.append(gr)
                        parts_i.append(gi)
                        continue
                    rows = slice(m * SUBLANES, (m + 1) * SUBLANES)
                    fr, fi = _cmul(pw_ref[0, m, :, sl], pw_ref[1, m, :, sl], gr, gi)
                    parts_r.append(sloc_ref[t, rows, re] + fr)
                    parts_i.append(sloc_ref[t, rows, im] + fi)
                rows2 = slice(2 * m2 * SUBLANES, (2 * m2 + 2) * SUBLANES)
                s_ref[t, rows2, re] = jnp.concatenate(parts_r, axis=0).astype(bf16)
                s_ref[t, rows2, im] = jnp.concatenate(parts_i, axis=0).astype(bf16)

    scan_tile(0)

    for n in range(CONV_WIDTH // MXU_N):
        bc = tile_dot(h_ref, win_ref, O_BC, n)
        gc = tile_dot(h_ref, win_ref, O_GC, n)
        conv = (convw_ref[0:1, cols(n)] * v_ref[SUBLANES - 2:SUBLANES - 2 + tb, cols(n)]
                + convw_ref[1:2, cols(n)] * v_ref[SUBLANES - 1:SUBLANES - 1 + tb, cols(n)]
                + convw_ref[2:3, cols(n)] * v_ref[SUBLANES:SUBLANES + tb, cols(n)])
        z_ref[:, cols(n)] = (bc * conv * (gc * _sigmoid(gc))).astype(bf16)
        if n == 1:
            scan_tile(1)

    pb = p_ref[...].astype(bf16)
    for n in range(D_MODEL // MXU_N):
        eg = tile_dot(x1b_ref, wpg_ref, 0, n)
        ei = jnp.dot(pb, wple_ref[:, cols(n)], preferred_element_type=f32)
        e_ref[:, cols(n)] = ei * _sigmoid(eg)
        if n == 1:
            scan_tile(2)

    for n in range(SSM_WIDTH // MXU_N):
        gs = tile_dot(h_ref, win_ref, O_GS, n)
        gs_ref[:, cols(n)] = gs * _sigmoid(gs)
    scan_tile(3)

    out_ref[...] = x1_ref[...] + _rms(e_ref[...], plen_ref[...])

    def gated_b(n):
        rc = tile_dot(h_ref, win_ref, O_RC, n)
        yb = tile_dot(z_ref, wco_ref, 0, n)
        mg_ref[:, cols(n)] = _sigmoid(rc) * yb

    gated_b(0)
    gated_b(1)

    for t in range(N_TILES):
        lhs = jnp.concatenate(
            [s_ref[t], up_ref[0, t].astype(bf16), up_ref[1, t].astype(bf16)], axis=1)
        yy = jnp.dot(lhs, wc_ref[t], preferred_element_type=f32)
        yp_ref[0, t] = yy[:, 0:LANES]
        yp_ref[1, t] = yy[:, LANES:2 * LANES]
    gated_b(2)
    gated_b(3)
    for t in range(N_TILES):
        for par in range(2):
            for j in range(SUBLANES):
                for mh in range(npair // SUBLANES):
                    y_ref[t, pl.ds(j * ns + 2 * SUBLANES * mh + par, SUBLANES, stride=2), :] = (
                        _gelu_tanh(yp_ref[par, t, pl.ds(SUBLANES * SUBLANES * mh + j, SUBLANES,
                                                        stride=SUBLANES), :]))
    y = jnp.concatenate([y_ref[t] for t in range(N_TILES)], axis=1)
    yb16 = y.astype(bf16)
    for n in range(SSM_WIDTH // MXU_N):
        glu = jnp.dot(yb16, wglu_ref[:, cols(n)], preferred_element_type=f32)
        y2_ref[:, cols(n)] = (y[:, cols(n)] * _sigmoid(glu + bglu_ref[:, cols(n)])
                              * gs_ref[:, cols(n)]).astype(bf16)

    n_out = D_MODEL // MXU_N
    rs = [tile_dot(h_ref, win_ref, O_RS, n) for n in range(2)]
    for n in range(n_out):
        ya = tile_dot(y2_ref, wso_ref, 0, n)
        mprev_ref[:, cols(n)] = (_sigmoid(rs[n]) * ya + mg_ref[:, cols(n)]).astype(bf16)
        if n + 2 < n_out:
            rs.append(tile_dot(h_ref, win_ref, O_RS, n + 2))


def _ssm_params(a_re, a_im, log_step, b_re, b_im, c_re, c_im, d_skip):
    f32 = jnp.float32
    G, P, H, T, GT = SSM_GROUPS, SSM_STATE, SSM_GROUP, N_TILES, GROUPS_PER_TILE
    dt = jnp.exp(log_step.astype(f32))[:, None]
    mag = jnp.exp(a_re * dt)
    abar_r = mag * jnp.cos(a_im * dt)
    abar_i = mag * jnp.sin(a_im * dt)
    den = a_re * a_re + a_im * a_im
    nr = abar_r - 1.0
    fr = (nr * a_re + abar_i * a_im) / den
    fi = (abar_i * a_re - nr * a_im) / den
    bbar_r = fr[..., None] * b_re - fi[..., None] * b_im
    bbar_i = fr[..., None] * b_im + fi[..., None] * b_re
    ab_r, ab_i = _cmul(abar_r[..., None], abar_i[..., None], bbar_r, bbar_i)
    ca_r, ca_i = _cmul(abar_r[:, None, :], abar_i[:, None, :], c_re, c_im)
    caa_r, caa_i = _cmul(abar_r[:, None, :], abar_i[:, None, :], ca_r, ca_i)

    def re_prod(xr, xi, yr, yi):
        yr_t = jnp.swapaxes(yr, 1, 2)[:, None]
        yi_t = jnp.swapaxes(yi, 1, 2)[:, None]
        return jnp.sum(xr[:, :, None, :] * yr_t - xi[:, :, None, :] * yi_t, axis=-1)

    cb_d = (re_prod(c_re, c_im, bbar_r, bbar_i)
            + jnp.eye(H, dtype=f32) * d_skip.reshape(G, H, 1))
    cab = re_prod(ca_r, ca_i, bbar_r, bbar_i)

    xb = jnp.stack([jnp.stack([ab_r, ab_i]), jnp.stack([bbar_r, bbar_i])])
    xb = xb.reshape(2, 2, T, GT, P, H).transpose(2, 0, 3, 5, 1, 4).reshape(T, 2 * LANES, 2 * P)
    yc = jnp.stack([jnp.stack([ca_r, -ca_i]), jnp.stack([caa_r, -caa_i])])
    yc = yc.reshape(2, 2, T, N_COLS, 2, H, P).transpose(2, 3, 1, 4, 6, 0, 5)
    yc = yc.reshape(T, TILE_STATE, 2 * H)
    zero = jnp.zeros_like(cab)
    sk = jnp.stack([jnp.stack([cb_d, cab]), jnp.stack([zero, cb_d])])
    sk = sk.reshape(2, 2, T, GT, H, H).transpose(2, 0, 3, 5, 1, 4).reshape(T, 2 * LANES, 2 * H)
    a = jnp.stack([abar_r.reshape(N_STATE), abar_i.reshape(N_STATE)])
    return xb, yc, sk, a


def _resident(shape):
    return pl.BlockSpec(shape, lambda i: (0,) * len(shape), pipeline_mode=pl.Buffered(1))


@jax.jit
def kernel(x, p, pre_norm, w_in, a_re, a_im, log_step, b_re, b_im, c_re, c_im, d_skip,
           w_glu, b_glu, w_ssm_out, conv_w, w_conv_out, w_o, post_norm,
           w_ple, w_ple_gate, ple_norm):
    bsz, seqlen, _ = x.shape
    depth = p.shape[0]
    assert bsz == 1 and depth == 1 and seqlen % ROW_BLOCK == 0
    bf16 = jnp.bfloat16
    tb = ROW_BLOCK

    xb, yc, sk, a_bar = _ssm_params(
        a_re[0], a_im[0], log_step[0], b_re[0], b_im[0], c_re[0], c_im[0], d_skip[0])
    nb = seqlen // tb
    cur = lambda i: (jnp.minimum(i, nb - 1), 0)
    prev = lambda i: (jnp.maximum(i - 1, 0), 0)
    hbm = pl.BlockSpec(memory_space=pl.ANY)
    operands = [
        (x[0], pl.BlockSpec((tb, D_MODEL), cur)),
        (p[0, 0], pl.BlockSpec((tb, PLE_DIM), prev)),
        (pre_norm, _resident((1, D_MODEL))),
        (w_in.reshape(D_MODEL, IN_WIDTH), hbm),
        (xb, _resident(xb.shape)),
        (yc, _resident(yc.shape)),
        (sk, _resident(sk.shape)),
        (a_bar, _resident(a_bar.shape)),
        (w_glu.reshape(SSM_WIDTH, SSM_WIDTH), hbm),
        (b_glu, _resident((1, SSM_WIDTH))),
        (w_ssm_out.reshape(SSM_WIDTH, D_MODEL), hbm),
        (conv_w[0], _resident((CONV_K, CONV_WIDTH))),
        (w_conv_out.reshape(CONV_WIDTH, D_MODEL), hbm),
        (w_o.reshape(D_MODEL, D_MODEL), hbm),
        (post_norm, _resident((1, D_MODEL))),
        (w_ple.reshape(PLE_DIM, D_MODEL), hbm),
        (w_ple_gate.reshape(D_MODEL, D_MODEL), hbm),
        (ple_norm, _resident((1, D_MODEL))),
    ]
    out = pl.pallas_call(
        _block_kernel,
        grid=(nb + 1,),
        in_specs=[spec for _, spec in operands],
        out_specs=pl.BlockSpec((tb, D_MODEL), prev),
        out_shape=jax.ShapeDtypeStruct((seqlen, D_MODEL), jnp.float32),
        scratch_shapes=[
            pltpu.VMEM((D_MODEL, IN_WIDTH), bf16),
            pltpu.VMEM((SSM_WIDTH, SSM_WIDTH), bf16),
            pltpu.VMEM((SSM_WIDTH, D_MODEL), bf16),
            pltpu.VMEM((CONV_WIDTH, D_MODEL), bf16),
            pltpu.VMEM((D_MODEL, D_MODEL), bf16),
            pltpu.VMEM((PLE_DIM, D_MODEL), bf16),
            pltpu.VMEM((D_MODEL, D_MODEL), bf16),
            pltpu.VMEM((2, STAGE_ROWS, IN_WIDTH), jnp.float32),
            pltpu.SemaphoreType.DMA((2,)),
            pltpu.VMEM((N_TILES, 2 * LANES, TILE_STATE), jnp.bfloat16),
            pltpu.VMEM((N_TILES, TILE_STATE + 2 * LANES, 2 * LANES), jnp.bfloat16),
            pltpu.VMEM((2, SUBLANES, N_STATE), jnp.float32),
            pltpu.VMEM((8, SUBLANES, N_STATE), jnp.float32),
            pltpu.VMEM((2, SEG_PAIRS, SUBLANES, N_STATE), jnp.float32),
            pltpu.VMEM((2, SUBLANES, N_STATE), jnp.float32),
            pltpu.VMEM((tb, D_MODEL), jnp.bfloat16),
            pltpu.VMEM((N_TILES, tb, LANES), jnp.float32),
            pltpu.VMEM((2, N_TILES, tb // 2, LANES), jnp.float32),
            pltpu.VMEM((tb + SUBLANES, CONV_WIDTH), jnp.float32),
            pltpu.VMEM((N_TILES, tb // 2, TILE_STATE), jnp.float32),
            pltpu.VMEM((N_TILES, tb // 2, TILE_STATE), jnp.bfloat16),
            pltpu.VMEM((tb, CONV_WIDTH), jnp.bfloat16),
            pltpu.VMEM((tb, SSM_WIDTH), jnp.float32),
            pltpu.VMEM((tb, D_MODEL), jnp.float32),
            pltpu.VMEM((2, N_TILES, tb // 2, LANES), jnp.float32),
            pltpu.VMEM((N_TILES, tb, LANES), jnp.float32),
            pltpu.VMEM((tb, SSM_WIDTH), jnp.bfloat16),
            pltpu.VMEM((tb, D_MODEL), jnp.float32),
            pltpu.VMEM((tb, D_MODEL), jnp.float32),
            pltpu.VMEM((tb, D_MODEL), jnp.bfloat16),
            pltpu.VMEM((tb, D_MODEL), jnp.float32),
            pltpu.VMEM((tb, D_MODEL), jnp.bfloat16),
            pltpu.VMEM((tb, D_MODEL), jnp.float32),
        ],
        compiler_params=pltpu.CompilerParams(
            dimension_semantics=("arbitrary",),
            vmem_limit_bytes=VMEM_LIMIT_BYTES),
        name="hybrid_s5_shortconv_block",
    )(*[a for a, _ in operands])
    return out[None]
```

```python
import math

import jax
import jax.numpy as jnp
from jax import lax
from jax.experimental import pallas as pl
from jax.experimental.pallas import tpu as pltpu

D_MODEL = 1024
PLE_DIM = 256
SSM_WIDTH = 512
SSM_GROUP = 16
SSM_GROUPS = 32
SSM_STATE = 64
CONV_WIDTH = 1024
CONV_K = 3
RMS_EPS = 1e-6
IN_WIDTH = 7168

LANES = 128
MXU_N = 256
SUBLANES = 8
GROUPS_PER_TILE = LANES // SSM_GROUP
N_TILES = SSM_WIDTH // LANES
HALF = GROUPS_PER_TILE * SSM_STATE
TILE_STATE = 2 * HALF
N_COLS = HALF // LANES
N_STATE = SSM_GROUPS * SSM_STATE
ROW_BLOCK = 384
SEG_PAIRS = ROW_BLOCK // SUBLANES // 2
STAGE_ROWS = 64
VMEM_LIMIT_BYTES = 56 * 1024 * 1024

O_U, O_GS, O_XC, O_BC, O_CC, O_GC, O_RS, O_RC = 0, 512, 1024, 2048, 3072, 4096, 5120, 6144


def _rms(v, gain):
    return v * lax.rsqrt(jnp.mean(v * v, axis=-1, keepdims=True) + RMS_EPS) * gain


def _sigmoid(v):
    return 1.0 / (1.0 + jnp.exp(-v))


def _gelu_tanh(v):
    c = math.sqrt(2.0 / math.pi)
    return v * (0.5 * (1.0 + jnp.tanh(c * (v + 0.044715 * (v * v * v)))))


def _cmul(ar, ai, xr, xi):
    return ar * xr - ai * xi, ar * xi + ai * xr


def _expand_tables(xb_ref, yc_ref, sk_ref, a_ref, wb_ref, wc_ref, arow_ref, seg_ref, pw_ref):
    f32 = jnp.float32
    bf16 = jnp.bfloat16
    gmask = GROUPS_PER_TILE - 1

    def iota(shape, dim):
        return lax.broadcasted_iota(jnp.int32, shape, dim)

    def onehot(cond):
        return jnp.where(cond, 1.0, 0.0).astype(bf16)

    r = iota((LANES, TILE_STATE), 0)
    c = iota((LANES, TILE_STATE), 1)
    rep_b = onehot(((c & (SSM_STATE - 1)) == (r & (SSM_STATE - 1))) & (((c >> 7) & 1) == (r >> 6)))
    r = iota((2 * LANES, TILE_STATE), 0)
    c = iota((2 * LANES, TILE_STATE), 1)
    own_b = ((r >> 4) & gmask) == (((c >> 8) << 1) | ((c >> 6) & 1))
    r = iota((2 * SSM_GROUP, 2 * LANES), 0)
    c = iota((2 * SSM_GROUP, 2 * LANES), 1)
    rep_c = onehot(((c >> 7) == (r >> 4)) & ((c & (SSM_GROUP - 1)) == (r & (SSM_GROUP - 1))))
    r = iota((TILE_STATE, 2 * LANES), 0)
    c = iota((TILE_STATE, 2 * LANES), 1)
    own_c = (((r >> 8) << 1) | ((r >> 6) & 1)) == ((c >> 4) & gmask)
    r = iota((2 * LANES, 2 * LANES), 0)
    c = iota((2 * LANES, 2 * LANES), 1)
    own_s = ((r >> 4) & gmask) == ((c >> 4) & gmask)
    for t in range(N_TILES):
        wide = jnp.dot(xb_ref[t].astype(bf16), rep_b, preferred_element_type=f32)
        wb_ref[t] = jnp.where(own_b, wide, 0.0).astype(bf16)
        wide = jnp.dot(yc_ref[t].astype(bf16), rep_c, preferred_element_type=f32)
        wc_ref[t, 0:TILE_STATE, :] = jnp.where(own_c, wide, 0.0).astype(bf16)
        wide = jnp.dot(sk_ref[t].astype(bf16), rep_c, preferred_element_type=f32)
        wc_ref[t, TILE_STATE:TILE_STATE + 2 * LANES, :] = jnp.where(own_s, wide, 0.0).astype(bf16)

    npair = pw_ref.shape[1]
    sq = [_cmul(a_ref[0:1, :], a_ref[1:2, :], a_ref[0:1, :], a_ref[1:2, :])]
    while len(sq) < npair.bit_length():
        sq.append(_cmul(*sq[-1], *sq[-1]))

    def q_power(m):
        acc = None
        for k in range(m.bit_length()):
            if (m >> k) & 1:
                acc = sq[k] if acc is None else _cmul(*acc, *sq[k])
        return acc

    shape = (SUBLANES, N_STATE)
    arow_ref[0] = jnp.broadcast_to(sq[0][0], shape)
    arow_ref[1] = jnp.broadcast_to(sq[0][1], shape)
    pw_ref[0, 0:1, :] = jnp.ones((1, N_STATE), f32)
    pw_ref[1, 0:1, :] = jnp.zeros((1, N_STATE), f32)
    for m in range(1, npair):
        pw_ref[0, m:m + 1, :], pw_ref[1, m:m + 1, :] = q_power(m)
    row = iota(shape, 0)
    a_seg = q_power(npair)
    seg_ref[6] = jnp.broadcast_to(a_seg[0], shape)
    seg_ref[7] = jnp.broadcast_to(a_seg[1], shape)
    for k, shift in enumerate((1, 2, 4)):
        seg_ref[2 * k] = jnp.where(row >= shift, jnp.broadcast_to(a_seg[0], shape), 0.0)
        seg_ref[2 * k + 1] = jnp.where(row >= shift, jnp.broadcast_to(a_seg[1], shape), 0.0)
        a_seg = _cmul(*a_seg, *a_seg)


def _load_weights(pairs, stage_ref, sem_ref):
    chunk_rows, stage_cols = stage_ref.shape[1:]
    rounds = []
    for src_ref, dst_ref in pairs:
        rows, width = src_ref.shape
        assert rows % chunk_rows == 0 and width <= stage_cols
        per_round = stage_cols // width
        starts = list(range(0, rows, chunk_rows))
        for i in range(0, len(starts), per_round):
            rounds.append([(src_ref, dst_ref, r0, width, j * width)
                           for j, r0 in enumerate(starts[i:i + per_round])])

    def copies(k):
        return [pltpu.make_async_copy(src_ref.at[pl.ds(r0, chunk_rows), :],
                                      stage_ref.at[k % 2, :, pl.ds(c0, width)],
                                      sem_ref.at[k % 2])
                for src_ref, _, r0, width, c0 in rounds[k]]

    for cp in copies(0):
        cp.start()
    for k, chunks in enumerate(rounds):
        if k + 1 < len(rounds):
            for cp in copies(k + 1):
                cp.start()
        for cp in copies(k):
            cp.wait()
        for _, dst_ref, r0, width, c0 in chunks:
            dst_ref[r0:r0 + chunk_rows, :] = stage_ref[k % 2, :, c0:c0 + width].astype(dst_ref.dtype)


def _block_kernel(x_ref, p_ref, pre_ref, win_hbm, xb_ref, yc_ref, sk_ref, a_ref,
                  wglu_hbm, bglu_ref, wso_hbm, convw_ref, wco_hbm, wo_hbm, post_ref,
                  wple_hbm, wpg_hbm, plen_ref, out_ref,
                  win_ref, wglu_ref, wso_ref, wco_ref, wo_ref, wple_ref, wpg_ref, stage_ref, sem_ref,
                  wb_ref, wc_ref, arow_ref, seg_ref, pw_ref, state_ref, h_ref, un_ref, up_ref, v_ref,
                  sloc_ref, s_ref, z_ref, gs_ref, mg_ref,
                  yp_ref, y_ref, y2_ref, o_ref, x1_ref, x1b_ref, e_ref, mprev_ref, xprev_ref):
    tb = x_ref.shape[0]
    ns = tb // SUBLANES
    npair = ns // 2
    f32 = jnp.float32
    bf16 = jnp.bfloat16

    @pl.when(pl.program_id(0) == 0)
    def _():
        _load_weights([(wo_hbm, wo_ref), (win_hbm, win_ref), (wpg_hbm, wpg_ref),
                       (wple_hbm, wple_ref), (wco_hbm, wco_ref), (wglu_hbm, wglu_ref),
                       (wso_hbm, wso_ref)], stage_ref, sem_ref)
        _expand_tables(xb_ref, yc_ref, sk_ref, a_ref, wb_ref, wc_ref, arow_ref, seg_ref, pw_ref)
        state_ref[...] = jnp.zeros_like(state_ref)
        v_ref[tb:tb + SUBLANES, :] = jnp.zeros((SUBLANES, CONV_WIDTH), f32)
        mprev_ref[...] = jnp.zeros_like(mprev_ref)
        xprev_ref[...] = jnp.zeros_like(xprev_ref)

    def cols(n):
        return slice(n * MXU_N, (n + 1) * MXU_N)

    def tile_dot(lhs_ref, w_ref, off, n):
        return jnp.dot(lhs_ref[...], w_ref[:, off + n * MXU_N:off + (n + 1) * MXU_N],
                       preferred_element_type=f32)


    for n in range(D_MODEL // MXU_N):
        o_ref[:, cols(n)] = tile_dot(mprev_ref, wo_ref, 0, n)

    h_ref[...] = _rms(x_ref[...], pre_ref[...]).astype(bf16)

    for n in range(SSM_WIDTH // MXU_N):
        u = tile_dot(h_ref, win_ref, O_U, n)
        for c in range(MXU_N // LANES):
            un_ref[n * (MXU_N // LANES) + c] = u[:, c * LANES:(c + 1) * LANES]
    for t in range(N_TILES):
        for par in range(2):
            for j in range(SUBLANES):
                for mh in range(npair // SUBLANES):
                    up_ref[par, t, pl.ds(SUBLANES * SUBLANES * mh + j, SUBLANES,
                                         stride=SUBLANES), :] = (
                        un_ref[t, pl.ds(j * ns + 2 * SUBLANES * mh + par, SUBLANES, stride=2), :])

    v_ref[0:SUBLANES, :] = v_ref[tb:tb + SUBLANES, :]
    for n in range(CONV_WIDTH // MXU_N):
        v_ref[SUBLANES:SUBLANES + tb, cols(n)] = (
            tile_dot(h_ref, win_ref, O_CC, n) * tile_dot(h_ref, win_ref, O_XC, n))

    x1 = xprev_ref[...] + _rms(o_ref[...], post_ref[...])
    x1_ref[...] = x1
    x1b_ref[...] = x1.astype(bf16)
    xprev_ref[...] = x_ref[...]

    srow = lax.broadcasted_iota(jnp.int32, (SUBLANES, LANES), 0)
    row0 = srow == 0

    def scan_tile(t):
        ub = jnp.concatenate([up_ref[0, t], up_ref[1, t]], axis=1).astype(bf16)
        for q in range(N_COLS):
            bu = jnp.dot(ub, wb_ref[t, :, cols(q)],
                         preferred_element_type=f32)
            re = slice(q * 2 * LANES, q * 2 * LANES + LANES)
            im = slice(q * 2 * LANES + LANES, (q + 1) * 2 * LANES)
            sl = slice(t * HALF + q * LANES, t * HALF + (q + 1) * LANES)
            ar = arow_ref[0, :, sl]
            ai = arow_ref[1, :, sl]
            sr = bu[0:SUBLANES, 0:LANES]
            si = bu[0:SUBLANES, LANES:2 * LANES]
            for m in range(1, npair):
                rows = slice(m * SUBLANES, (m + 1) * SUBLANES)
                sloc_ref[t, rows, re] = sr
                sloc_ref[t, rows, im] = si
                mr, mi = _cmul(ar, ai, sr, si)
                sr = mr + bu[rows, 0:LANES]
                si = mi + bu[rows, LANES:2 * LANES]
            gr = jnp.where(row0, state_ref[0, :, sl], pltpu.roll(sr, 1, 0))
            gi = jnp.where(row0, state_ref[1, :, sl], pltpu.roll(si, 1, 0))
            for k, shift in enumerate((1, 2, 4)):
                keep = srow >= shift
                fr, fi = _cmul(seg_ref[2 * k, :, sl], seg_ref[2 * k + 1, :, sl],
                               jnp.where(keep, pltpu.roll(gr, shift, 0), 0.0),
                               jnp.where(keep, pltpu.roll(gi, shift, 0), 0.0))
                gr, gi = gr + fr, gi + fi
            fr, fi = _cmul(seg_ref[6, :, sl], seg_ref[7, :, sl], gr, gi)
            nr, ni = fr + sr, fi + si
            state_ref[0, :, sl] = jnp.broadcast_to(nr[SUBLANES - 1:SUBLANES, :], nr.shape)
            state_ref[1, :, sl] = jnp.broadcast_to(ni[SUBLANES - 1:SUBLANES, :], ni.shape)
            for m2 in range(npair // 2):
                parts_r, parts_i = [], []
                for m in (2 * m2, 2 * m2 + 1):
                    if m == 0:
                        parts_r.append(gr)
                        parts_i.append(gi)
                        continue
                    rows = slice(m * SUBLANES, (m + 1) * SUBLANES)
                    fr, fi = _cmul(jnp.broadcast_to(pw_ref[0, m:m + 1, sl], gr.shape),
                                   jnp.broadcast_to(pw_ref[1, m:m + 1, sl], gi.shape), gr, gi)
                    parts_r.append(sloc_ref[t, rows, re] + fr)
                    parts_i.append(sloc_ref[t, rows, im] + fi)
                rows2 = slice(2 * m2 * SUBLANES, (2 * m2 + 2) * SUBLANES)
                s_ref[t, rows2, re] = jnp.concatenate(parts_r, axis=0).astype(bf16)
                s_ref[t, rows2, im] = jnp.concatenate(parts_i, axis=0).astype(bf16)

    scan_tile(0)

    for n in range(CONV_WIDTH // MXU_N):
        bc = tile_dot(h_ref, win_ref, O_BC, n)
        gc = tile_dot(h_ref, win_ref, O_GC, n)
        conv = (convw_ref[0:1, cols(n)] * v_ref[SUBLANES - 2:SUBLANES - 2 + tb, cols(n)]
                + convw_ref[1:2, cols(n)] * v_ref[SUBLANES - 1:SUBLANES - 1 + tb, cols(n)]
                + convw_ref[2:3, cols(n)] * v_ref[SUBLANES:SUBLANES + tb, cols(n)])
        z_ref[:, cols(n)] = (bc * conv * (gc * _sigmoid(gc))).astype(bf16)
        if n == 1:
            scan_tile(1)

    pb = p_ref[...].astype(bf16)
    for n in range(D_MODEL // MXU_N):
        eg = tile_dot(x1b_ref, wpg_ref, 0, n)
        ei = jnp.dot(pb, wple_ref[:, cols(n)], preferred_element_type=f32)
        e_ref[:, cols(n)] = ei * _sigmoid(eg)
        if n == 1:
            scan_tile(2)

    for n in range(SSM_WIDTH // MXU_N):
        gs = tile_dot(h_ref, win_ref, O_GS, n)
        gs_ref[:, cols(n)] = gs * _sigmoid(gs)
    scan_tile(3)

    out_ref[...] = x1_ref[...] + _rms(e_ref[...], plen_ref[...])

    def gated_b(n):
        rc = tile_dot(h_ref, win_ref, O_RC, n)
        yb = tile_dot(z_ref, wco_ref, 0, n)
        mg_ref[:, cols(n)] = _sigmoid(rc) * yb

    gated_b(0)
    gated_b(1)

    for t in range(N_TILES):
        lhs = jnp.concatenate(
            [s_ref[t], up_ref[0, t].astype(bf16), up_ref[1, t].astype(bf16)], axis=1)
        yy = jnp.dot(lhs, wc_ref[t], preferred_element_type=f32)
        yp_ref[0, t] = yy[:, 0:LANES]
        yp_ref[1, t] = yy[:, LANES:2 * LANES]
    gated_b(2)
    gated_b(3)
    for t in range(N_TILES):
        for par in range(2):
            for j in range(SUBLANES):
                for mh in range(npair // SUBLANES):
                    y_ref[t, pl.ds(j * ns + 2 * SUBLANES * mh + par, SUBLANES, stride=2), :] = (
                        _gelu_tanh(yp_ref[par, t, pl.ds(SUBLANES * SUBLANES * mh + j, SUBLANES,
                                                        stride=SUBLANES), :]))
    y = jnp.concatenate([y_ref[t] for t in range(N_TILES)], axis=1)
    yb16 = y.astype(bf16)
    for n in range(SSM_WIDTH // MXU_N):
        glu = jnp.dot(yb16, wglu_ref[:, cols(n)], preferred_element_type=f32)
        y2_ref[:, cols(n)] = (y[:, cols(n)] * _sigmoid(glu + bglu_ref[:, cols(n)])
                              * gs_ref[:, cols(n)]).astype(bf16)

    n_out = D_MODEL // MXU_N
    rs = [tile_dot(h_ref, win_ref, O_RS, n) for n in range(2)]
    for n in range(n_out):
        ya = tile_dot(y2_ref, wso_ref, 0, n)
        mprev_ref[:, cols(n)] = (_sigmoid(rs[n]) * ya + mg_ref[:, cols(n)]).astype(bf16)
        if n + 2 < n_out:
            rs.append(tile_dot(h_ref, win_ref, O_RS, n + 2))


def _ssm_params(a_re, a_im, log_step, b_re, b_im, c_re, c_im, d_skip):
    f32 = jnp.float32
    G, P, H, T, GT = SSM_GROUPS, SSM_STATE, SSM_GROUP, N_TILES, GROUPS_PER_TILE
    dt = jnp.exp(log_step.astype(f32))[:, None]
    mag = jnp.exp(a_re * dt)
    abar_r = mag * jnp.cos(a_im * dt)
    abar_i = mag * jnp.sin(a_im * dt)
    den = a_re * a_re + a_im * a_im
    nr = abar_r - 1.0
    fr = (nr * a_re + abar_i * a_im) / den
    fi = (abar_i * a_re - nr * a_im) / den
    bbar_r = fr[..., None] * b_re - fi[..., None] * b_im
    bbar_i = fr[..., None] * b_im + fi[..., None] * b_re
    ab_r, ab_i = _cmul(abar_r[..., None], abar_i[..., None], bbar_r, bbar_i)
    ca_r, ca_i = _cmul(abar_r[:, None, :], abar_i[:, None, :], c_re, c_im)
    caa_r, caa_i = _cmul(abar_r[:, None, :], abar_i[:, None, :], ca_r, ca_i)

    def re_prod(xr, xi, yr, yi):
        yr_t = jnp.swapaxes(yr, 1, 2)[:, None]
        yi_t = jnp.swapaxes(yi, 1, 2)[:, None]
        return jnp.sum(xr[:, :, None, :] * yr_t - xi[:, :, None, :] * yi_t, axis=-1)

    cb_d = (re_prod(c_re, c_im, bbar_r, bbar_i)
            + jnp.eye(H, dtype=f32) * d_skip.reshape(G, H, 1))
    cab = re_prod(ca_r, ca_i, bbar_r, bbar_i)

    xb = jnp.stack([jnp.stack([ab_r, ab_i]), jnp.stack([bbar_r, bbar_i])])
    xb = xb.reshape(2, 2, T, GT, P, H).transpose(2, 0, 3, 5, 1, 4).reshape(T, 2 * LANES, 2 * P)
    yc = jnp.stack([jnp.stack([ca_r, -ca_i]), jnp.stack([caa_r, -caa_i])])
    yc = yc.reshape(2, 2, T, N_COLS, 2, H, P).transpose(2, 3, 1, 4, 6, 0, 5)
    yc = yc.reshape(T, TILE_STATE, 2 * H)
    zero = jnp.zeros_like(cab)
    sk = jnp.stack([jnp.stack([cb_d, cab]), jnp.stack([zero, cb_d])])
    sk = sk.reshape(2, 2, T, GT, H, H).transpose(2, 0, 3, 5, 1, 4).reshape(T, 2 * LANES, 2 * H)
    a = jnp.stack([abar_r.reshape(N_STATE), abar_i.reshape(N_STATE)])
    bf16 = jnp.bfloat16
    return xb.astype(bf16), yc.astype(bf16), sk.astype(bf16), a


def _resident(shape):
    return pl.BlockSpec(shape, lambda i: (0,) * len(shape), pipeline_mode=pl.Buffered(1))


@jax.jit
def kernel(x, p, pre_norm, w_in, a_re, a_im, log_step, b_re, b_im, c_re, c_im, d_skip,
           w_glu, b_glu, w_ssm_out, conv_w, w_conv_out, w_o, post_norm,
           w_ple, w_ple_gate, ple_norm):
    bsz, seqlen, _ = x.shape
    depth = p.shape[0]
    assert bsz == 1 and depth == 1
    bf16 = jnp.bfloat16
    tb = ROW_BLOCK

    xb, yc, sk, a_bar = _ssm_params(
        a_re[0], a_im[0], log_step[0], b_re[0], b_im[0], c_re[0], c_im[0], d_skip[0])
    nb = pl.cdiv(seqlen, tb)
    cur = lambda i: (jnp.minimum(i, nb - 1), 0)
    prev = lambda i: (jnp.maximum(i - 1, 0), 0)
    hbm = pl.BlockSpec(memory_space=pl.ANY)
    operands = [
        (x[0], pl.BlockSpec((tb, D_MODEL), cur)),
        (p[0, 0], pl.BlockSpec((tb, PLE_DIM), prev)),
        (pre_norm, _resident((1, D_MODEL))),
        (w_in.reshape(D_MODEL, IN_WIDTH), hbm),
        (xb, _resident(xb.shape)),
        (yc, _resident(yc.shape)),
        (sk, _resident(sk.shape)),
        (a_bar, _resident(a_bar.shape)),
        (w_glu.reshape(SSM_WIDTH, SSM_WIDTH), hbm),
        (b_glu, _resident((1, SSM_WIDTH))),
        (w_ssm_out.reshape(SSM_WIDTH, D_MODEL), hbm),
        (conv_w[0], _resident((CONV_K, CONV_WIDTH))),
        (w_conv_out.reshape(CONV_WIDTH, D_MODEL), hbm),
        (w_o.reshape(D_MODEL, D_MODEL), hbm),
        (post_norm, _resident((1, D_MODEL))),
        (w_ple.reshape(PLE_DIM, D_MODEL), hbm),
        (w_ple_gate.reshape(D_MODEL, D_MODEL), hbm),
        (ple_norm, _resident((1, D_MODEL))),
    ]
    out = pl.pallas_call(
        _block_kernel,
        grid=(nb + 1,),
        in_specs=[spec for _, spec in operands],
        out_specs=pl.BlockSpec((tb, D_MODEL), prev),
        out_shape=jax.ShapeDtypeStruct((seqlen, D_MODEL), jnp.float32),
        scratch_shapes=[
            pltpu.VMEM((D_MODEL, IN_WIDTH), bf16),
            pltpu.VMEM((SSM_WIDTH, SSM_WIDTH), bf16),
            pltpu.VMEM((SSM_WIDTH, D_MODEL), bf16),
            pltpu.VMEM((CONV_WIDTH, D_MODEL), bf16),
            pltpu.VMEM((D_MODEL, D_MODEL), bf16),
            pltpu.VMEM((PLE_DIM, D_MODEL), bf16),
            pltpu.VMEM((D_MODEL, D_MODEL), bf16),
            pltpu.VMEM((2, STAGE_ROWS, IN_WIDTH), jnp.float32),
            pltpu.SemaphoreType.DMA((2,)),
            pltpu.VMEM((N_TILES, 2 * LANES, TILE_STATE), jnp.bfloat16),
            pltpu.VMEM((N_TILES, TILE_STATE + 2 * LANES, 2 * LANES), jnp.bfloat16),
            pltpu.VMEM((2, SUBLANES, N_STATE), jnp.float32),
            pltpu.VMEM((8, SUBLANES, N_STATE), jnp.float32),
            pltpu.VMEM((2, SEG_PAIRS, N_STATE), jnp.float32),
            pltpu.VMEM((2, SUBLANES, N_STATE), jnp.float32),
            pltpu.VMEM((tb, D_MODEL), jnp.bfloat16),
            pltpu.VMEM((N_TILES, tb, LANES), jnp.float32),
            pltpu.VMEM((2, N_TILES, tb // 2, LANES), jnp.float32),
            pltpu.VMEM((tb + SUBLANES, CONV_WIDTH), jnp.float32),
            pltpu.VMEM((N_TILES, tb // 2, TILE_STATE), jnp.float32),
            pltpu.VMEM((N_TILES, tb // 2, TILE_STATE), jnp.bfloat16),
            pltpu.VMEM((tb, CONV_WIDTH), jnp.bfloat16),
            pltpu.VMEM((tb, SSM_WIDTH), jnp.float32),
            pltpu.VMEM((tb, D_MODEL), jnp.float32),
            pltpu.VMEM((2, N_TILES, tb // 2, LANES), jnp.float32),
            pltpu.VMEM((N_TILES, tb, LANES), jnp.float32),
            pltpu.VMEM((tb, SSM_WIDTH), jnp.bfloat16),
            pltpu.VMEM((tb, D_MODEL), jnp.float32),
            pltpu.VMEM((tb, D_MODEL), jnp.float32),
            pltpu.VMEM((tb, D_MODEL), jnp.bfloat16),
            pltpu.VMEM((tb, D_MODEL), jnp.float32),
            pltpu.VMEM((tb, D_MODEL), jnp.bfloat16),
            pltpu.VMEM((tb, D_MODEL), jnp.float32),
        ],
        compiler_params=pltpu.CompilerParams(
            dimension_semantics=("arbitrary",),
            vmem_limit_bytes=VMEM_LIMIT_BYTES),
        name="hybrid_s5_shortconv_block",
    )(*[a for a, _ in operands])
    return out[None]
```

```python
import math

import jax
import jax.numpy as jnp
from jax import lax
from jax.experimental import pallas as pl
from jax.experimental.pallas import tpu as pltpu

D_MODEL = 1024
PLE_DIM = 256
SSM_WIDTH = 512
SSM_GROUP = 16
SSM_GROUPS = 32
SSM_STATE = 64
CONV_WIDTH = 1024
CONV_K = 3
RMS_EPS = 1e-6
IN_WIDTH = 7168

LANES = 128
MXU_N = 256
SUBLANES = 8
ROW_TILE = 16
GROUPS_PER_TILE = LANES // SSM_GROUP
N_TILES = SSM_WIDTH // LANES
HALF = GROUPS_PER_TILE * SSM_STATE
TILE_STATE = 2 * HALF
N_COLS = HALF // LANES
N_STATE = SSM_GROUPS * SSM_STATE
ROW_BLOCK = 384
SEG_PAIRS = ROW_BLOCK // SUBLANES // 2
STAGE_ROWS = 64
VMEM_LIMIT_BYTES = 56 * 1024 * 1024

O_U, O_GS, O_XC, O_BC, O_CC, O_GC, O_RS, O_RC = 0, 512, 1024, 2048, 3072, 4096, 5120, 6144


def _rms(v, gain):
    return v * lax.rsqrt(jnp.mean(v * v, axis=-1, keepdims=True) + RMS_EPS) * gain


def _sigmoid(v):
    return 1.0 / (1.0 + jnp.exp(-v))


def _gelu_tanh(v):
    c = math.sqrt(2.0 / math.pi)
    return v * (0.5 * (1.0 + jnp.tanh(c * (v + 0.044715 * (v * v * v)))))


def _cmul(ar, ai, xr, xi):
    return ar * xr - ai * xi, ar * xi + ai * xr


def _expand_tables(xb_ref, yc_ref, sk_ref, a_ref, wb_ref, wc_ref, arow_ref, seg_ref):
    f32 = jnp.float32
    bf16 = jnp.bfloat16
    gmask = GROUPS_PER_TILE - 1

    def iota(shape, dim):
        return lax.broadcasted_iota(jnp.int32, shape, dim)

    def onehot(cond):
        return jnp.where(cond, 1.0, 0.0).astype(bf16)

    r = iota((LANES, TILE_STATE), 0)
    c = iota((LANES, TILE_STATE), 1)
    rep_b = onehot(((c & (SSM_STATE - 1)) == (r & (SSM_STATE - 1))) & (((c >> 7) & 1) == (r >> 6)))
    r = iota((2 * LANES, TILE_STATE), 0)
    c = iota((2 * LANES, TILE_STATE), 1)
    own_b = ((r >> 4) & gmask) == (((c >> 8) << 1) | ((c >> 6) & 1))
    r = iota((2 * SSM_GROUP, 2 * LANES), 0)
    c = iota((2 * SSM_GROUP, 2 * LANES), 1)
    rep_c = onehot(((c >> 7) == (r >> 4)) & ((c & (SSM_GROUP - 1)) == (r & (SSM_GROUP - 1))))
    r = iota((TILE_STATE, 2 * LANES), 0)
    c = iota((TILE_STATE, 2 * LANES), 1)
    own_c = (((r >> 8) << 1) | ((r >> 6) & 1)) == ((c >> 4) & gmask)
    r = iota((2 * LANES, 2 * LANES), 0)
    c = iota((2 * LANES, 2 * LANES), 1)
    own_s = ((r >> 4) & gmask) == ((c >> 4) & gmask)
    for t in range(N_TILES):
        wide = jnp.dot(xb_ref[t].astype(bf16), rep_b, preferred_element_type=f32)
        wb_ref[t] = jnp.where(own_b, wide, 0.0).astype(bf16)
        wide = jnp.dot(yc_ref[t].astype(bf16), rep_c, preferred_element_type=f32)
        wc_ref[t, 0:TILE_STATE, :] = jnp.where(own_c, wide, 0.0).astype(bf16)
        wide = jnp.dot(sk_ref[t].astype(bf16), rep_c, preferred_element_type=f32)
        wc_ref[t, TILE_STATE:TILE_STATE + 2 * LANES, :] = jnp.where(own_s, wide, 0.0).astype(bf16)

    sq = [_cmul(a_ref[0:1, :], a_ref[1:2, :], a_ref[0:1, :], a_ref[1:2, :])]
    while len(sq) < SEG_PAIRS.bit_length():
        sq.append(_cmul(*sq[-1], *sq[-1]))
    a_seg = None
    for k in range(SEG_PAIRS.bit_length()):
        if (SEG_PAIRS >> k) & 1:
            a_seg = sq[k] if a_seg is None else _cmul(*a_seg, *sq[k])
    shape = (SUBLANES, N_STATE)
    arow_ref[0] = jnp.broadcast_to(sq[0][0], shape)
    arow_ref[1] = jnp.broadcast_to(sq[0][1], shape)
    row = iota(shape, 0)
    seg_ref[6] = jnp.broadcast_to(a_seg[0], shape)
    seg_ref[7] = jnp.broadcast_to(a_seg[1], shape)
    for k, shift in enumerate((1, 2, 4)):
        seg_ref[2 * k] = jnp.where(row >= shift, jnp.broadcast_to(a_seg[0], shape), 0.0)
        seg_ref[2 * k + 1] = jnp.where(row >= shift, jnp.broadcast_to(a_seg[1], shape), 0.0)
        a_seg = _cmul(*a_seg, *a_seg)


def _load_weights(pairs, stage_ref, sem_ref):
    chunk_rows, stage_cols = stage_ref.shape[1:]
    rounds = []
    for src_ref, dst_ref in pairs:
        rows, width = src_ref.shape
        assert rows % chunk_rows == 0 and width <= stage_cols
        per_round = stage_cols // width
        starts = list(range(0, rows, chunk_rows))
        for i in range(0, len(starts), per_round):
            rounds.append([(src_ref, dst_ref, r0, width, j * width)
                           for j, r0 in enumerate(starts[i:i + per_round])])

    def copies(k):
        return [pltpu.make_async_copy(src_ref.at[pl.ds(r0, chunk_rows), :],
                                      stage_ref.at[k % 2, :, pl.ds(c0, width)],
                                      sem_ref.at[k % 2])
                for src_ref, _, r0, width, c0 in rounds[k]]

    for cp in copies(0):
        cp.start()
    for k, chunks in enumerate(rounds):
        if k + 1 < len(rounds):
            for cp in copies(k + 1):
                cp.start()
        for cp in copies(k):
            cp.wait()
        for _, dst_ref, r0, width, c0 in chunks:
            dst_ref[r0:r0 + chunk_rows, :] = stage_ref[k % 2, :, c0:c0 + width].astype(dst_ref.dtype)


def _block_kernel(x_ref, p_ref, pre_ref, win_hbm, xb_ref, yc_ref, sk_ref, a_ref,
                  wglu_hbm, bglu_ref, wso_hbm, convw_ref, wco_hbm, wo_hbm, post_ref,
                  wple_hbm, wpg_hbm, plen_ref, out_ref,
                  win_ref, wglu_ref, wso_ref, wco_ref, wo_ref, wple_ref, wpg_ref, stage_ref, sem_ref,
                  wb_ref, wc_ref, arow_ref, seg_ref, state_ref, h_ref, un_ref, up_ref, v_ref,
                  sloc_ref, s_ref, z_ref, gs_ref, mg_ref,
                  yp_ref, y_ref, y2_ref, o_ref, x1_ref, x1b_ref, e_ref, mprev_ref, xprev_ref):
    tb = x_ref.shape[0]
    ns = tb // SUBLANES
    npair = ns // 2
    f32 = jnp.float32
    bf16 = jnp.bfloat16

    @pl.when(pl.program_id(0) == 0)
    def _():
        _load_weights([(wo_hbm, wo_ref), (win_hbm, win_ref), (wpg_hbm, wpg_ref),
                       (wple_hbm, wple_ref), (wco_hbm, wco_ref), (wglu_hbm, wglu_ref),
                       (wso_hbm, wso_ref)], stage_ref, sem_ref)
        _expand_tables(xb_ref, yc_ref, sk_ref, a_ref, wb_ref, wc_ref, arow_ref, seg_ref)
        state_ref[...] = jnp.zeros_like(state_ref)
        v_ref[tb:tb + SUBLANES, :] = jnp.zeros((SUBLANES, CONV_WIDTH), f32)
        mprev_ref[...] = jnp.zeros_like(mprev_ref)
        xprev_ref[...] = jnp.zeros_like(xprev_ref)

    def cols(n):
        return slice(n * MXU_N, (n + 1) * MXU_N)

    def tile_dot(lhs_ref, w_ref, off, n):
        return jnp.dot(lhs_ref[...], w_ref[:, off + n * MXU_N:off + (n + 1) * MXU_N],
                       preferred_element_type=f32)


    for n in range(D_MODEL // MXU_N):
        o_ref[:, cols(n)] = tile_dot(mprev_ref, wo_ref, 0, n)

    for r in range(0, tb, ROW_TILE):
        rows = slice(r, r + ROW_TILE)
        x1 = xprev_ref[rows, :] + _rms(o_ref[rows, :], post_ref[...])
        x1_ref[rows, :] = x1
        x1b_ref[rows, :] = x1.astype(bf16)
    for r in range(0, tb, ROW_TILE):
        rows = slice(r, r + ROW_TILE)
        xs = x_ref[rows, :]
        h_ref[rows, :] = _rms(xs, pre_ref[...]).astype(bf16)
        xprev_ref[rows, :] = xs

    for n in range(SSM_WIDTH // MXU_N):
        u = tile_dot(h_ref, win_ref, O_U, n)
        for c in range(MXU_N // LANES):
            un_ref[n * (MXU_N // LANES) + c] = u[:, c * LANES:(c + 1) * LANES]
    for t in range(N_TILES):
        for par in range(2):
            for j in range(SUBLANES):
                for mh in range(npair // SUBLANES):
                    up_ref[par, t, pl.ds(SUBLANES * SUBLANES * mh + j, SUBLANES,
                                         stride=SUBLANES), :] = (
                        un_ref[t, pl.ds(j * ns + 2 * SUBLANES * mh + par, SUBLANES, stride=2), :])

    v_ref[0:SUBLANES, :] = v_ref[tb:tb + SUBLANES, :]
    for n in range(CONV_WIDTH // MXU_N):
        v_ref[SUBLANES:SUBLANES + tb, cols(n)] = (
            tile_dot(h_ref, win_ref, O_CC, n) * tile_dot(h_ref, win_ref, O_XC, n))

    srow = lax.broadcasted_iota(jnp.int32, (SUBLANES, LANES), 0)
    row0 = srow == 0

    def scan_tile(t):
        ub = jnp.concatenate([up_ref[0, t], up_ref[1, t]], axis=1).astype(bf16)
        for q in range(N_COLS):
            bu = jnp.dot(ub, wb_ref[t, :, cols(q)],
                         preferred_element_type=f32)
            re = slice(q * 2 * LANES, q * 2 * LANES + LANES)
            im = slice(q * 2 * LANES + LANES, (q + 1) * 2 * LANES)
            sl = slice(t * HALF + q * LANES, t * HALF + (q + 1) * LANES)
            ar = arow_ref[0, :, sl]
            ai = arow_ref[1, :, sl]
            sr = bu[0:SUBLANES, 0:LANES]
            si = bu[0:SUBLANES, LANES:2 * LANES]
            for m in range(1, npair):
                rows = slice(m * SUBLANES, (m + 1) * SUBLANES)
                sloc_ref[t, rows, re] = sr
                sloc_ref[t, rows, im] = si
                mr, mi = _cmul(ar, ai, sr, si)
                sr = mr + bu[rows, 0:LANES]
                si = mi + bu[rows, LANES:2 * LANES]
            gr = jnp.where(row0, state_ref[0, :, sl], pltpu.roll(sr, 1, 0))
            gi = jnp.where(row0, state_ref[1, :, sl], pltpu.roll(si, 1, 0))
            for k, shift in enumerate((1, 2, 4)):
                keep = srow >= shift
                fr, fi = _cmul(seg_ref[2 * k, :, sl], seg_ref[2 * k + 1, :, sl],
                               jnp.where(keep, pltpu.roll(gr, shift, 0), 0.0),
                               jnp.where(keep, pltpu.roll(gi, shift, 0), 0.0))
                gr, gi = gr + fr, gi + fi
            fr, fi = _cmul(seg_ref[6, :, sl], seg_ref[7, :, sl], gr, gi)
            nr, ni = fr + sr, fi + si
            state_ref[0, :, sl] = jnp.broadcast_to(nr[SUBLANES - 1:SUBLANES, :], nr.shape)
            state_ref[1, :, sl] = jnp.broadcast_to(ni[SUBLANES - 1:SUBLANES, :], ni.shape)
            fr, fi = gr, gi
            for m2 in range(npair // 2):
                parts_r, parts_i = [], []
                for m in (2 * m2, 2 * m2 + 1):
                    if m == 0:
                        parts_r.append(gr)
                        parts_i.append(gi)
                        continue
                    rows = slice(m * SUBLANES, (m + 1) * SUBLANES)
                    fr, fi = _cmul(ar, ai, fr, fi)
                    parts_r.append(sloc_ref[t, rows, re] + fr)
                    parts_i.append(sloc_ref[t, rows, im] + fi)
                rows2 = slice(2 * m2 * SUBLANES, (2 * m2 + 2) * SUBLANES)
                s_ref[t, rows2, re] = jnp.concatenate(parts_r, axis=0).astype(bf16)
                s_ref[t, rows2, im] = jnp.concatenate(parts_i, axis=0).astype(bf16)

    scan_tile(0)

    for n in range(CONV_WIDTH // MXU_N):
        bc = tile_dot(h_ref, win_ref, O_BC, n)
        gc = tile_dot(h_ref, win_ref, O_GC, n)
        conv = (convw_ref[0:1, cols(n)] * v_ref[SUBLANES - 2:SUBLANES - 2 + tb, cols(n)]
                + convw_ref[1:2, cols(n)] * v_ref[SUBLANES - 1:SUBLANES - 1 + tb, cols(n)]
                + convw_ref[2:3, cols(n)] * v_ref[SUBLANES:SUBLANES + tb, cols(n)])
        z_ref[:, cols(n)] = (bc * conv * (gc * _sigmoid(gc))).astype(bf16)
        if n == 1:
            scan_tile(1)

    pb = p_ref[...].astype(bf16)
    for n in range(D_MODEL // MXU_N):
        eg = tile_dot(x1b_ref, wpg_ref, 0, n)
        ei = jnp.dot(pb, wple_ref[:, cols(n)], preferred_element_type=f32)
        e_ref[:, cols(n)] = ei * _sigmoid(eg)
        if n == 1:
            scan_tile(2)

    for n in range(SSM_WIDTH // MXU_N):
        gs = tile_dot(h_ref, win_ref, O_GS, n)
        gs_ref[:, cols(n)] = gs * _sigmoid(gs)
    scan_tile(3)

    for r in range(0, tb, ROW_TILE):
        rows = slice(r, r + ROW_TILE)
        out_ref[rows, :] = x1_ref[rows, :] + _rms(e_ref[rows, :], plen_ref[...])

    def gated_b(n):
        rc = tile_dot(h_ref, win_ref, O_RC, n)
        yb = tile_dot(z_ref, wco_ref, 0, n)
        mg_ref[:, cols(n)] = _sigmoid(rc) * yb

    gated_b(0)
    gated_b(1)

    for t in range(N_TILES):
        lhs = jnp.concatenate(
            [s_ref[t], up_ref[0, t].astype(bf16), up_ref[1, t].astype(bf16)], axis=1)
        yy = jnp.dot(lhs, wc_ref[t], preferred_element_type=f32)
        yp_ref[0, t] = yy[:, 0:LANES]
        yp_ref[1, t] = yy[:, LANES:2 * LANES]
    gated_b(2)
    gated_b(3)
    for t in range(N_TILES):
        for par in range(2):
            for j in range(SUBLANES):
                for mh in range(npair // SUBLANES):
                    y_ref[t, pl.ds(j * ns + 2 * SUBLANES * mh + par, SUBLANES, stride=2), :] = (
                        _gelu_tanh(yp_ref[par, t, pl.ds(SUBLANES * SUBLANES * mh + j, SUBLANES,
                                                        stride=SUBLANES), :]))
    y = jnp.concatenate([y_ref[t] for t in range(N_TILES)], axis=1)
    yb16 = y.astype(bf16)
    for n in range(SSM_WIDTH // MXU_N):
        glu = jnp.dot(yb16, wglu_ref[:, cols(n)], preferred_element_type=f32)
        y2_ref[:, cols(n)] = (y[:, cols(n)] * _sigmoid(glu + bglu_ref[:, cols(n)])
                              * gs_ref[:, cols(n)]).astype(bf16)

    n_out = D_MODEL // MXU_N
    rs = [tile_dot(h_ref, win_ref, O_RS, n) for n in range(2)]
    for n in range(n_out):
        ya = tile_dot(y2_ref, wso_ref, 0, n)
        mprev_ref[:, cols(n)] = (_sigmoid(rs[n]) * ya + mg_ref[:, cols(n)]).astype(bf16)
        if n + 2 < n_out:
            rs.append(tile_dot(h_ref, win_ref, O_RS, n + 2))


def _ssm_params(a_re, a_im, log_step, b_re, b_im, c_re, c_im, d_skip):
    f32 = jnp.float32
    G, P, H, T, GT = SSM_GROUPS, SSM_STATE, SSM_GROUP, N_TILES, GROUPS_PER_TILE
    dt = jnp.exp(log_step.astype(f32))[:, None]
    mag = jnp.exp(a_re * dt)
    abar_r = mag * jnp.cos(a_im * dt)
    abar_i = mag * jnp.sin(a_im * dt)
    den = a_re * a_re + a_im * a_im
    nr = abar_r - 1.0
    fr = (nr * a_re + abar_i * a_im) / den
    fi = (abar_i * a_re - nr * a_im) / den
    bbar_r = fr[..., None] * b_re - fi[..., None] * b_im
    bbar_i = fr[..., None] * b_im + fi[..., None] * b_re
    ab_r, ab_i = _cmul(abar_r[..., None], abar_i[..., None], bbar_r, bbar_i)
    ca_r, ca_i = _cmul(abar_r[:, None, :], abar_i[:, None, :], c_re, c_im)
    caa_r, caa_i = _cmul(abar_r[:, None, :], abar_i[:, None, :], ca_r, ca_i)

    def re_prod(xr, xi, yr, yi):
        yr_t = jnp.swapaxes(yr, 1, 2)[:, None]
        yi_t = jnp.swapaxes(yi, 1, 2)[:, None]
        return jnp.sum(xr[:, :, None, :] * yr_t - xi[:, :, None, :] * yi_t, axis=-1)

    cb_d = (re_prod(c_re, c_im, bbar_r, bbar_i)
            + jnp.eye(H, dtype=f32) * d_skip.reshape(G, H, 1))
    cab = re_prod(ca_r, ca_i, bbar_r, bbar_i)

    xb = jnp.stack([jnp.stack([ab_r, ab_i]), jnp.stack([bbar_r, bbar_i])])
    xb = xb.reshape(2, 2, T, GT, P, H).transpose(2, 0, 3, 5, 1, 4).reshape(T, 2 * LANES, 2 * P)
    yc = jnp.stack([jnp.stack([ca_r, -ca_i]), jnp.stack([caa_r, -caa_i])])
    yc = yc.reshape(2, 2, T, N_COLS, 2, H, P).transpose(2, 3, 1, 4, 6, 0, 5)
    yc = yc.reshape(T, TILE_STATE, 2 * H)
    zero = jnp.zeros_like(cab)
    sk = jnp.stack([jnp.stack([cb_d, cab]), jnp.stack([zero, cb_d])])
    sk = sk.reshape(2, 2, T, GT, H, H).transpose(2, 0, 3, 5, 1, 4).reshape(T, 2 * LANES, 2 * H)
    a = jnp.stack([abar_r.reshape(N_STATE), abar_i.reshape(N_STATE)])
    bf16 = jnp.bfloat16
    return xb.astype(bf16), yc.astype(bf16), sk.astype(bf16), a


def _resident(shape):
    return pl.BlockSpec(shape, lambda i: (0,) * len(shape), pipeline_mode=pl.Buffered(1))


@jax.jit
def kernel(x, p, pre_norm, w_in, a_re, a_im, log_step, b_re, b_im, c_re, c_im, d_skip,
           w_glu, b_glu, w_ssm_out, conv_w, w_conv_out, w_o, post_norm,
           w_ple, w_ple_gate, ple_norm):
    bsz, seqlen, _ = x.shape
    depth = p.shape[0]
    assert bsz == 1 and depth == 1
    bf16 = jnp.bfloat16
    tb = ROW_BLOCK

    xb, yc, sk, a_bar = _ssm_params(
        a_re[0], a_im[0], log_step[0], b_re[0], b_im[0], c_re[0], c_im[0], d_skip[0])
    nb = pl.cdiv(seqlen, tb)
    cur = lambda i: (jnp.minimum(i, nb - 1), 0)
    prev = lambda i: (jnp.maximum(i - 1, 0), 0)
    hbm = pl.BlockSpec(memory_space=pl.ANY)
    operands = [
        (x[0], pl.BlockSpec((tb, D_MODEL), cur)),
        (p[0, 0], pl.BlockSpec((tb, PLE_DIM), prev)),
        (pre_norm, _resident((1, D_MODEL))),
        (w_in.reshape(D_MODEL, IN_WIDTH), hbm),
        (xb, _resident(xb.shape)),
        (yc, _resident(yc.shape)),
        (sk, _resident(sk.shape)),
        (a_bar, _resident(a_bar.shape)),
        (w_glu.reshape(SSM_WIDTH, SSM_WIDTH), hbm),
        (b_glu, _resident((1, SSM_WIDTH))),
        (w_ssm_out.reshape(SSM_WIDTH, D_MODEL), hbm),
        (conv_w[0], _resident((CONV_K, CONV_WIDTH))),
        (w_conv_out.reshape(CONV_WIDTH, D_MODEL), hbm),
        (w_o.reshape(D_MODEL, D_MODEL), hbm),
        (post_norm, _resident((1, D_MODEL))),
        (w_ple.reshape(PLE_DIM, D_MODEL), hbm),
        (w_ple_gate.reshape(D_MODEL, D_MODEL), hbm),
        (ple_norm, _resident((1, D_MODEL))),
    ]
    out = pl.pallas_call(
        _block_kernel,
        grid=(nb + 1,),
        in_specs=[spec for _, spec in operands],
        out_specs=pl.BlockSpec((tb, D_MODEL), prev),
        out_shape=jax.ShapeDtypeStruct((seqlen, D_MODEL), jnp.float32),
        scratch_shapes=[
            pltpu.VMEM((D_MODEL, IN_WIDTH), bf16),
            pltpu.VMEM((SSM_WIDTH, SSM_WIDTH), bf16),
            pltpu.VMEM((SSM_WIDTH, D_MODEL), bf16),
            pltpu.VMEM((CONV_WIDTH, D_MODEL), bf16),
            pltpu.VMEM((D_MODEL, D_MODEL), bf16),
            pltpu.VMEM((PLE_DIM, D_MODEL), bf16),
            pltpu.VMEM((D_MODEL, D_MODEL), bf16),
            pltpu.VMEM((2, STAGE_ROWS, IN_WIDTH), jnp.float32),
            pltpu.SemaphoreType.DMA((2,)),
            pltpu.VMEM((N_TILES, 2 * LANES, TILE_STATE), jnp.bfloat16),
            pltpu.VMEM((N_TILES, TILE_STATE + 2 * LANES, 2 * LANES), jnp.bfloat16),
            pltpu.VMEM((2, SUBLANES, N_STATE), jnp.float32),
            pltpu.VMEM((8, SUBLANES, N_STATE), jnp.float32),
            pltpu.VMEM((2, SUBLANES, N_STATE), jnp.float32),
            pltpu.VMEM((tb, D_MODEL), jnp.bfloat16),
            pltpu.VMEM((N_TILES, tb, LANES), jnp.float32),
            pltpu.VMEM((2, N_TILES, tb // 2, LANES), jnp.float32),
            pltpu.VMEM((tb + SUBLANES, CONV_WIDTH), jnp.float32),
            pltpu.VMEM((N_TILES, tb // 2, TILE_STATE), jnp.float32),
            pltpu.VMEM((N_TILES, tb // 2, TILE_STATE), jnp.bfloat16),
            pltpu.VMEM((tb, CONV_WIDTH), jnp.bfloat16),
            pltpu.VMEM((tb, SSM_WIDTH), jnp.float32),
            pltpu.VMEM((tb, D_MODEL), jnp.float32),
            pltpu.VMEM((2, N_TILES, tb // 2, LANES), jnp.float32),
            pltpu.VMEM((N_TILES, tb, LANES), jnp.float32),
            pltpu.VMEM((tb, SSM_WIDTH), jnp.bfloat16),
            pltpu.VMEM((tb, D_MODEL), jnp.float32),
            pltpu.VMEM((tb, D_MODEL), jnp.float32),
            pltpu.VMEM((tb, D_MODEL), jnp.bfloat16),
            pltpu.VMEM((tb, D_MODEL), jnp.float32),
            pltpu.VMEM((tb, D_MODEL), jnp.bfloat16),
            pltpu.VMEM((tb, D_MODEL), jnp.float32),
        ],
        compiler_params=pltpu.CompilerParams(
            dimension_semantics=("arbitrary",),
            vmem_limit_bytes=VMEM_LIMIT_BYTES),
        name="hybrid_s5_shortconv_block",
    )(*[a for a, _ in operands])
    return out[None]
```

```python
import math

import jax
import jax.numpy as jnp
from jax import lax
from jax.experimental import pallas as pl
from jax.experimental.pallas import tpu as pltpu

D_MODEL = 1024
PLE_DIM = 256
SSM_WIDTH = 512
SSM_GROUP = 16
SSM_GROUPS = 32
SSM_STATE = 64
CONV_WIDTH = 1024
CONV_K = 3
RMS_EPS = 1e-6
IN_WIDTH = 7168

LANES = 128
MXU_N = 256
SUBLANES = 8
ROW_TILE = 16
GROUPS_PER_TILE = LANES // SSM_GROUP
N_TILES = SSM_WIDTH // LANES
HALF = GROUPS_PER_TILE * SSM_STATE
TILE_STATE = 2 * HALF
N_COLS = HALF // LANES
N_STATE = SSM_GROUPS * SSM_STATE
ROW_BLOCK = 384
SEG_PAIRS = ROW_BLOCK // SUBLANES // 2
STAGE_ROWS = 64
VMEM_LIMIT_BYTES = 58 * 1024 * 1024

O_U, O_GS, O_XC, O_BC, O_CC, O_GC, O_RS, O_RC = 0, 512, 1024, 2048, 3072, 4096, 5120, 6144


def _rms(v, gain):
    return v * lax.rsqrt(jnp.mean(v * v, axis=-1, keepdims=True) + RMS_EPS) * gain


def _sigmoid(v):
    return 1.0 / (1.0 + jnp.exp(-v))


def _gelu_tanh(v):
    c = math.sqrt(2.0 / math.pi)
    return v * (0.5 * (1.0 + jnp.tanh(c * (v + 0.044715 * (v * v * v)))))


def _cmul(ar, ai, xr, xi):
    return ar * xr - ai * xi, ar * xi + ai * xr


def _expand_tables(xb_ref, yc_ref, sk_ref, a_ref, wb_ref, wc_ref, arow_ref, seg_ref):
    f32 = jnp.float32
    bf16 = jnp.bfloat16
    gmask = GROUPS_PER_TILE - 1

    def iota(shape, dim):
        return lax.broadcasted_iota(jnp.int32, shape, dim)

    def onehot(cond):
        return jnp.where(cond, 1.0, 0.0).astype(bf16)

    r = iota((LANES, TILE_STATE), 0)
    c = iota((LANES, TILE_STATE), 1)
    rep_b = onehot(((c & (SSM_STATE - 1)) == (r & (SSM_STATE - 1))) & (((c >> 7) & 1) == (r >> 6)))
    r = iota((2 * LANES, TILE_STATE), 0)
    c = iota((2 * LANES, TILE_STATE), 1)
    own_b = ((r >> 4) & gmask) == (((c >> 8) << 1) | ((c >> 6) & 1))
    r = iota((2 * SSM_GROUP, 2 * LANES), 0)
    c = iota((2 * SSM_GROUP, 2 * LANES), 1)
    rep_c = onehot(((c >> 7) == (r >> 4)) & ((c & (SSM_GROUP - 1)) == (r & (SSM_GROUP - 1))))
    r = iota((TILE_STATE, 2 * LANES), 0)
    c = iota((TILE_STATE, 2 * LANES), 1)
    own_c = (((r >> 8) << 1) | ((r >> 6) & 1)) == ((c >> 4) & gmask)
    r = iota((2 * LANES, 2 * LANES), 0)
    c = iota((2 * LANES, 2 * LANES), 1)
    own_s = ((r >> 4) & gmask) == ((c >> 4) & gmask)
    for t in range(N_TILES):
        wide = jnp.dot(xb_ref[t].astype(bf16), rep_b, preferred_element_type=f32)
        wb_ref[t] = jnp.where(own_b, wide, 0.0).astype(bf16)
        wide = jnp.dot(yc_ref[t].astype(bf16), rep_c, preferred_element_type=f32)
        wc_ref[t, 0:TILE_STATE, :] = jnp.where(own_c, wide, 0.0).astype(bf16)
        wide = jnp.dot(sk_ref[t].astype(bf16), rep_c, preferred_element_type=f32)
        wc_ref[t, TILE_STATE:TILE_STATE + 2 * LANES, :] = jnp.where(own_s, wide, 0.0).astype(bf16)

    sq = [_cmul(a_ref[0:1, :], a_ref[1:2, :], a_ref[0:1, :], a_ref[1:2, :])]
    while len(sq) < SEG_PAIRS.bit_length():
        sq.append(_cmul(*sq[-1], *sq[-1]))
    a_seg = None
    for k in range(SEG_PAIRS.bit_length()):
        if (SEG_PAIRS >> k) & 1:
            a_seg = sq[k] if a_seg is None else _cmul(*a_seg, *sq[k])
    shape = (SUBLANES, N_STATE)
    arow_ref[0] = jnp.broadcast_to(sq[0][0], shape)
    arow_ref[1] = jnp.broadcast_to(sq[0][1], shape)
    row = iota(shape, 0)
    seg_ref[6] = jnp.broadcast_to(a_seg[0], shape)
    seg_ref[7] = jnp.broadcast_to(a_seg[1], shape)
    for k, shift in enumerate((1, 2, 4)):
        seg_ref[2 * k] = jnp.where(row >= shift, jnp.broadcast_to(a_seg[0], shape), 0.0)
        seg_ref[2 * k + 1] = jnp.where(row >= shift, jnp.broadcast_to(a_seg[1], shape), 0.0)
        a_seg = _cmul(*a_seg, *a_seg)


def _load_weights(pairs, stage_ref, sem_ref):
    chunk_rows, stage_cols = stage_ref.shape[1:]
    rounds = []
    for src_ref, dst_ref in pairs:
        rows, width = src_ref.shape
        assert rows % chunk_rows == 0 and width <= stage_cols
        per_round = stage_cols // width
        starts = list(range(0, rows, chunk_rows))
        for i in range(0, len(starts), per_round):
            rounds.append([(src_ref, dst_ref, r0, width, j * width)
                           for j, r0 in enumerate(starts[i:i + per_round])])

    def copies(k):
        return [pltpu.make_async_copy(src_ref.at[pl.ds(r0, chunk_rows), :],
                                      stage_ref.at[k % 2, :, pl.ds(c0, width)],
                                      sem_ref.at[k % 2])
                for src_ref, _, r0, width, c0 in rounds[k]]

    for cp in copies(0):
        cp.start()
    for k, chunks in enumerate(rounds):
        if k + 1 < len(rounds):
            for cp in copies(k + 1):
                cp.start()
        for cp in copies(k):
            cp.wait()
        for _, dst_ref, r0, width, c0 in chunks:
            dst_ref[r0:r0 + chunk_rows, :] = stage_ref[k % 2, :, c0:c0 + width].astype(dst_ref.dtype)


def _block_kernel(*refs):
    last = pl.num_programs(0) - 1

    @pl.when(pl.program_id(0) == 0)
    def _():
        _first_step_setup(*refs)

    @pl.when(pl.program_id(0) < last)
    def _():
        _run_step(True, *refs)

    @pl.when(pl.program_id(0) == last)
    def _():
        _run_step(False, *refs)


def _first_step_setup(x_ref, p_ref, pre_ref, win_hbm, xb_ref, yc_ref, sk_ref, a_ref,
                      wglu_hbm, bglu_ref, wso_hbm, convw_ref, wco_hbm, wo_hbm, post_ref,
                      wple_hbm, wpg_hbm, plen_ref, out_ref,
                      win_ref, wglu_ref, wso_ref, wco_ref, wo_ref, wple_ref, wpg_ref, stage_ref,
                      sem_ref, wb_ref, wc_ref, arow_ref, seg_ref, state_ref, h_ref, un_ref, up_ref,
                      v_ref, sloc_ref, s_ref, z_ref, gs_ref, mg_ref,
                      yp_ref, y_ref, y2_ref, o_ref, x1_ref, x1b_ref, e_ref, mprev_ref, xprev_ref):
    tb = x_ref.shape[0]
    _load_weights([(wo_hbm, wo_ref), (win_hbm, win_ref), (wpg_hbm, wpg_ref),
                   (wple_hbm, wple_ref), (wco_hbm, wco_ref), (wglu_hbm, wglu_ref),
                   (wso_hbm, wso_ref)], stage_ref, sem_ref)
    _expand_tables(xb_ref, yc_ref, sk_ref, a_ref, wb_ref, wc_ref, arow_ref, seg_ref)
    state_ref[...] = jnp.zeros_like(state_ref)
    v_ref[tb:tb + SUBLANES, :] = jnp.zeros((SUBLANES, CONV_WIDTH), jnp.float32)
    mprev_ref[...] = jnp.zeros_like(mprev_ref)
    xprev_ref[...] = jnp.zeros_like(xprev_ref)


def _run_step(mixers, x_ref, p_ref, pre_ref, win_hbm, xb_ref, yc_ref, sk_ref, a_ref,
              wglu_hbm, bglu_ref, wso_hbm, convw_ref, wco_hbm, wo_hbm, post_ref,
              wple_hbm, wpg_hbm, plen_ref, out_ref,
              win_ref, wglu_ref, wso_ref, wco_ref, wo_ref, wple_ref, wpg_ref, stage_ref, sem_ref,
              wb_ref, wc_ref, arow_ref, seg_ref, state_ref, h_ref, un_ref, up_ref, v_ref,
              sloc_ref, s_ref, z_ref, gs_ref, mg_ref,
              yp_ref, y_ref, y2_ref, o_ref, x1_ref, x1b_ref, e_ref, mprev_ref, xprev_ref):
    tb = x_ref.shape[0]
    ns = tb // SUBLANES
    npair = ns // 2
    f32 = jnp.float32
    bf16 = jnp.bfloat16

    def cols(n):
        return slice(n * MXU_N, (n + 1) * MXU_N)

    def tile_dot(lhs_ref, w_ref, off, n):
        return jnp.dot(lhs_ref[...], w_ref[:, off + n * MXU_N:off + (n + 1) * MXU_N],
                       preferred_element_type=f32)

    def finish_project():
        for n in range(D_MODEL // MXU_N):
            o_ref[:, cols(n)] = tile_dot(mprev_ref, wo_ref, 0, n)

    def finish_residual():
        for r in range(0, tb, ROW_TILE):
            rows = slice(r, r + ROW_TILE)
            x1 = xprev_ref[rows, :] + _rms(o_ref[rows, :], post_ref[...])
            x1_ref[rows, :] = x1
            x1b_ref[rows, :] = x1.astype(bf16)

    def finish_embed(n):
        eg = tile_dot(x1b_ref, wpg_ref, 0, n)
        ei = jnp.dot(p_ref[...].astype(bf16), wple_ref[:, cols(n)], preferred_element_type=f32)
        e_ref[:, cols(n)] = ei * _sigmoid(eg)

    def finish_store():
        for r in range(0, tb, ROW_TILE):
            rows = slice(r, r + ROW_TILE)
            out_ref[rows, :] = x1_ref[rows, :] + _rms(e_ref[rows, :], plen_ref[...])

    if not mixers:
        finish_project()
        finish_residual()
        for n in range(D_MODEL // MXU_N):
            finish_embed(n)
        finish_store()
        return

    finish_project()
    finish_residual()
    for r in range(0, tb, ROW_TILE):
        rows = slice(r, r + ROW_TILE)
        xs = x_ref[rows, :]
        h_ref[rows, :] = _rms(xs, pre_ref[...]).astype(bf16)
        xprev_ref[rows, :] = xs

    for n in range(SSM_WIDTH // MXU_N):
        u = tile_dot(h_ref, win_ref, O_U, n)
        for c in range(MXU_N // LANES):
            un_ref[n * (MXU_N // LANES) + c] = u[:, c * LANES:(c + 1) * LANES]
    for t in range(N_TILES):
        for par in range(2):
            for j in range(SUBLANES):
                for mh in range(npair // SUBLANES):
                    up_ref[par, t, pl.ds(SUBLANES * SUBLANES * mh + j, SUBLANES,
                                         stride=SUBLANES), :] = (
                        un_ref[t, pl.ds(j * ns + 2 * SUBLANES * mh + par, SUBLANES, stride=2), :])

    v_ref[0:SUBLANES, :] = v_ref[tb:tb + SUBLANES, :]
    for n in range(CONV_WIDTH // MXU_N):
        v_ref[SUBLANES:SUBLANES + tb, cols(n)] = (
            tile_dot(h_ref, win_ref, O_CC, n) * tile_dot(h_ref, win_ref, O_XC, n))

    srow = lax.broadcasted_iota(jnp.int32, (SUBLANES, LANES), 0)
    row0 = srow == 0

    def scan_tile(t):
        ub = jnp.concatenate([up_ref[0, t], up_ref[1, t]], axis=1).astype(bf16)
        for q in range(N_COLS):
            bu = jnp.dot(ub, wb_ref[t, :, cols(q)],
                         preferred_element_type=f32)
            re = slice(q * 2 * LANES, q * 2 * LANES + LANES)
            im = slice(q * 2 * LANES + LANES, (q + 1) * 2 * LANES)
            sl = slice(t * HALF + q * LANES, t * HALF + (q + 1) * LANES)
            ar = arow_ref[0, :, sl]
            ai = arow_ref[1, :, sl]
            sr = bu[0:SUBLANES, 0:LANES]
            si = bu[0:SUBLANES, LANES:2 * LANES]
            for m in range(1, npair):
                rows = slice(m * SUBLANES, (m + 1) * SUBLANES)
                sloc_ref[t, rows, re] = sr
                sloc_ref[t, rows, im] = si
                mr, mi = _cmul(ar, ai, sr, si)
                sr = mr + bu[rows, 0:LANES]
                si = mi + bu[rows, LANES:2 * LANES]
            gr = jnp.where(row0, state_ref[0, :, sl], pltpu.roll(sr, 1, 0))
            gi = jnp.where(row0, state_ref[1, :, sl], pltpu.roll(si, 1, 0))
            for k, shift in enumerate((1, 2, 4)):
                keep = srow >= shift
                fr, fi = _cmul(seg_ref[2 * k, :, sl], seg_ref[2 * k + 1, :, sl],
                               jnp.where(keep, pltpu.roll(gr, shift, 0), 0.0),
                               jnp.where(keep, pltpu.roll(gi, shift, 0), 0.0))
                gr, gi = gr + fr, gi + fi
            fr, fi = _cmul(seg_ref[6, :, sl], seg_ref[7, :, sl], gr, gi)
            nr, ni = fr + sr, fi + si
            state_ref[0, :, sl] = jnp.broadcast_to(nr[SUBLANES - 1:SUBLANES, :], nr.shape)
            state_ref[1, :, sl] = jnp.broadcast_to(ni[SUBLANES - 1:SUBLANES, :], ni.shape)
            fr, fi = gr, gi
            for m2 in range(npair // 2):
                parts_r, parts_i = [], []
                for m in (2 * m2, 2 * m2 + 1):
                    if m == 0:
                        parts_r.append(gr)
                        parts_i.append(gi)
                        continue
                    rows = slice(m * SUBLANES, (m + 1) * SUBLANES)
                    fr, fi = _cmul(ar, ai, fr, fi)
                    parts_r.append(sloc_ref[t, rows, re] + fr)
                    parts_i.append(sloc_ref[t, rows, im] + fi)
                rows2 = slice(2 * m2 * SUBLANES, (2 * m2 + 2) * SUBLANES)
                s_ref[t, rows2, re] = jnp.concatenate(parts_r, axis=0).astype(bf16)
                s_ref[t, rows2, im] = jnp.concatenate(parts_i, axis=0).astype(bf16)

    scan_tile(0)

    for n in range(CONV_WIDTH // MXU_N):
        bc = tile_dot(h_ref, win_ref, O_BC, n)
        gc = tile_dot(h_ref, win_ref, O_GC, n)
        conv = (convw_ref[0:1, cols(n)] * v_ref[SUBLANES - 2:SUBLANES - 2 + tb, cols(n)]
                + convw_ref[1:2, cols(n)] * v_ref[SUBLANES - 1:SUBLANES - 1 + tb, cols(n)]
                + convw_ref[2:3, cols(n)] * v_ref[SUBLANES:SUBLANES + tb, cols(n)])
        z_ref[:, cols(n)] = (bc * conv * (gc * _sigmoid(gc))).astype(bf16)
        if n == 1:
            scan_tile(1)

    for n in range(D_MODEL // MXU_N):
        finish_embed(n)
        if n == 1:
            scan_tile(2)

    for n in range(SSM_WIDTH // MXU_N):
        gs = tile_dot(h_ref, win_ref, O_GS, n)
        gs_ref[:, cols(n)] = gs * _sigmoid(gs)
    scan_tile(3)

    finish_store()

    def gated_b(n):
        rc = tile_dot(h_ref, win_ref, O_RC, n)
        yb = tile_dot(z_ref, wco_ref, 0, n)
        mg_ref[:, cols(n)] = _sigmoid(rc) * yb

    gated_b(0)
    gated_b(1)

    for t in range(N_TILES):
        lhs = jnp.concatenate(
            [s_ref[t], up_ref[0, t].astype(bf16), up_ref[1, t].astype(bf16)], axis=1)
        yy = jnp.dot(lhs, wc_ref[t], preferred_element_type=f32)
        yp_ref[0, t] = yy[:, 0:LANES]
        yp_ref[1, t] = yy[:, LANES:2 * LANES]
    gated_b(2)
    gated_b(3)
    for t in range(N_TILES):
        for par in range(2):
            for j in range(SUBLANES):
                for mh in range(npair // SUBLANES):
                    y_ref[t, pl.ds(j * ns + 2 * SUBLANES * mh + par, SUBLANES, stride=2), :] = (
                        _gelu_tanh(yp_ref[par, t, pl.ds(SUBLANES * SUBLANES * mh + j, SUBLANES,
                                                        stride=SUBLANES), :]))
    y = jnp.concatenate([y_ref[t] for t in range(N_TILES)], axis=1)
    yb16 = y.astype(bf16)
    for n in range(SSM_WIDTH // MXU_N):
        glu = jnp.dot(yb16, wglu_ref[:, cols(n)], preferred_element_type=f32)
        y2_ref[:, cols(n)] = (y[:, cols(n)] * _sigmoid(glu + bglu_ref[:, cols(n)])
                              * gs_ref[:, cols(n)]).astype(bf16)

    n_out = D_MODEL // MXU_N
    rs = [tile_dot(h_ref, win_ref, O_RS, n) for n in range(2)]
    for n in range(n_out):
        ya = tile_dot(y2_ref, wso_ref, 0, n)
        mprev_ref[:, cols(n)] = (_sigmoid(rs[n]) * ya + mg_ref[:, cols(n)]).astype(bf16)
        if n + 2 < n_out:
            rs.append(tile_dot(h_ref, win_ref, O_RS, n + 2))


def _ssm_params(a_re, a_im, log_step, b_re, b_im, c_re, c_im, d_skip):
    f32 = jnp.float32
    G, P, H, T, GT = SSM_GROUPS, SSM_STATE, SSM_GROUP, N_TILES, GROUPS_PER_TILE
    dt = jnp.exp(log_step.astype(f32))[:, None]
    mag = jnp.exp(a_re * dt)
    abar_r = mag * jnp.cos(a_im * dt)
    abar_i = mag * jnp.sin(a_im * dt)
    den = a_re * a_re + a_im * a_im
    nr = abar_r - 1.0
    fr = (nr * a_re + abar_i * a_im) / den
    fi = (abar_i * a_re - nr * a_im) / den
    bbar_r = fr[..., None] * b_re - fi[..., None] * b_im
    bbar_i = fr[..., None] * b_im + fi[..., None] * b_re
    ab_r, ab_i = _cmul(abar_r[..., None], abar_i[..., None], bbar_r, bbar_i)
    ca_r, ca_i = _cmul(abar_r[:, None, :], abar_i[:, None, :], c_re, c_im)
    caa_r, caa_i = _cmul(abar_r[:, None, :], abar_i[:, None, :], ca_r, ca_i)

    def re_prod(xr, xi, yr, yi):
        yr_t = jnp.swapaxes(yr, 1, 2)[:, None]
        yi_t = jnp.swapaxes(yi, 1, 2)[:, None]
        return jnp.sum(xr[:, :, None, :] * yr_t - xi[:, :, None, :] * yi_t, axis=-1)

    cb_d = (re_prod(c_re, c_im, bbar_r, bbar_i)
            + jnp.eye(H, dtype=f32) * d_skip.reshape(G, H, 1))
    cab = re_prod(ca_r, ca_i, bbar_r, bbar_i)

    xb = jnp.stack([jnp.stack([ab_r, ab_i]), jnp.stack([bbar_r, bbar_i])])
    xb = xb.reshape(2, 2, T, GT, P, H).transpose(2, 0, 3, 5, 1, 4).reshape(T, 2 * LANES, 2 * P)
    yc = jnp.stack([jnp.stack([ca_r, -ca_i]), jnp.stack([caa_r, -caa_i])])
    yc = yc.reshape(2, 2, T, N_COLS, 2, H, P).transpose(2, 3, 1, 4, 6, 0, 5)
    yc = yc.reshape(T, TILE_STATE, 2 * H)
    zero = jnp.zeros_like(cab)
    sk = jnp.stack([jnp.stack([cb_d, cab]), jnp.stack([zero, cb_d])])
    sk = sk.reshape(2, 2, T, GT, H, H).transpose(2, 0, 3, 5, 1, 4).reshape(T, 2 * LANES, 2 * H)
    a = jnp.stack([abar_r.reshape(N_STATE), abar_i.reshape(N_STATE)])
    bf16 = jnp.bfloat16
    return xb.astype(bf16), yc.astype(bf16), sk.astype(bf16), a


def _resident(shape):
    return pl.BlockSpec(shape, lambda i: (0,) * len(shape), pipeline_mode=pl.Buffered(1))


@jax.jit
def kernel(x, p, pre_norm, w_in, a_re, a_im, log_step, b_re, b_im, c_re, c_im, d_skip,
           w_glu, b_glu, w_ssm_out, conv_w, w_conv_out, w_o, post_norm,
           w_ple, w_ple_gate, ple_norm):
    bsz, seqlen, _ = x.shape
    depth = p.shape[0]
    assert bsz == 1 and depth == 1
    bf16 = jnp.bfloat16
    tb = ROW_BLOCK

    xb, yc, sk, a_bar = _ssm_params(
        a_re[0], a_im[0], log_step[0], b_re[0], b_im[0], c_re[0], c_im[0], d_skip[0])
    nb = pl.cdiv(seqlen, tb)
    cur = lambda i: (jnp.minimum(i, nb - 1), 0)
    prev = lambda i: (jnp.maximum(i - 1, 0), 0)
    hbm = pl.BlockSpec(memory_space=pl.ANY)
    operands = [
        (x[0], pl.BlockSpec((tb, D_MODEL), cur)),
        (p[0, 0], pl.BlockSpec((tb, PLE_DIM), prev)),
        (pre_norm, _resident((1, D_MODEL))),
        (w_in.reshape(D_MODEL, IN_WIDTH), hbm),
        (xb, _resident(xb.shape)),
        (yc, _resident(yc.shape)),
        (sk, _resident(sk.shape)),
        (a_bar, _resident(a_bar.shape)),
        (w_glu.reshape(SSM_WIDTH, SSM_WIDTH), hbm),
        (b_glu, _resident((1, SSM_WIDTH))),
        (w_ssm_out.reshape(SSM_WIDTH, D_MODEL), hbm),
        (conv_w[0], _resident((CONV_K, CONV_WIDTH))),
        (w_conv_out.reshape(CONV_WIDTH, D_MODEL), hbm),
        (w_o.reshape(D_MODEL, D_MODEL), hbm),
        (post_norm, _resident((1, D_MODEL))),
        (w_ple.reshape(PLE_DIM, D_MODEL), hbm),
        (w_ple_gate.reshape(D_MODEL, D_MODEL), hbm),
        (ple_norm, _resident((1, D_MODEL))),
    ]
    out = pl.pallas_call(
        _block_kernel,
        grid=(nb + 1,),
        in_specs=[spec for _, spec in operands],
        out_specs=pl.BlockSpec((tb, D_MODEL), prev),
        out_shape=jax.ShapeDtypeStruct((seqlen, D_MODEL), jnp.float32),
        scratch_shapes=[
            pltpu.VMEM((D_MODEL, IN_WIDTH), bf16),
            pltpu.VMEM((SSM_WIDTH, SSM_WIDTH), bf16),
            pltpu.VMEM((SSM_WIDTH, D_MODEL), bf16),
            pltpu.VMEM((CONV_WIDTH, D_MODEL), bf16),
            pltpu.VMEM((D_MODEL, D_MODEL), bf16),
            pltpu.VMEM((PLE_DIM, D_MODEL), bf16),
            pltpu.VMEM((D_MODEL, D_MODEL), bf16),
            pltpu.VMEM((2, STAGE_ROWS, IN_WIDTH), jnp.float32),
            pltpu.SemaphoreType.DMA((2,)),
            pltpu.VMEM((N_TILES, 2 * LANES, TILE_STATE), jnp.bfloat16),
            pltpu.VMEM((N_TILES, TILE_STATE + 2 * LANES, 2 * LANES), jnp.bfloat16),
            pltpu.VMEM((2, SUBLANES, N_STATE), jnp.float32),
            pltpu.VMEM((8, SUBLANES, N_STATE), jnp.float32),
            pltpu.VMEM((2, SUBLANES, N_STATE), jnp.float32),
            pltpu.VMEM((tb, D_MODEL), jnp.bfloat16),
            pltpu.VMEM((N_TILES, tb, LANES), jnp.float32),
            pltpu.VMEM((2, N_TILES, tb // 2, LANES), jnp.float32),
            pltpu.VMEM((tb + SUBLANES, CONV_WIDTH), jnp.float32),
            pltpu.VMEM((N_TILES, tb // 2, TILE_STATE), jnp.float32),
            pltpu.VMEM((N_TILES, tb // 2, TILE_STATE), jnp.bfloat16),
            pltpu.VMEM((tb, CONV_WIDTH), jnp.bfloat16),
            pltpu.VMEM((tb, SSM_WIDTH), jnp.float32),
            pltpu.VMEM((tb, D_MODEL), jnp.float32),
            pltpu.VMEM((2, N_TILES, tb // 2, LANES), jnp.float32),
            pltpu.VMEM((N_TILES, tb, LANES), jnp.float32),
            pltpu.VMEM((tb, SSM_WIDTH), jnp.bfloat16),
            pltpu.VMEM((tb, D_MODEL), jnp.float32),
            pltpu.VMEM((tb, D_MODEL), jnp.float32),
            pltpu.VMEM((tb, D_MODEL), jnp.bfloat16),
            pltpu.VMEM((tb, D_MODEL), jnp.float32),
            pltpu.VMEM((tb, D_MODEL), jnp.bfloat16),
            pltpu.VMEM((tb, D_MODEL), jnp.float32),
        ],
        compiler_params=pltpu.CompilerParams(
            dimension_semantics=("arbitrary",),
            vmem_limit_bytes=VMEM_LIMIT_BYTES),
        name="hybrid_s5_shortconv_block",
    )(*[a for a, _ in operands])
    return out[None]
```

```python
import math

import jax
import jax.numpy as jnp
from jax import lax
from jax.experimental import pallas as pl
from jax.experimental.pallas import tpu as pltpu

D_MODEL = 1024
PLE_DIM = 256
SSM_WIDTH = 512
SSM_GROUP = 16
SSM_GROUPS = 32
SSM_STATE = 64
CONV_WIDTH = 1024
CONV_K = 3
RMS_EPS = 1e-6
IN_WIDTH = 7168

LANES = 128
MXU_N = 256
SUBLANES = 8
ROW_TILE = 16
GROUPS_PER_TILE = LANES // SSM_GROUP
N_TILES = SSM_WIDTH // LANES
HALF = GROUPS_PER_TILE * SSM_STATE
TILE_STATE = 2 * HALF
N_COLS = HALF // LANES
N_STATE = SSM_GROUPS * SSM_STATE
ROW_BLOCK = 384
SEG_PAIRS = ROW_BLOCK // SUBLANES // 2
STAGE_ROWS = 64
VMEM_LIMIT_BYTES = 58 * 1024 * 1024

O_U, O_GS, O_XC, O_BC, O_CC, O_GC, O_RS, O_RC = 0, 512, 1024, 2048, 3072, 4096, 5120, 6144


def _rms(v, gain):
    return v * lax.rsqrt(jnp.mean(v * v, axis=-1, keepdims=True) + RMS_EPS) * gain


def _sigmoid(v):
    return 1.0 / (1.0 + jnp.exp(-v))


def _gelu_tanh(v):
    c = math.sqrt(2.0 / math.pi)
    return v * (0.5 * (1.0 + jnp.tanh(c * (v + 0.044715 * (v * v * v)))))


def _cmul(ar, ai, xr, xi):
    return ar * xr - ai * xi, ar * xi + ai * xr


def _expand_tables(xb_ref, yc_ref, sk_ref, a_ref, wb_ref, wc_ref, arow_ref, seg_ref):
    f32 = jnp.float32
    bf16 = jnp.bfloat16
    gmask = GROUPS_PER_TILE - 1

    def iota(shape, dim):
        return lax.broadcasted_iota(jnp.int32, shape, dim)

    def onehot(cond):
        return jnp.where(cond, 1.0, 0.0).astype(bf16)

    r = iota((LANES, TILE_STATE), 0)
    c = iota((LANES, TILE_STATE), 1)
    rep_b = onehot(((c & (SSM_STATE - 1)) == (r & (SSM_STATE - 1))) & (((c >> 7) & 1) == (r >> 6)))
    r = iota((2 * LANES, TILE_STATE), 0)
    c = iota((2 * LANES, TILE_STATE), 1)
    own_b = ((r >> 4) & gmask) == (((c >> 8) << 1) | ((c >> 6) & 1))
    r = iota((2 * SSM_GROUP, 2 * LANES), 0)
    c = iota((2 * SSM_GROUP, 2 * LANES), 1)
    rep_c = onehot(((c >> 7) == (r >> 4)) & ((c & (SSM_GROUP - 1)) == (r & (SSM_GROUP - 1))))
    r = iota((TILE_STATE, 2 * LANES), 0)
    c = iota((TILE_STATE, 2 * LANES), 1)
    own_c = (((r >> 8) << 1) | ((r >> 6) & 1)) == ((c >> 4) & gmask)
    r = iota((2 * LANES, 2 * LANES), 0)
    c = iota((2 * LANES, 2 * LANES), 1)
    own_s = ((r >> 4) & gmask) == ((c >> 4) & gmask)
    for t in range(N_TILES):
        wide = jnp.dot(xb_ref[t].astype(bf16), rep_b, preferred_element_type=f32)
        wb_ref[t] = jnp.where(own_b, wide, 0.0).astype(bf16)
        wide = jnp.dot(yc_ref[t].astype(bf16), rep_c, preferred_element_type=f32)
        wc_ref[t, 0:TILE_STATE, :] = jnp.where(own_c, wide, 0.0).astype(bf16)
        wide = jnp.dot(sk_ref[t].astype(bf16), rep_c, preferred_element_type=f32)
        wc_ref[t, TILE_STATE:TILE_STATE + 2 * LANES, :] = jnp.where(own_s, wide, 0.0).astype(bf16)

    sq = [_cmul(a_ref[0:1, :], a_ref[1:2, :], a_ref[0:1, :], a_ref[1:2, :])]
    while len(sq) < SEG_PAIRS.bit_length():
        sq.append(_cmul(*sq[-1], *sq[-1]))
    a_seg = None
    for k in range(SEG_PAIRS.bit_length()):
        if (SEG_PAIRS >> k) & 1:
            a_seg = sq[k] if a_seg is None else _cmul(*a_seg, *sq[k])
    shape = (SUBLANES, N_STATE)
    arow_ref[0] = jnp.broadcast_to(sq[0][0], shape)
    arow_ref[1] = jnp.broadcast_to(sq[0][1], shape)
    row = iota(shape, 0)
    seg_ref[6] = jnp.broadcast_to(a_seg[0], shape)
    seg_ref[7] = jnp.broadcast_to(a_seg[1], shape)
    for k, shift in enumerate((1, 2, 4)):
        seg_ref[2 * k] = jnp.where(row >= shift, jnp.broadcast_to(a_seg[0], shape), 0.0)
        seg_ref[2 * k + 1] = jnp.where(row >= shift, jnp.broadcast_to(a_seg[1], shape), 0.0)
        a_seg = _cmul(*a_seg, *a_seg)


def _load_weights(pairs, stage_ref, sem_ref):
    chunk_rows, stage_cols = stage_ref.shape[1:]
    rounds = []
    for src_ref, dst_ref in pairs:
        rows, width = src_ref.shape
        assert rows % chunk_rows == 0 and width <= stage_cols
        per_round = stage_cols // width
        starts = list(range(0, rows, chunk_rows))
        for i in range(0, len(starts), per_round):
            rounds.append([(src_ref, dst_ref, r0, width, j * width)
                           for j, r0 in enumerate(starts[i:i + per_round])])

    def copies(k):
        return [pltpu.make_async_copy(src_ref.at[pl.ds(r0, chunk_rows), :],
                                      stage_ref.at[k % 2, :, pl.ds(c0, width)],
                                      sem_ref.at[k % 2])
                for src_ref, _, r0, width, c0 in rounds[k]]

    for cp in copies(0):
        cp.start()
    for k, chunks in enumerate(rounds):
        if k + 1 < len(rounds):
            for cp in copies(k + 1):
                cp.start()
        for cp in copies(k):
            cp.wait()
        for _, dst_ref, r0, width, c0 in chunks:
            for j in range(width // MXU_N):
                dst_ref[j, r0:r0 + chunk_rows, :] = (
                    stage_ref[k % 2, :, c0 + j * MXU_N:c0 + (j + 1) * MXU_N].astype(dst_ref.dtype))


def _block_kernel(*refs):
    last = pl.num_programs(0) - 1

    @pl.when(pl.program_id(0) == 0)
    def _():
        _first_step_setup(*refs)

    @pl.when(pl.program_id(0) < last)
    def _():
        _run_step(True, *refs)

    @pl.when(pl.program_id(0) == last)
    def _():
        _run_step(False, *refs)


def _first_step_setup(x_ref, p_ref, pre_ref, win_hbm, xb_ref, yc_ref, sk_ref, a_ref,
                      wglu_hbm, bglu_ref, wso_hbm, convw_ref, wco_hbm, wo_hbm, post_ref,
                      wple_hbm, wpg_hbm, plen_ref, out_ref,
                      win_ref, wglu_ref, wso_ref, wco_ref, wo_ref, wple_ref, wpg_ref, stage_ref,
                      sem_ref, wb_ref, wc_ref, arow_ref, seg_ref, state_ref, h_ref, un_ref, up_ref,
                      v_ref, sloc_ref, s_ref, z_ref, gs_ref, mg_ref,
                      yp_ref, y_ref, y2_ref, o_ref, x1_ref, x1b_ref, e_ref, mprev_ref, xprev_ref):
    tb = x_ref.shape[0]
    _load_weights([(wo_hbm, wo_ref), (win_hbm, win_ref), (wpg_hbm, wpg_ref),
                   (wple_hbm, wple_ref), (wco_hbm, wco_ref), (wglu_hbm, wglu_ref),
                   (wso_hbm, wso_ref)], stage_ref, sem_ref)
    _expand_tables(xb_ref, yc_ref, sk_ref, a_ref, wb_ref, wc_ref, arow_ref, seg_ref)
    state_ref[...] = jnp.zeros_like(state_ref)
    v_ref[tb:tb + SUBLANES, :] = jnp.zeros((SUBLANES, CONV_WIDTH), jnp.float32)
    mprev_ref[...] = jnp.zeros_like(mprev_ref)
    xprev_ref[...] = jnp.zeros_like(xprev_ref)


def _run_step(mixers, x_ref, p_ref, pre_ref, win_hbm, xb_ref, yc_ref, sk_ref, a_ref,
              wglu_hbm, bglu_ref, wso_hbm, convw_ref, wco_hbm, wo_hbm, post_ref,
              wple_hbm, wpg_hbm, plen_ref, out_ref,
              win_ref, wglu_ref, wso_ref, wco_ref, wo_ref, wple_ref, wpg_ref, stage_ref, sem_ref,
              wb_ref, wc_ref, arow_ref, seg_ref, state_ref, h_ref, un_ref, up_ref, v_ref,
              sloc_ref, s_ref, z_ref, gs_ref, mg_ref,
              yp_ref, y_ref, y2_ref, o_ref, x1_ref, x1b_ref, e_ref, mprev_ref, xprev_ref):
    tb = x_ref.shape[0]
    ns = tb // SUBLANES
    npair = ns // 2
    f32 = jnp.float32
    bf16 = jnp.bfloat16

    def cols(n):
        return slice(n * MXU_N, (n + 1) * MXU_N)

    def tile_dot(lhs_ref, w_ref, off, n):
        return jnp.dot(lhs_ref[...], w_ref[off // MXU_N + n], preferred_element_type=f32)

    def finish_project():
        for n in range(D_MODEL // MXU_N):
            o_ref[:, cols(n)] = tile_dot(mprev_ref, wo_ref, 0, n)

    def finish_residual():
        for r in range(0, tb, ROW_TILE):
            rows = slice(r, r + ROW_TILE)
            x1 = xprev_ref[rows, :] + _rms(o_ref[rows, :], post_ref[...])
            x1_ref[rows, :] = x1
            x1b_ref[rows, :] = x1.astype(bf16)

    def finish_embed(n):
        eg = tile_dot(x1b_ref, wpg_ref, 0, n)
        ei = jnp.dot(p_ref[...].astype(bf16), wple_ref[n], preferred_element_type=f32)
        e_ref[:, cols(n)] = ei * _sigmoid(eg)

    def finish_store():
        for r in range(0, tb, ROW_TILE):
            rows = slice(r, r + ROW_TILE)
            out_ref[rows, :] = x1_ref[rows, :] + _rms(e_ref[rows, :], plen_ref[...])

    if not mixers:
        finish_project()
        finish_residual()
        for n in range(D_MODEL // MXU_N):
            finish_embed(n)
        finish_store()
        return

    finish_project()
    finish_residual()
    for r in range(0, tb, ROW_TILE):
        rows = slice(r, r + ROW_TILE)
        xs = x_ref[rows, :]
        h_ref[rows, :] = _rms(xs, pre_ref[...]).astype(bf16)
        xprev_ref[rows, :] = xs

    for n in range(SSM_WIDTH // MXU_N):
        u = tile_dot(h_ref, win_ref, O_U, n)
        for c in range(MXU_N // LANES):
            un_ref[n * (MXU_N // LANES) + c] = u[:, c * LANES:(c + 1) * LANES]
    for t in range(N_TILES):
        for par in range(2):
            for j in range(SUBLANES):
                for mh in range(npair // SUBLANES):
                    up_ref[par, t, pl.ds(SUBLANES * SUBLANES * mh + j, SUBLANES,
                                         stride=SUBLANES), :] = (
                        un_ref[t, pl.ds(j * ns + 2 * SUBLANES * mh + par, SUBLANES, stride=2), :])

    v_ref[0:SUBLANES, :] = v_ref[tb:tb + SUBLANES, :]
    for n in range(CONV_WIDTH // MXU_N):
        v_ref[SUBLANES:SUBLANES + tb, cols(n)] = (
            tile_dot(h_ref, win_ref, O_CC, n) * tile_dot(h_ref, win_ref, O_XC, n))

    srow = lax.broadcasted_iota(jnp.int32, (SUBLANES, LANES), 0)
    row0 = srow == 0

    def scan_tile(t):
        ub = jnp.concatenate([up_ref[0, t], up_ref[1, t]], axis=1).astype(bf16)
        for q in range(N_COLS):
            bu = jnp.dot(ub, wb_ref[t, :, cols(q)],
                         preferred_element_type=f32)
            re = slice(q * 2 * LANES, q * 2 * LANES + LANES)
            im = slice(q * 2 * LANES + LANES, (q + 1) * 2 * LANES)
            sl = slice(t * HALF + q * LANES, t * HALF + (q + 1) * LANES)
            ar = arow_ref[0, :, sl]
            ai = arow_ref[1, :, sl]
            sr = bu[0:SUBLANES, 0:LANES]
            si = bu[0:SUBLANES, LANES:2 * LANES]
            for m in range(1, npair):
                rows = slice(m * SUBLANES, (m + 1) * SUBLANES)
                sloc_ref[t, rows, re] = sr
                sloc_ref[t, rows, im] = si
                mr, mi = _cmul(ar, ai, sr, si)
                sr = mr + bu[rows, 0:LANES]
                si = mi + bu[rows, LANES:2 * LANES]
            gr = jnp.where(row0, state_ref[0, :, sl], pltpu.roll(sr, 1, 0))
            gi = jnp.where(row0, state_ref[1, :, sl], pltpu.roll(si, 1, 0))
            for k, shift in enumerate((1, 2, 4)):
                keep = srow >= shift
                fr, fi = _cmul(seg_ref[2 * k, :, sl], seg_ref[2 * k + 1, :, sl],
                               jnp.where(keep, pltpu.roll(gr, shift, 0), 0.0),
                               jnp.where(keep, pltpu.roll(gi, shift, 0), 0.0))
                gr, gi = gr + fr, gi + fi
            fr, fi = _cmul(seg_ref[6, :, sl], seg_ref[7, :, sl], gr, gi)
            nr, ni = fr + sr, fi + si
            state_ref[0, :, sl] = jnp.broadcast_to(nr[SUBLANES - 1:SUBLANES, :], nr.shape)
            state_ref[1, :, sl] = jnp.broadcast_to(ni[SUBLANES - 1:SUBLANES, :], ni.shape)
            fr, fi = gr, gi
            for m2 in range(npair // 2):
                parts_r, parts_i = [], []
                for m in (2 * m2, 2 * m2 + 1):
                    if m == 0:
                        parts_r.append(gr)
                        parts_i.append(gi)
                        continue
                    rows = slice(m * SUBLANES, (m + 1) * SUBLANES)
                    fr, fi = _cmul(ar, ai, fr, fi)
                    parts_r.append(sloc_ref[t, rows, re] + fr)
                    parts_i.append(sloc_ref[t, rows, im] + fi)
                rows2 = slice(2 * m2 * SUBLANES, (2 * m2 + 2) * SUBLANES)
                s_ref[t, rows2, re] = jnp.concatenate(parts_r, axis=0).astype(bf16)
                s_ref[t, rows2, im] = jnp.concatenate(parts_i, axis=0).astype(bf16)

    scan_tile(0)

    for n in range(CONV_WIDTH // MXU_N):
        bc = tile_dot(h_ref, win_ref, O_BC, n)
        gc = tile_dot(h_ref, win_ref, O_GC, n)
        conv = (convw_ref[0:1, cols(n)] * v_ref[SUBLANES - 2:SUBLANES - 2 + tb, cols(n)]
                + convw_ref[1:2, cols(n)] * v_ref[SUBLANES - 1:SUBLANES - 1 + tb, cols(n)]
                + convw_ref[2:3, cols(n)] * v_ref[SUBLANES:SUBLANES + tb, cols(n)])
        z_ref[:, cols(n)] = (bc * conv * (gc * _sigmoid(gc))).astype(bf16)
        if n == 1:
            scan_tile(1)

    for n in range(D_MODEL // MXU_N):
        finish_embed(n)
        if n == 1:
            scan_tile(2)

    for n in range(SSM_WIDTH // MXU_N):
        gs = tile_dot(h_ref, win_ref, O_GS, n)
        gs_ref[:, cols(n)] = gs * _sigmoid(gs)
    scan_tile(3)

    finish_store()

    def gated_b(n):
        rc = tile_dot(h_ref, win_ref, O_RC, n)
        yb = tile_dot(z_ref, wco_ref, 0, n)
        mg_ref[:, cols(n)] = _sigmoid(rc) * yb

    gated_b(0)
    gated_b(1)

    for t in range(N_TILES):
        lhs = jnp.concatenate(
            [s_ref[t], up_ref[0, t].astype(bf16), up_ref[1, t].astype(bf16)], axis=1)
        yy = jnp.dot(lhs, wc_ref[t], preferred_element_type=f32)
        yp_ref[0, t] = yy[:, 0:LANES]
        yp_ref[1, t] = yy[:, LANES:2 * LANES]
    gated_b(2)
    gated_b(3)
    for t in range(N_TILES):
        for par in range(2):
            for j in range(SUBLANES):
                for mh in range(npair // SUBLANES):
                    y_ref[t, pl.ds(j * ns + 2 * SUBLANES * mh + par, SUBLANES, stride=2), :] = (
                        _gelu_tanh(yp_ref[par, t, pl.ds(SUBLANES * SUBLANES * mh + j, SUBLANES,
                                                        stride=SUBLANES), :]))
    y = jnp.concatenate([y_ref[t] for t in range(N_TILES)], axis=1)
    yb16 = y.astype(bf16)
    for n in range(SSM_WIDTH // MXU_N):
        glu = jnp.dot(yb16, wglu_ref[n], preferred_element_type=f32)
        y2_ref[:, cols(n)] = (y[:, cols(n)] * _sigmoid(glu + bglu_ref[:, cols(n)])
                              * gs_ref[:, cols(n)]).astype(bf16)

    n_out = D_MODEL // MXU_N
    rs = [tile_dot(h_ref, win_ref, O_RS, n) for n in range(2)]
    for n in range(n_out):
        ya = tile_dot(y2_ref, wso_ref, 0, n)
        mprev_ref[:, cols(n)] = (_sigmoid(rs[n]) * ya + mg_ref[:, cols(n)]).astype(bf16)
        if n + 2 < n_out:
            rs.append(tile_dot(h_ref, win_ref, O_RS, n + 2))


def _ssm_params(a_re, a_im, log_step, b_re, b_im, c_re, c_im, d_skip):
    f32 = jnp.float32
    G, P, H, T, GT = SSM_GROUPS, SSM_STATE, SSM_GROUP, N_TILES, GROUPS_PER_TILE
    dt = jnp.exp(log_step.astype(f32))[:, None]
    mag = jnp.exp(a_re * dt)
    abar_r = mag * jnp.cos(a_im * dt)
    abar_i = mag * jnp.sin(a_im * dt)
    den = a_re * a_re + a_im * a_im
    nr = abar_r - 1.0
    fr = (nr * a_re + abar_i * a_im) / den
    fi = (abar_i * a_re - nr * a_im) / den
    bbar_r = fr[..., None] * b_re - fi[..., None] * b_im
    bbar_i = fr[..., None] * b_im + fi[..., None] * b_re
    ab_r, ab_i = _cmul(abar_r[..., None], abar_i[..., None], bbar_r, bbar_i)
    ca_r, ca_i = _cmul(abar_r[:, None, :], abar_i[:, None, :], c_re, c_im)
    caa_r, caa_i = _cmul(abar_r[:, None, :], abar_i[:, None, :], ca_r, ca_i)

    def re_prod(xr, xi, yr, yi):
        yr_t = jnp.swapaxes(yr, 1, 2)[:, None]
        yi_t = jnp.swapaxes(yi, 1, 2)[:, None]
        return jnp.sum(xr[:, :, None, :] * yr_t - xi[:, :, None, :] * yi_t, axis=-1)

    cb_d = (re_prod(c_re, c_im, bbar_r, bbar_i)
            + jnp.eye(H, dtype=f32) * d_skip.reshape(G, H, 1))
    cab = re_prod(ca_r, ca_i, bbar_r, bbar_i)

    xb = jnp.stack([jnp.stack([ab_r, ab_i]), jnp.stack([bbar_r, bbar_i])])
    xb = xb.reshape(2, 2, T, GT, P, H).transpose(2, 0, 3, 5, 1, 4).reshape(T, 2 * LANES, 2 * P)
    yc = jnp.stack([jnp.stack([ca_r, -ca_i]), jnp.stack([caa_r, -caa_i])])
    yc = yc.reshape(2, 2, T, N_COLS, 2, H, P).transpose(2, 3, 1, 4, 6, 0, 5)
    yc = yc.reshape(T, TILE_STATE, 2 * H)
    zero = jnp.zeros_like(cab)
    sk = jnp.stack([jnp.stack([cb_d, cab]), jnp.stack([zero, cb_d])])
    sk = sk.reshape(2, 2, T, GT, H, H).transpose(2, 0, 3, 5, 1, 4).reshape(T, 2 * LANES, 2 * H)
    a = jnp.stack([abar_r.reshape(N_STATE), abar_i.reshape(N_STATE)])
    bf16 = jnp.bfloat16
    return xb.astype(bf16), yc.astype(bf16), sk.astype(bf16), a


def _resident(shape):
    return pl.BlockSpec(shape, lambda i: (0,) * len(shape), pipeline_mode=pl.Buffered(1))


@jax.jit
def kernel(x, p, pre_norm, w_in, a_re, a_im, log_step, b_re, b_im, c_re, c_im, d_skip,
           w_glu, b_glu, w_ssm_out, conv_w, w_conv_out, w_o, post_norm,
           w_ple, w_ple_gate, ple_norm):
    bsz, seqlen, _ = x.shape
    depth = p.shape[0]
    assert bsz == 1 and depth == 1
    bf16 = jnp.bfloat16
    tb = ROW_BLOCK

    xb, yc, sk, a_bar = _ssm_params(
        a_re[0], a_im[0], log_step[0], b_re[0], b_im[0], c_re[0], c_im[0], d_skip[0])
    nb = pl.cdiv(seqlen, tb)
    cur = lambda i: (jnp.minimum(i, nb - 1), 0)
    prev = lambda i: (jnp.maximum(i - 1, 0), 0)
    hbm = pl.BlockSpec(memory_space=pl.ANY)
    operands = [
        (x[0], pl.BlockSpec((tb, D_MODEL), cur)),
        (p[0, 0], pl.BlockSpec((tb, PLE_DIM), prev)),
        (pre_norm, _resident((1, D_MODEL))),
        (w_in.reshape(D_MODEL, IN_WIDTH), hbm),
        (xb, _resident(xb.shape)),
        (yc, _resident(yc.shape)),
        (sk, _resident(sk.shape)),
        (a_bar, _resident(a_bar.shape)),
        (w_glu.reshape(SSM_WIDTH, SSM_WIDTH), hbm),
        (b_glu, _resident((1, SSM_WIDTH))),
        (w_ssm_out.reshape(SSM_WIDTH, D_MODEL), hbm),
        (conv_w[0], _resident((CONV_K, CONV_WIDTH))),
        (w_conv_out.reshape(CONV_WIDTH, D_MODEL), hbm),
        (w_o.reshape(D_MODEL, D_MODEL), hbm),
        (post_norm, _resident((1, D_MODEL))),
        (w_ple.reshape(PLE_DIM, D_MODEL), hbm),
        (w_ple_gate.reshape(D_MODEL, D_MODEL), hbm),
        (ple_norm, _resident((1, D_MODEL))),
    ]
    out = pl.pallas_call(
        _block_kernel,
        grid=(nb + 1,),
        in_specs=[spec for _, spec in operands],
        out_specs=pl.BlockSpec((tb, D_MODEL), prev),
        out_shape=jax.ShapeDtypeStruct((seqlen, D_MODEL), jnp.float32),
        scratch_shapes=[
            pltpu.VMEM((IN_WIDTH // MXU_N, D_MODEL, MXU_N), bf16),
            pltpu.VMEM((SSM_WIDTH // MXU_N, SSM_WIDTH, MXU_N), bf16),
            pltpu.VMEM((D_MODEL // MXU_N, SSM_WIDTH, MXU_N), bf16),
            pltpu.VMEM((D_MODEL // MXU_N, CONV_WIDTH, MXU_N), bf16),
            pltpu.VMEM((D_MODEL // MXU_N, D_MODEL, MXU_N), bf16),
            pltpu.VMEM((D_MODEL // MXU_N, PLE_DIM, MXU_N), bf16),
            pltpu.VMEM((D_MODEL // MXU_N, D_MODEL, MXU_N), bf16),
            pltpu.VMEM((2, STAGE_ROWS, IN_WIDTH), jnp.float32),
            pltpu.SemaphoreType.DMA((2,)),
            pltpu.VMEM((N_TILES, 2 * LANES, TILE_STATE), jnp.bfloat16),
            pltpu.VMEM((N_TILES, TILE_STATE + 2 * LANES, 2 * LANES), jnp.bfloat16),
            pltpu.VMEM((2, SUBLANES, N_STATE), jnp.float32),
            pltpu.VMEM((8, SUBLANES, N_STATE), jnp.float32),
            pltpu.VMEM((2, SUBLANES, N_STATE), jnp.float32),
            pltpu.VMEM((tb, D_MODEL), jnp.bfloat16),
            pltpu.VMEM((N_TILES, tb, LANES), jnp.float32),
            pltpu.VMEM((2, N_TILES, tb // 2, LANES), jnp.float32),
            pltpu.VMEM((tb + SUBLANES, CONV_WIDTH), jnp.float32),
            pltpu.VMEM((N_TILES, tb // 2, TILE_STATE), jnp.float32),
            pltpu.VMEM((N_TILES, tb // 2, TILE_STATE), jnp.bfloat16),
            pltpu.VMEM((tb, CONV_WIDTH), jnp.bfloat16),
            pltpu.VMEM((tb, SSM_WIDTH), jnp.float32),
            pltpu.VMEM((tb, D_MODEL), jnp.float32),
            pltpu.VMEM((2, N_TILES, tb // 2, LANES), jnp.float32),
            pltpu.VMEM((N_TILES, tb, LANES), jnp.float32),
            pltpu.VMEM((tb, SSM_WIDTH), jnp.bfloat16),
            pltpu.VMEM((tb, D_MODEL), jnp.float32),
            pltpu.VMEM((tb, D_MODEL), jnp.float32),
            pltpu.VMEM((tb, D_MODEL), jnp.bfloat16),
            pltpu.VMEM((tb, D_MODEL), jnp.float32),
            pltpu.VMEM((tb, D_MODEL), jnp.bfloat16),
            pltpu.VMEM((tb, D_MODEL), jnp.float32),
        ],
        compiler_params=pltpu.CompilerParams(
            dimension_semantics=("arbitrary",),
            vmem_limit_bytes=VMEM_LIMIT_BYTES),
        name="hybrid_s5_shortconv_block",
    )(*[a for a, _ in operands])
    return out[None]
```

```python
import math

import jax
import jax.numpy as jnp
from jax import lax
from jax.experimental import pallas as pl
from jax.experimental.pallas import tpu as pltpu

D_MODEL = 1024
PLE_DIM = 256
SSM_WIDTH = 512
SSM_GROUP = 16
SSM_GROUPS = 32
SSM_STATE = 64
CONV_WIDTH = 1024
CONV_K = 3
RMS_EPS = 1e-6
IN_WIDTH = 7168

LANES = 128
MXU_N = 256
SUBLANES = 8
ROW_TILE = 16
GROUPS_PER_TILE = LANES // SSM_GROUP
N_TILES = SSM_WIDTH // LANES
HALF = GROUPS_PER_TILE * SSM_STATE
TILE_STATE = 2 * HALF
N_COLS = HALF // LANES
N_STATE = SSM_GROUPS * SSM_STATE
ROW_BLOCK = 384
SEG_PAIRS = ROW_BLOCK // SUBLANES // 2
STAGE_ROWS = 64
VMEM_LIMIT_BYTES = 58 * 1024 * 1024

O_U, O_GS, O_XC, O_BC, O_CC, O_GC, O_RS, O_RC = 0, 512, 1024, 2048, 3072, 4096, 5120, 6144


def _rms(v, gain):
    return v * lax.rsqrt(jnp.mean(v * v, axis=-1, keepdims=True) + RMS_EPS) * gain


def _sigmoid(v):
    return 1.0 / (1.0 + jnp.exp(-v))


def _gelu_tanh(v):
    c = math.sqrt(2.0 / math.pi)
    return v * (0.5 * (1.0 + jnp.tanh(c * (v + 0.044715 * (v * v * v)))))


def _cmul(ar, ai, xr, xi):
    return ar * xr - ai * xi, ar * xi + ai * xr


def _expand_tables(xb_ref, yc_ref, sk_ref, a_ref, wb_ref, wc_ref, arow_ref, seg_ref):
    f32 = jnp.float32
    bf16 = jnp.bfloat16
    gmask = GROUPS_PER_TILE - 1

    def iota(shape, dim):
        return lax.broadcasted_iota(jnp.int32, shape, dim)

    def onehot(cond):
        return jnp.where(cond, 1.0, 0.0).astype(bf16)

    r = iota((LANES, TILE_STATE), 0)
    c = iota((LANES, TILE_STATE), 1)
    rep_b = onehot(((c & (SSM_STATE - 1)) == (r & (SSM_STATE - 1))) & (((c >> 7) & 1) == (r >> 6)))
    r = iota((2 * LANES, TILE_STATE), 0)
    c = iota((2 * LANES, TILE_STATE), 1)
    own_b = ((r >> 4) & gmask) == (((c >> 8) << 1) | ((c >> 6) & 1))
    r = iota((2 * SSM_GROUP, 2 * LANES), 0)
    c = iota((2 * SSM_GROUP, 2 * LANES), 1)
    rep_c = onehot(((c >> 7) == (r >> 4)) & ((c & (SSM_GROUP - 1)) == (r & (SSM_GROUP - 1))))
    r = iota((TILE_STATE, 2 * LANES), 0)
    c = iota((TILE_STATE, 2 * LANES), 1)
    own_c = (((r >> 8) << 1) | ((r >> 6) & 1)) == ((c >> 4) & gmask)
    r = iota((2 * LANES, 2 * LANES), 0)
    c = iota((2 * LANES, 2 * LANES), 1)
    own_s = ((r >> 4) & gmask) == ((c >> 4) & gmask)
    for t in range(N_TILES):
        wide = jnp.dot(xb_ref[t].astype(bf16), rep_b, preferred_element_type=f32)
        wb_ref[t] = jnp.where(own_b, wide, 0.0).astype(bf16)
        wide = jnp.dot(yc_ref[t].astype(bf16), rep_c, preferred_element_type=f32)
        wc_ref[t, 0:TILE_STATE, :] = jnp.where(own_c, wide, 0.0).astype(bf16)
        wide = jnp.dot(sk_ref[t].astype(bf16), rep_c, preferred_element_type=f32)
        wc_ref[t, TILE_STATE:TILE_STATE + 2 * LANES, :] = jnp.where(own_s, wide, 0.0).astype(bf16)

    sq = [_cmul(a_ref[0:1, :], a_ref[1:2, :], a_ref[0:1, :], a_ref[1:2, :])]
    while len(sq) < SEG_PAIRS.bit_length():
        sq.append(_cmul(*sq[-1], *sq[-1]))
    a_seg = None
    for k in range(SEG_PAIRS.bit_length()):
        if (SEG_PAIRS >> k) & 1:
            a_seg = sq[k] if a_seg is None else _cmul(*a_seg, *sq[k])
    shape = (SUBLANES, N_STATE)
    arow_ref[0] = jnp.broadcast_to(sq[0][0], shape)
    arow_ref[1] = jnp.broadcast_to(sq[0][1], shape)
    row = iota(shape, 0)
    seg_ref[6] = jnp.broadcast_to(a_seg[0], shape)
    seg_ref[7] = jnp.broadcast_to(a_seg[1], shape)
    for k, shift in enumerate((1, 2, 4)):
        seg_ref[2 * k] = jnp.where(row >= shift, jnp.broadcast_to(a_seg[0], shape), 0.0)
        seg_ref[2 * k + 1] = jnp.where(row >= shift, jnp.broadcast_to(a_seg[1], shape), 0.0)
        a_seg = _cmul(*a_seg, *a_seg)


def _load_weights(pairs, stage_ref, sem_ref):
    chunk_rows, stage_cols = stage_ref.shape[1:]
    rounds = []
    for src_ref, dst_ref in pairs:
        rows, width = src_ref.shape
        assert rows % chunk_rows == 0 and width <= stage_cols
        per_round = stage_cols // width
        starts = list(range(0, rows, chunk_rows))
        for i in range(0, len(starts), per_round):
            rounds.append([(src_ref, dst_ref, r0, width, j * width)
                           for j, r0 in enumerate(starts[i:i + per_round])])

    def copies(k):
        return [pltpu.make_async_copy(src_ref.at[pl.ds(r0, chunk_rows), :],
                                      stage_ref.at[k % 2, :, pl.ds(c0, width)],
                                      sem_ref.at[k % 2])
                for src_ref, _, r0, width, c0 in rounds[k]]

    for cp in copies(0):
        cp.start()
    for k, chunks in enumerate(rounds):
        if k + 1 < len(rounds):
            for cp in copies(k + 1):
                cp.start()
        for cp in copies(k):
            cp.wait()
        for _, dst_ref, r0, width, c0 in chunks:
            for j in range(width // MXU_N):
                dst_ref[j, r0:r0 + chunk_rows, :] = (
                    stage_ref[k % 2, :, c0 + j * MXU_N:c0 + (j + 1) * MXU_N].astype(dst_ref.dtype))


def _block_kernel(*refs):
    last = pl.num_programs(0) - 1

    @pl.when(pl.program_id(0) == 0)
    def _():
        _first_step_setup(*refs)

    @pl.when(pl.program_id(0) < last)
    def _():
        _run_step(True, *refs)

    @pl.when(pl.program_id(0) == last)
    def _():
        _run_step(False, *refs)


def _first_step_setup(x_ref, p_ref, pre_ref, win_hbm, xb_ref, yc_ref, sk_ref, a_ref,
                      wglu_hbm, bglu_ref, wso_hbm, convw_ref, wco_hbm, wo_hbm, post_ref,
                      wple_hbm, wpg_hbm, plen_ref, out_ref,
                      win_ref, wglu_ref, wso_ref, wco_ref, wo_ref, wple_ref, wpg_ref, stage_ref,
                      sem_ref, wb_ref, wc_ref, arow_ref, seg_ref, state_ref, h_ref, un_ref, up_ref,
                      v_ref, sloc_ref, s_ref, z_ref, gs_ref, mg_ref,
                      yp_ref, y_ref, y2_ref, o_ref, x1_ref, x1b_ref, e_ref, mprev_ref, xprev_ref):
    tb = x_ref.shape[0]
    _load_weights([(wo_hbm, wo_ref), (win_hbm, win_ref), (wpg_hbm, wpg_ref),
                   (wple_hbm, wple_ref), (wco_hbm, wco_ref), (wglu_hbm, wglu_ref),
                   (wso_hbm, wso_ref)], stage_ref, sem_ref)
    _expand_tables(xb_ref, yc_ref, sk_ref, a_ref, wb_ref, wc_ref, arow_ref, seg_ref)
    state_ref[...] = jnp.zeros_like(state_ref)
    v_ref[tb:tb + SUBLANES, :] = jnp.zeros((SUBLANES, CONV_WIDTH), jnp.float32)
    mprev_ref[...] = jnp.zeros_like(mprev_ref)
    xprev_ref[...] = jnp.zeros_like(xprev_ref)


def _run_step(mixers, x_ref, p_ref, pre_ref, win_hbm, xb_ref, yc_ref, sk_ref, a_ref,
              wglu_hbm, bglu_ref, wso_hbm, convw_ref, wco_hbm, wo_hbm, post_ref,
              wple_hbm, wpg_hbm, plen_ref, out_ref,
              win_ref, wglu_ref, wso_ref, wco_ref, wo_ref, wple_ref, wpg_ref, stage_ref, sem_ref,
              wb_ref, wc_ref, arow_ref, seg_ref, state_ref, h_ref, un_ref, up_ref, v_ref,
              sloc_ref, s_ref, z_ref, gs_ref, mg_ref,
              yp_ref, y_ref, y2_ref, o_ref, x1_ref, x1b_ref, e_ref, mprev_ref, xprev_ref):
    tb = x_ref.shape[0]
    ns = tb // SUBLANES
    npair = ns // 2
    f32 = jnp.float32
    bf16 = jnp.bfloat16

    def cols(n):
        return slice(n * MXU_N, (n + 1) * MXU_N)

    def tile_dot(lhs_ref, w_ref, off, n):
        lhs = jnp.concatenate([lhs_ref[k] for k in range(lhs_ref.shape[0])], axis=1)
        return jnp.dot(lhs, w_ref[off // MXU_N + n], preferred_element_type=f32)

    def store_ktiles(dst_ref, rows, val):
        for k in range(dst_ref.shape[0]):
            dst_ref[k, rows, :] = val[:, k * MXU_N:(k + 1) * MXU_N]

    def finish_project():
        for n in range(D_MODEL // MXU_N):
            o_ref[:, cols(n)] = tile_dot(mprev_ref, wo_ref, 0, n)

    def finish_residual():
        for r in range(0, tb, ROW_TILE):
            rows = slice(r, r + ROW_TILE)
            x1 = xprev_ref[rows, :] + _rms(o_ref[rows, :], post_ref[...])
            x1_ref[rows, :] = x1
            store_ktiles(x1b_ref, rows, x1.astype(bf16))

    def finish_embed(n):
        eg = tile_dot(x1b_ref, wpg_ref, 0, n)
        ei = jnp.dot(p_ref[...].astype(bf16), wple_ref[n], preferred_element_type=f32)
        e_ref[:, cols(n)] = ei * _sigmoid(eg)

    def finish_store():
        for r in range(0, tb, ROW_TILE):
            rows = slice(r, r + ROW_TILE)
            out_ref[rows, :] = x1_ref[rows, :] + _rms(e_ref[rows, :], plen_ref[...])

    if not mixers:
        finish_project()
        finish_residual()
        for n in range(D_MODEL // MXU_N):
            finish_embed(n)
        finish_store()
        return

    finish_project()
    finish_residual()
    for r in range(0, tb, ROW_TILE):
        rows = slice(r, r + ROW_TILE)
        xs = x_ref[rows, :]
        store_ktiles(h_ref, rows, _rms(xs, pre_ref[...]).astype(bf16))
        xprev_ref[rows, :] = xs

    for n in range(SSM_WIDTH // MXU_N):
        u = tile_dot(h_ref, win_ref, O_U, n)
        for c in range(MXU_N // LANES):
            un_ref[n * (MXU_N // LANES) + c] = u[:, c * LANES:(c + 1) * LANES]
    for t in range(N_TILES):
        for par in range(2):
            for j in range(SUBLANES):
                for mh in range(npair // SUBLANES):
                    up_ref[par, t, pl.ds(SUBLANES * SUBLANES * mh + j, SUBLANES,
                                         stride=SUBLANES), :] = (
                        un_ref[t, pl.ds(j * ns + 2 * SUBLANES * mh + par, SUBLANES, stride=2), :])

    v_ref[0:SUBLANES, :] = v_ref[tb:tb + SUBLANES, :]
    for n in range(CONV_WIDTH // MXU_N):
        v_ref[SUBLANES:SUBLANES + tb, cols(n)] = (
            tile_dot(h_ref, win_ref, O_CC, n) * tile_dot(h_ref, win_ref, O_XC, n))

    srow = lax.broadcasted_iota(jnp.int32, (SUBLANES, LANES), 0)
    row0 = srow == 0

    def scan_tile(t):
        ub = jnp.concatenate([up_ref[0, t], up_ref[1, t]], axis=1).astype(bf16)
        for q in range(N_COLS):
            bu = jnp.dot(ub, wb_ref[t, :, cols(q)],
                         preferred_element_type=f32)
            re = slice(q * 2 * LANES, q * 2 * LANES + LANES)
            im = slice(q * 2 * LANES + LANES, (q + 1) * 2 * LANES)
            sl = slice(t * HALF + q * LANES, t * HALF + (q + 1) * LANES)
            ar = arow_ref[0, :, sl]
            ai = arow_ref[1, :, sl]
            sr = bu[0:SUBLANES, 0:LANES]
            si = bu[0:SUBLANES, LANES:2 * LANES]
            for m in range(1, npair):
                rows = slice(m * SUBLANES, (m + 1) * SUBLANES)
                sloc_ref[t, rows, re] = sr
                sloc_ref[t, rows, im] = si
                mr, mi = _cmul(ar, ai, sr, si)
                sr = mr + bu[rows, 0:LANES]
                si = mi + bu[rows, LANES:2 * LANES]
            gr = jnp.where(row0, state_ref[0, :, sl], pltpu.roll(sr, 1, 0))
            gi = jnp.where(row0, state_ref[1, :, sl], pltpu.roll(si, 1, 0))
            for k, shift in enumerate((1, 2, 4)):
                keep = srow >= shift
                fr, fi = _cmul(seg_ref[2 * k, :, sl], seg_ref[2 * k + 1, :, sl],
                               jnp.where(keep, pltpu.roll(gr, shift, 0), 0.0),
                               jnp.where(keep, pltpu.roll(gi, shift, 0), 0.0))
                gr, gi = gr + fr, gi + fi
            fr, fi = _cmul(seg_ref[6, :, sl], seg_ref[7, :, sl], gr, gi)
            nr, ni = fr + sr, fi + si
            state_ref[0, :, sl] = jnp.broadcast_to(nr[SUBLANES - 1:SUBLANES, :], nr.shape)
            state_ref[1, :, sl] = jnp.broadcast_to(ni[SUBLANES - 1:SUBLANES, :], ni.shape)
            fr, fi = gr, gi
            for m2 in range(npair // 2):
                parts_r, parts_i = [], []
                for m in (2 * m2, 2 * m2 + 1):
                    if m == 0:
                        parts_r.append(gr)
                        parts_i.append(gi)
                        continue
                    rows = slice(m * SUBLANES, (m + 1) * SUBLANES)
                    fr, fi = _cmul(ar, ai, fr, fi)
                    parts_r.append(sloc_ref[t, rows, re] + fr)
                    parts_i.append(sloc_ref[t, rows, im] + fi)
                rows2 = slice(2 * m2 * SUBLANES, (2 * m2 + 2) * SUBLANES)
                s_ref[t, q, rows2, 0:LANES] = jnp.concatenate(parts_r, axis=0).astype(bf16)
                s_ref[t, q, rows2, LANES:2 * LANES] = jnp.concatenate(parts_i, axis=0).astype(bf16)

    scan_tile(0)

    for n in range(CONV_WIDTH // MXU_N):
        bc = tile_dot(h_ref, win_ref, O_BC, n)
        gc = tile_dot(h_ref, win_ref, O_GC, n)
        conv = (convw_ref[0:1, cols(n)] * v_ref[SUBLANES - 2:SUBLANES - 2 + tb, cols(n)]
                + convw_ref[1:2, cols(n)] * v_ref[SUBLANES - 1:SUBLANES - 1 + tb, cols(n)]
                + convw_ref[2:3, cols(n)] * v_ref[SUBLANES:SUBLANES + tb, cols(n)])
        z_ref[n] = (bc * conv * (gc * _sigmoid(gc))).astype(bf16)
        if n == 1:
            scan_tile(1)

    for n in range(D_MODEL // MXU_N):
        finish_embed(n)
        if n == 1:
            scan_tile(2)

    for n in range(SSM_WIDTH // MXU_N):
        gs = tile_dot(h_ref, win_ref, O_GS, n)
        gs_ref[:, cols(n)] = gs * _sigmoid(gs)
    scan_tile(3)

    finish_store()

    def gated_b(n):
        rc = tile_dot(h_ref, win_ref, O_RC, n)
        yb = tile_dot(z_ref, wco_ref, 0, n)
        mg_ref[:, cols(n)] = _sigmoid(rc) * yb

    gated_b(0)
    gated_b(1)

    for t in range(N_TILES):
        lhs = jnp.concatenate(
            [s_ref[t, q] for q in range(N_COLS)]
            + [up_ref[0, t].astype(bf16), up_ref[1, t].astype(bf16)], axis=1)
        yy = jnp.dot(lhs, wc_ref[t], preferred_element_type=f32)
        yp_ref[0, t] = yy[:, 0:LANES]
        yp_ref[1, t] = yy[:, LANES:2 * LANES]
    gated_b(2)
    gated_b(3)
    for t in range(N_TILES):
        for par in range(2):
            for j in range(SUBLANES):
                for mh in range(npair // SUBLANES):
                    y_ref[t, pl.ds(j * ns + 2 * SUBLANES * mh + par, SUBLANES, stride=2), :] = (
                        _gelu_tanh(yp_ref[par, t, pl.ds(SUBLANES * SUBLANES * mh + j, SUBLANES,
                                                        stride=SUBLANES), :]))
    y = jnp.concatenate([y_ref[t] for t in range(N_TILES)], axis=1)
    yb16 = y.astype(bf16)
    for n in range(SSM_WIDTH // MXU_N):
        glu = jnp.dot(yb16, wglu_ref[n], preferred_element_type=f32)
        y2_ref[n] = (y[:, cols(n)] * _sigmoid(glu + bglu_ref[:, cols(n)])
                     * gs_ref[:, cols(n)]).astype(bf16)

    n_out = D_MODEL // MXU_N
    rs = [tile_dot(h_ref, win_ref, O_RS, n) for n in range(2)]
    for n in range(n_out):
        ya = tile_dot(y2_ref, wso_ref, 0, n)
        mprev_ref[n] = (_sigmoid(rs[n]) * ya + mg_ref[:, cols(n)]).astype(bf16)
        if n + 2 < n_out:
            rs.append(tile_dot(h_ref, win_ref, O_RS, n + 2))


def _ssm_params(a_re, a_im, log_step, b_re, b_im, c_re, c_im, d_skip):
    f32 = jnp.float32
    G, P, H, T, GT = SSM_GROUPS, SSM_STATE, SSM_GROUP, N_TILES, GROUPS_PER_TILE
    dt = jnp.exp(log_step.astype(f32))[:, None]
    mag = jnp.exp(a_re * dt)
    abar_r = mag * jnp.cos(a_im * dt)
    abar_i = mag * jnp.sin(a_im * dt)
    den = a_re * a_re + a_im * a_im
    nr = abar_r - 1.0
    fr = (nr * a_re + abar_i * a_im) / den
    fi = (abar_i * a_re - nr * a_im) / den
    bbar_r = fr[..., None] * b_re - fi[..., None] * b_im
    bbar_i = fr[..., None] * b_im + fi[..., None] * b_re
    ab_r, ab_i = _cmul(abar_r[..., None], abar_i[..., None], bbar_r, bbar_i)
    ca_r, ca_i = _cmul(abar_r[:, None, :], abar_i[:, None, :], c_re, c_im)
    caa_r, caa_i = _cmul(abar_r[:, None, :], abar_i[:, None, :], ca_r, ca_i)

    def re_prod(xr, xi, yr, yi):
        yr_t = jnp.swapaxes(yr, 1, 2)[:, None]
        yi_t = jnp.swapaxes(yi, 1, 2)[:, None]
        return jnp.sum(xr[:, :, None, :] * yr_t - xi[:, :, None, :] * yi_t, axis=-1)

    cb_d = (re_prod(c_re, c_im, bbar_r, bbar_i)
            + jnp.eye(H, dtype=f32) * d_skip.reshape(G, H, 1))
    cab = re_prod(ca_r, ca_i, bbar_r, bbar_i)

    xb = jnp.stack([jnp.stack([ab_r, ab_i]), jnp.stack([bbar_r, bbar_i])])
    xb = xb.reshape(2, 2, T, GT, P, H).transpose(2, 0, 3, 5, 1, 4).reshape(T, 2 * LANES, 2 * P)
    yc = jnp.stack([jnp.stack([ca_r, -ca_i]), jnp.stack([caa_r, -caa_i])])
    yc = yc.reshape(2, 2, T, N_COLS, 2, H, P).transpose(2, 3, 1, 4, 6, 0, 5)
    yc = yc.reshape(T, TILE_STATE, 2 * H)
    zero = jnp.zeros_like(cab)
    sk = jnp.stack([jnp.stack([cb_d, cab]), jnp.stack([zero, cb_d])])
    sk = sk.reshape(2, 2, T, GT, H, H).transpose(2, 0, 3, 5, 1, 4).reshape(T, 2 * LANES, 2 * H)
    a = jnp.stack([abar_r.reshape(N_STATE), abar_i.reshape(N_STATE)])
    bf16 = jnp.bfloat16
    return xb.astype(bf16), yc.astype(bf16), sk.astype(bf16), a


def _resident(shape):
    return pl.BlockSpec(shape, lambda i: (0,) * len(shape), pipeline_mode=pl.Buffered(1))


@jax.jit
def kernel(x, p, pre_norm, w_in, a_re, a_im, log_step, b_re, b_im, c_re, c_im, d_skip,
           w_glu, b_glu, w_ssm_out, conv_w, w_conv_out, w_o, post_norm,
           w_ple, w_ple_gate, ple_norm):
    bsz, seqlen, _ = x.shape
    depth = p.shape[0]
    assert bsz == 1 and depth == 1
    bf16 = jnp.bfloat16
    tb = ROW_BLOCK

    xb, yc, sk, a_bar = _ssm_params(
        a_re[0], a_im[0], log_step[0], b_re[0], b_im[0], c_re[0], c_im[0], d_skip[0])
    nb = pl.cdiv(seqlen, tb)
    cur = lambda i: (jnp.minimum(i, nb - 1), 0)
    prev = lambda i: (jnp.maximum(i - 1, 0), 0)
    hbm = pl.BlockSpec(memory_space=pl.ANY)
    operands = [
        (x[0], pl.BlockSpec((tb, D_MODEL), cur)),
        (p[0, 0], pl.BlockSpec((tb, PLE_DIM), prev)),
        (pre_norm, _resident((1, D_MODEL))),
        (w_in.reshape(D_MODEL, IN_WIDTH), hbm),
        (xb, _resident(xb.shape)),
        (yc, _resident(yc.shape)),
        (sk, _resident(sk.shape)),
        (a_bar, _resident(a_bar.shape)),
        (w_glu.reshape(SSM_WIDTH, SSM_WIDTH), hbm),
        (b_glu, _resident((1, SSM_WIDTH))),
        (w_ssm_out.reshape(SSM_WIDTH, D_MODEL), hbm),
        (conv_w[0], _resident((CONV_K, CONV_WIDTH))),
        (w_conv_out.reshape(CONV_WIDTH, D_MODEL), hbm),
        (w_o.reshape(D_MODEL, D_MODEL), hbm),
        (post_norm, _resident((1, D_MODEL))),
        (w_ple.reshape(PLE_DIM, D_MODEL), hbm),
        (w_ple_gate.reshape(D_MODEL, D_MODEL), hbm),
        (ple_norm, _resident((1, D_MODEL))),
    ]
    out = pl.pallas_call(
        _block_kernel,
        grid=(nb + 1,),
        in_specs=[spec for _, spec in operands],
        out_specs=pl.BlockSpec((tb, D_MODEL), prev),
        out_shape=jax.ShapeDtypeStruct((seqlen, D_MODEL), jnp.float32),
        scratch_shapes=[
            pltpu.VMEM((IN_WIDTH // MXU_N, D_MODEL, MXU_N), bf16),
            pltpu.VMEM((SSM_WIDTH // MXU_N, SSM_WIDTH, MXU_N), bf16),
            pltpu.VMEM((D_MODEL // MXU_N, SSM_WIDTH, MXU_N), bf16),
            pltpu.VMEM((D_MODEL // MXU_N, CONV_WIDTH, MXU_N), bf16),
            pltpu.VMEM((D_MODEL // MXU_N, D_MODEL, MXU_N), bf16),
            pltpu.VMEM((D_MODEL // MXU_N, PLE_DIM, MXU_N), bf16),
            pltpu.VMEM((D_MODEL // MXU_N, D_MODEL, MXU_N), bf16),
            pltpu.VMEM((2, STAGE_ROWS, IN_WIDTH), jnp.float32),
            pltpu.SemaphoreType.DMA((2,)),
            pltpu.VMEM((N_TILES, 2 * LANES, TILE_STATE), jnp.bfloat16),
            pltpu.VMEM((N_TILES, TILE_STATE + 2 * LANES, 2 * LANES), jnp.bfloat16),
            pltpu.VMEM((2, SUBLANES, N_STATE), jnp.float32),
            pltpu.VMEM((8, SUBLANES, N_STATE), jnp.float32),
            pltpu.VMEM((2, SUBLANES, N_STATE), jnp.float32),
            pltpu.VMEM((D_MODEL // MXU_N, tb, MXU_N), jnp.bfloat16),
            pltpu.VMEM((N_TILES, tb, LANES), jnp.float32),
            pltpu.VMEM((2, N_TILES, tb // 2, LANES), jnp.float32),
            pltpu.VMEM((tb + SUBLANES, CONV_WIDTH), jnp.float32),
            pltpu.VMEM((N_TILES, tb // 2, TILE_STATE), jnp.float32),
            pltpu.VMEM((N_TILES, N_COLS, tb // 2, MXU_N), jnp.bfloat16),
            pltpu.VMEM((CONV_WIDTH // MXU_N, tb, MXU_N), jnp.bfloat16),
            pltpu.VMEM((tb, SSM_WIDTH), jnp.float32),
            pltpu.VMEM((tb, D_MODEL), jnp.float32),
            pltpu.VMEM((2, N_TILES, tb // 2, LANES), jnp.float32),
            pltpu.VMEM((N_TILES, tb, LANES), jnp.float32),
            pltpu.VMEM((SSM_WIDTH // MXU_N, tb, MXU_N), jnp.bfloat16),
            pltpu.VMEM((tb, D_MODEL), jnp.float32),
            pltpu.VMEM((tb, D_MODEL), jnp.float32),
            pltpu.VMEM((D_MODEL // MXU_N, tb, MXU_N), jnp.bfloat16),
            pltpu.VMEM((tb, D_MODEL), jnp.float32),
            pltpu.VMEM((D_MODEL // MXU_N, tb, MXU_N), jnp.bfloat16),
            pltpu.VMEM((tb, D_MODEL), jnp.float32),
        ],
        compiler_params=pltpu.CompilerParams(
            dimension_semantics=("arbitrary",),
            vmem_limit_bytes=VMEM_LIMIT_BYTES),
        name="hybrid_s5_shortconv_block",
    )(*[a for a, _ in operands])
    return out[None]
```

```python
import math

import jax
import jax.numpy as jnp
from jax import lax
from jax.experimental import pallas as pl
from jax.experimental.pallas import tpu as pltpu

D_MODEL = 1024
PLE_DIM = 256
SSM_WIDTH = 512
SSM_GROUP = 16
SSM_GROUPS = 32
SSM_STATE = 64
CONV_WIDTH = 1024
CONV_K = 3
RMS_EPS = 1e-6
IN_WIDTH = 7168

LANES = 128
MXU_N = 256
SUBLANES = 8
ROW_TILE = 16
GROUPS_PER_TILE = LANES // SSM_GROUP
N_TILES = SSM_WIDTH // LANES
HALF = GROUPS_PER_TILE * SSM_STATE
TILE_STATE = 2 * HALF
N_COLS = HALF // LANES
N_STATE = SSM_GROUPS * SSM_STATE
ROW_BLOCK = 384
SEG_PAIRS = ROW_BLOCK // SUBLANES // 2
STAGE_ROWS = 64
VMEM_LIMIT_BYTES = 58 * 1024 * 1024

O_U, O_GS, O_XC, O_BC, O_CC, O_GC, O_RS, O_RC = 0, 512, 1024, 2048, 3072, 4096, 5120, 6144


def _rms(v, gain):
    return v * lax.rsqrt(jnp.mean(v * v, axis=-1, keepdims=True) + RMS_EPS) * gain


def _sigmoid(v):
    return 1.0 / (1.0 + jnp.exp(-v))


def _gelu_tanh(v):
    c = math.sqrt(2.0 / math.pi)
    return v * (0.5 * (1.0 + jnp.tanh(c * (v + 0.044715 * (v * v * v)))))


def _cmul(ar, ai, xr, xi):
    return ar * xr - ai * xi, ar * xi + ai * xr


def _expand_tables(xb_ref, yc_ref, sk_ref, a_ref, wb_ref, wc_ref, arow_ref, seg_ref):
    f32 = jnp.float32
    bf16 = jnp.bfloat16
    gmask = GROUPS_PER_TILE - 1

    def iota(shape, dim):
        return lax.broadcasted_iota(jnp.int32, shape, dim)

    def onehot(cond):
        return jnp.where(cond, 1.0, 0.0).astype(bf16)

    r = iota((LANES, TILE_STATE), 0)
    c = iota((LANES, TILE_STATE), 1)
    rep_b = onehot(((c & (SSM_STATE - 1)) == (r & (SSM_STATE - 1))) & (((c >> 7) & 1) == (r >> 6)))
    r = iota((2 * LANES, TILE_STATE), 0)
    c = iota((2 * LANES, TILE_STATE), 1)
    own_b = ((r >> 4) & gmask) == (((c >> 8) << 1) | ((c >> 6) & 1))
    r = iota((2 * SSM_GROUP, 2 * LANES), 0)
    c = iota((2 * SSM_GROUP, 2 * LANES), 1)
    rep_c = onehot(((c >> 7) == (r >> 4)) & ((c & (SSM_GROUP - 1)) == (r & (SSM_GROUP - 1))))
    r = iota((TILE_STATE, 2 * LANES), 0)
    c = iota((TILE_STATE, 2 * LANES), 1)
    own_c = (((r >> 8) << 1) | ((r >> 6) & 1)) == ((c >> 4) & gmask)
    r = iota((2 * LANES, 2 * LANES), 0)
    c = iota((2 * LANES, 2 * LANES), 1)
    own_s = ((r >> 4) & gmask) == ((c >> 4) & gmask)
    for t in range(N_TILES):
        wide = jnp.dot(xb_ref[t].astype(bf16), rep_b, preferred_element_type=f32)
        wide = jnp.where(own_b, wide, 0.0).astype(bf16)
        for q in range(N_COLS):
            wb_ref[t, q] = wide[:, q * MXU_N:(q + 1) * MXU_N]
        wide = jnp.dot(yc_ref[t].astype(bf16), rep_c, preferred_element_type=f32)
        wc_ref[t, 0:TILE_STATE, :] = jnp.where(own_c, wide, 0.0).astype(bf16)
        wide = jnp.dot(sk_ref[t].astype(bf16), rep_c, preferred_element_type=f32)
        wc_ref[t, TILE_STATE:TILE_STATE + 2 * LANES, :] = jnp.where(own_s, wide, 0.0).astype(bf16)

    sq = [_cmul(a_ref[0:1, :], a_ref[1:2, :], a_ref[0:1, :], a_ref[1:2, :])]
    while len(sq) < SEG_PAIRS.bit_length():
        sq.append(_cmul(*sq[-1], *sq[-1]))
    a_seg = None
    for k in range(SEG_PAIRS.bit_length()):
        if (SEG_PAIRS >> k) & 1:
            a_seg = sq[k] if a_seg is None else _cmul(*a_seg, *sq[k])
    shape = (SUBLANES, N_STATE)
    arow_ref[0] = jnp.broadcast_to(sq[0][0], shape)
    arow_ref[1] = jnp.broadcast_to(sq[0][1], shape)
    row = iota(shape, 0)
    seg_ref[6] = jnp.broadcast_to(a_seg[0], shape)
    seg_ref[7] = jnp.broadcast_to(a_seg[1], shape)
    for k, shift in enumerate((1, 2, 4)):
        seg_ref[2 * k] = jnp.where(row >= shift, jnp.broadcast_to(a_seg[0], shape), 0.0)
        seg_ref[2 * k + 1] = jnp.where(row >= shift, jnp.broadcast_to(a_seg[1], shape), 0.0)
        a_seg = _cmul(*a_seg, *a_seg)


def _load_weights(pairs, stage_ref, sem_ref):
    chunk_rows, stage_cols = stage_ref.shape[1:]
    rounds = []
    for src_ref, dst_ref in pairs:
        rows, width = src_ref.shape
        assert rows % chunk_rows == 0 and width <= stage_cols
        per_round = stage_cols // width
        starts = list(range(0, rows, chunk_rows))
        for i in range(0, len(starts), per_round):
            rounds.append([(src_ref, dst_ref, r0, width, j * width)
                           for j, r0 in enumerate(starts[i:i + per_round])])

    def copies(k):
        return [pltpu.make_async_copy(src_ref.at[pl.ds(r0, chunk_rows), :],
                                      stage_ref.at[k % 2, :, pl.ds(c0, width)],
                                      sem_ref.at[k % 2])
                for src_ref, _, r0, width, c0 in rounds[k]]

    for cp in copies(0):
        cp.start()
    for k, chunks in enumerate(rounds):
        if k + 1 < len(rounds):
            for cp in copies(k + 1):
                cp.start()
        for cp in copies(k):
            cp.wait()
        for _, dst_ref, r0, width, c0 in chunks:
            for j in range(width // MXU_N):
                dst_ref[j, r0:r0 + chunk_rows, :] = (
                    stage_ref[k % 2, :, c0 + j * MXU_N:c0 + (j + 1) * MXU_N].astype(dst_ref.dtype))


def _block_kernel(*refs):
    last = pl.num_programs(0) - 1

    @pl.when(pl.program_id(0) == 0)
    def _():
        _first_step_setup(*refs)

    @pl.when(pl.program_id(0) < last)
    def _():
        _run_step(True, *refs)

    @pl.when(pl.program_id(0) == last)
    def _():
        _run_step(False, *refs)


def _first_step_setup(x_ref, p_ref, pre_ref, win_hbm, xb_ref, yc_ref, sk_ref, a_ref,
                      wglu_hbm, bglu_ref, wso_hbm, convw_ref, wco_hbm, wo_hbm, post_ref,
                      wple_hbm, wpg_hbm, plen_ref, out_ref,
                      win_ref, wglu_ref, wso_ref, wco_ref, wo_ref, wple_ref, wpg_ref, stage_ref,
                      sem_ref, wb_ref, wc_ref, arow_ref, seg_ref, state_ref, h_ref, un_ref, up_ref,
                      v_ref, sloc_ref, s_ref, z_ref, gs_ref, mg_ref,
                      yp_ref, y_ref, y2_ref, o_ref, x1_ref, x1b_ref, e_ref, mprev_ref, xprev_ref):
    tb = x_ref.shape[0]
    _load_weights([(wo_hbm, wo_ref), (win_hbm, win_ref), (wpg_hbm, wpg_ref),
                   (wple_hbm, wple_ref), (wco_hbm, wco_ref), (wglu_hbm, wglu_ref),
                   (wso_hbm, wso_ref)], stage_ref, sem_ref)
    _expand_tables(xb_ref, yc_ref, sk_ref, a_ref, wb_ref, wc_ref, arow_ref, seg_ref)
    state_ref[...] = jnp.zeros_like(state_ref)
    v_ref[:, tb:tb + SUBLANES, :] = jnp.zeros((CONV_WIDTH // MXU_N, SUBLANES, MXU_N), jnp.float32)
    mprev_ref[...] = jnp.zeros_like(mprev_ref)
    xprev_ref[...] = jnp.zeros_like(xprev_ref)


def _run_step(mixers, x_ref, p_ref, pre_ref, win_hbm, xb_ref, yc_ref, sk_ref, a_ref,
              wglu_hbm, bglu_ref, wso_hbm, convw_ref, wco_hbm, wo_hbm, post_ref,
              wple_hbm, wpg_hbm, plen_ref, out_ref,
              win_ref, wglu_ref, wso_ref, wco_ref, wo_ref, wple_ref, wpg_ref, stage_ref, sem_ref,
              wb_ref, wc_ref, arow_ref, seg_ref, state_ref, h_ref, un_ref, up_ref, v_ref,
              sloc_ref, s_ref, z_ref, gs_ref, mg_ref,
              yp_ref, y_ref, y2_ref, o_ref, x1_ref, x1b_ref, e_ref, mprev_ref, xprev_ref):
    tb = x_ref.shape[0]
    ns = tb // SUBLANES
    npair = ns // 2
    f32 = jnp.float32
    bf16 = jnp.bfloat16

    def cols(n):
        return slice(n * MXU_N, (n + 1) * MXU_N)

    def tile_dot(lhs_ref, w_ref, off, n):
        lhs = jnp.concatenate([lhs_ref[k] for k in range(lhs_ref.shape[0])], axis=1)
        return jnp.dot(lhs, w_ref[off // MXU_N + n], preferred_element_type=f32)

    def load_ktiles(src_ref, rows):
        return jnp.concatenate([src_ref[k, rows, :] for k in range(src_ref.shape[0])], axis=1)

    def store_ktiles(dst_ref, rows, val):
        for k in range(dst_ref.shape[0]):
            dst_ref[k, rows, :] = val[:, k * MXU_N:(k + 1) * MXU_N]

    def finish_project():
        for n in range(D_MODEL // MXU_N):
            o_ref[n] = tile_dot(mprev_ref, wo_ref, 0, n)

    def finish_residual():
        for r in range(0, tb, ROW_TILE):
            rows = slice(r, r + ROW_TILE)
            x1 = xprev_ref[rows, :] + _rms(load_ktiles(o_ref, rows), post_ref[...])
            store_ktiles(x1_ref, rows, x1)
            store_ktiles(x1b_ref, rows, x1.astype(bf16))

    def finish_embed(n):
        eg = tile_dot(x1b_ref, wpg_ref, 0, n)
        ei = jnp.dot(p_ref[...].astype(bf16), wple_ref[n], preferred_element_type=f32)
        e_ref[n] = ei * _sigmoid(eg)

    def finish_store():
        for r in range(0, tb, ROW_TILE):
            rows = slice(r, r + ROW_TILE)
            out_ref[rows, :] = load_ktiles(x1_ref, rows) + _rms(load_ktiles(e_ref, rows), plen_ref[...])

    if not mixers:
        finish_project()
        finish_residual()
        for n in range(D_MODEL // MXU_N):
            finish_embed(n)
        finish_store()
        return

    finish_project()
    finish_residual()
    for r in range(0, tb, ROW_TILE):
        rows = slice(r, r + ROW_TILE)
        xs = x_ref[rows, :]
        store_ktiles(h_ref, rows, _rms(xs, pre_ref[...]).astype(bf16))
        xprev_ref[rows, :] = xs

    for n in range(SSM_WIDTH // MXU_N):
        u = tile_dot(h_ref, win_ref, O_U, n)
        for c in range(MXU_N // LANES):
            un_ref[n * (MXU_N // LANES) + c] = u[:, c * LANES:(c + 1) * LANES]
    for t in range(N_TILES):
        for par in range(2):
            for j in range(SUBLANES):
                for mh in range(npair // SUBLANES):
                    up_ref[par, t, pl.ds(SUBLANES * SUBLANES * mh + j, SUBLANES,
                                         stride=SUBLANES), :] = (
                        un_ref[t, pl.ds(j * ns + 2 * SUBLANES * mh + par, SUBLANES, stride=2), :])

    v_ref[:, 0:SUBLANES, :] = v_ref[:, tb:tb + SUBLANES, :]
    for n in range(CONV_WIDTH // MXU_N):
        v_ref[n, SUBLANES:SUBLANES + tb, :] = (
            tile_dot(h_ref, win_ref, O_CC, n) * tile_dot(h_ref, win_ref, O_XC, n))

    srow = lax.broadcasted_iota(jnp.int32, (SUBLANES, LANES), 0)
    row0 = srow == 0

    def scan_tile(t):
        ub = jnp.concatenate([up_ref[0, t], up_ref[1, t]], axis=1).astype(bf16)
        for q in range(N_COLS):
            bu = jnp.dot(ub, wb_ref[t, q],
                         preferred_element_type=f32)
            re = slice(q * 2 * LANES, q * 2 * LANES + LANES)
            im = slice(q * 2 * LANES + LANES, (q + 1) * 2 * LANES)
            sl = slice(t * HALF + q * LANES, t * HALF + (q + 1) * LANES)
            ar = arow_ref[0, :, sl]
            ai = arow_ref[1, :, sl]
            sr = bu[0:SUBLANES, 0:LANES]
            si = bu[0:SUBLANES, LANES:2 * LANES]
            for m in range(1, npair):
                rows = slice(m * SUBLANES, (m + 1) * SUBLANES)
                sloc_ref[t, rows, re] = sr
                sloc_ref[t, rows, im] = si
                mr, mi = _cmul(ar, ai, sr, si)
                sr = mr + bu[rows, 0:LANES]
                si = mi + bu[rows, LANES:2 * LANES]
            gr = jnp.where(row0, state_ref[0, :, sl], pltpu.roll(sr, 1, 0))
            gi = jnp.where(row0, state_ref[1, :, sl], pltpu.roll(si, 1, 0))
            for k, shift in enumerate((1, 2, 4)):
                keep = srow >= shift
                fr, fi = _cmul(seg_ref[2 * k, :, sl], seg_ref[2 * k + 1, :, sl],
                               jnp.where(keep, pltpu.roll(gr, shift, 0), 0.0),
                               jnp.where(keep, pltpu.roll(gi, shift, 0), 0.0))
                gr, gi = gr + fr, gi + fi
            fr, fi = _cmul(seg_ref[6, :, sl], seg_ref[7, :, sl], gr, gi)
            nr, ni = fr + sr, fi + si
            state_ref[0, :, sl] = jnp.broadcast_to(nr[SUBLANES - 1:SUBLANES, :], nr.shape)
            state_ref[1, :, sl] = jnp.broadcast_to(ni[SUBLANES - 1:SUBLANES, :], ni.shape)
            fr, fi = gr, gi
            for m2 in range(npair // 2):
                parts_r, parts_i = [], []
                for m in (2 * m2, 2 * m2 + 1):
                    if m == 0:
                        parts_r.append(gr)
                        parts_i.append(gi)
                        continue
                    rows = slice(m * SUBLANES, (m + 1) * SUBLANES)
                    fr, fi = _cmul(ar, ai, fr, fi)
                    parts_r.append(sloc_ref[t, rows, re] + fr)
                    parts_i.append(sloc_ref[t, rows, im] + fi)
                rows2 = slice(2 * m2 * SUBLANES, (2 * m2 + 2) * SUBLANES)
                s_ref[t, q, rows2, 0:LANES] = jnp.concatenate(parts_r, axis=0).astype(bf16)
                s_ref[t, q, rows2, LANES:2 * LANES] = jnp.concatenate(parts_i, axis=0).astype(bf16)

    scan_tile(0)

    for n in range(CONV_WIDTH // MXU_N):
        bc = tile_dot(h_ref, win_ref, O_BC, n)
        gc = tile_dot(h_ref, win_ref, O_GC, n)
        conv = (convw_ref[0:1, cols(n)] * v_ref[n, SUBLANES - 2:SUBLANES - 2 + tb, :]
                + convw_ref[1:2, cols(n)] * v_ref[n, SUBLANES - 1:SUBLANES - 1 + tb, :]
                + convw_ref[2:3, cols(n)] * v_ref[n, SUBLANES:SUBLANES + tb, :])
        z_ref[n] = (bc * conv * (gc * _sigmoid(gc))).astype(bf16)
        if n == 1:
            scan_tile(1)

    for n in range(D_MODEL // MXU_N):
        finish_embed(n)
        if n == 1:
            scan_tile(2)

    for n in range(SSM_WIDTH // MXU_N):
        gs = tile_dot(h_ref, win_ref, O_GS, n)
        gs_ref[n] = gs * _sigmoid(gs)
    scan_tile(3)

    finish_store()

    def gated_b(n):
        rc = tile_dot(h_ref, win_ref, O_RC, n)
        yb = tile_dot(z_ref, wco_ref, 0, n)
        mg_ref[n] = _sigmoid(rc) * yb

    gated_b(0)
    gated_b(1)

    for t in range(N_TILES):
        lhs = jnp.concatenate(
            [s_ref[t, q] for q in range(N_COLS)]
            + [up_ref[0, t].astype(bf16), up_ref[1, t].astype(bf16)], axis=1)
        yy = jnp.dot(lhs, wc_ref[t], preferred_element_type=f32)
        yp_ref[0, t] = yy[:, 0:LANES]
        yp_ref[1, t] = yy[:, LANES:2 * LANES]
    gated_b(2)
    gated_b(3)
    for t in range(N_TILES):
        for par in range(2):
            for j in range(SUBLANES):
                for mh in range(npair // SUBLANES):
                    y_ref[t, pl.ds(j * ns + 2 * SUBLANES * mh + par, SUBLANES, stride=2), :] = (
                        _gelu_tanh(yp_ref[par, t, pl.ds(SUBLANES * SUBLANES * mh + j, SUBLANES,
                                                        stride=SUBLANES), :]))
    y = jnp.concatenate([y_ref[t] for t in range(N_TILES)], axis=1)
    yb16 = y.astype(bf16)
    for n in range(SSM_WIDTH // MXU_N):
        glu = jnp.dot(yb16, wglu_ref[n], preferred_element_type=f32)
        y2_ref[n] = (y[:, cols(n)] * _sigmoid(glu + bglu_ref[:, cols(n)])
                     * gs_ref[n]).astype(bf16)

    n_out = D_MODEL // MXU_N
    rs = [tile_dot(h_ref, win_ref, O_RS, n) for n in range(2)]
    for n in range(n_out):
        ya = tile_dot(y2_ref, wso_ref, 0, n)
        mprev_ref[n] = (_sigmoid(rs[n]) * ya + mg_ref[n]).astype(bf16)
        if n + 2 < n_out:
            rs.append(tile_dot(h_ref, win_ref, O_RS, n + 2))


def _ssm_params(a_re, a_im, log_step, b_re, b_im, c_re, c_im, d_skip):
    f32 = jnp.float32
    G, P, H, T, GT = SSM_GROUPS, SSM_STATE, SSM_GROUP, N_TILES, GROUPS_PER_TILE
    dt = jnp.exp(log_step.astype(f32))[:, None]
    mag = jnp.exp(a_re * dt)
    abar_r = mag * jnp.cos(a_im * dt)
    abar_i = mag * jnp.sin(a_im * dt)
    den = a_re * a_re + a_im * a_im
    nr = abar_r - 1.0
    fr = (nr * a_re + abar_i * a_im) / den
    fi = (abar_i * a_re - nr * a_im) / den
    bbar_r = fr[..., None] * b_re - fi[..., None] * b_im
    bbar_i = fr[..., None] * b_im + fi[..., None] * b_re
    ab_r, ab_i = _cmul(abar_r[..., None], abar_i[..., None], bbar_r, bbar_i)
    ca_r, ca_i = _cmul(abar_r[:, None, :], abar_i[:, None, :], c_re, c_im)
    caa_r, caa_i = _cmul(abar_r[:, None, :], abar_i[:, None, :], ca_r, ca_i)

    def re_prod(xr, xi, yr, yi):
        yr_t = jnp.swapaxes(yr, 1, 2)[:, None]
        yi_t = jnp.swapaxes(yi, 1, 2)[:, None]
        return jnp.sum(xr[:, :, None, :] * yr_t - xi[:, :, None, :] * yi_t, axis=-1)

    cb_d = (re_prod(c_re, c_im, bbar_r, bbar_i)
            + jnp.eye(H, dtype=f32) * d_skip.reshape(G, H, 1))
    cab = re_prod(ca_r, ca_i, bbar_r, bbar_i)

    xb = jnp.stack([jnp.stack([ab_r, ab_i]), jnp.stack([bbar_r, bbar_i])])
    xb = xb.reshape(2, 2, T, GT, P, H).transpose(2, 0, 3, 5, 1, 4).reshape(T, 2 * LANES, 2 * P)
    yc = jnp.stack([jnp.stack([ca_r, -ca_i]), jnp.stack([caa_r, -caa_i])])
    yc = yc.reshape(2, 2, T, N_COLS, 2, H, P).transpose(2, 3, 1, 4, 6, 0, 5)
    yc = yc.reshape(T, TILE_STATE, 2 * H)
    zero = jnp.zeros_like(cab)
    sk = jnp.stack([jnp.stack([cb_d, cab]), jnp.stack([zero, cb_d])])
    sk = sk.reshape(2, 2, T, GT, H, H).transpose(2, 0, 3, 5, 1, 4).reshape(T, 2 * LANES, 2 * H)
    a = jnp.stack([abar_r.reshape(N_STATE), abar_i.reshape(N_STATE)])
    bf16 = jnp.bfloat16
    return xb.astype(bf16), yc.astype(bf16), sk.astype(bf16), a


def _resident(shape):
    return pl.BlockSpec(shape, lambda i: (0,) * len(shape), pipeline_mode=pl.Buffered(1))


@jax.jit
def kernel(x, p, pre_norm, w_in, a_re, a_im, log_step, b_re, b_im, c_re, c_im, d_skip,
           w_glu, b_glu, w_ssm_out, conv_w, w_conv_out, w_o, post_norm,
           w_ple, w_ple_gate, ple_norm):
    bsz, seqlen, _ = x.shape
    depth = p.shape[0]
    assert bsz == 1 and depth == 1
    bf16 = jnp.bfloat16
    tb = ROW_BLOCK

    xb, yc, sk, a_bar = _ssm_params(
        a_re[0], a_im[0], log_step[0], b_re[0], b_im[0], c_re[0], c_im[0], d_skip[0])
    nb = pl.cdiv(seqlen, tb)
    cur = lambda i: (jnp.minimum(i, nb - 1), 0)
    prev = lambda i: (jnp.maximum(i - 1, 0), 0)
    hbm = pl.BlockSpec(memory_space=pl.ANY)
    operands = [
        (x[0], pl.BlockSpec((tb, D_MODEL), cur)),
        (p[0, 0], pl.BlockSpec((tb, PLE_DIM), prev)),
        (pre_norm, _resident((1, D_MODEL))),
        (w_in.reshape(D_MODEL, IN_WIDTH), hbm),
        (xb, _resident(xb.shape)),
        (yc, _resident(yc.shape)),
        (sk, _resident(sk.shape)),
        (a_bar, _resident(a_bar.shape)),
        (w_glu.reshape(SSM_WIDTH, SSM_WIDTH), hbm),
        (b_glu, _resident((1, SSM_WIDTH))),
        (w_ssm_out.reshape(SSM_WIDTH, D_MODEL), hbm),
        (conv_w[0], _resident((CONV_K, CONV_WIDTH))),
        (w_conv_out.reshape(CONV_WIDTH, D_MODEL), hbm),
        (w_o.reshape(D_MODEL, D_MODEL), hbm),
        (post_norm, _resident((1, D_MODEL))),
        (w_ple.reshape(PLE_DIM, D_MODEL), hbm),
        (w_ple_gate.reshape(D_MODEL, D_MODEL), hbm),
        (ple_norm, _resident((1, D_MODEL))),
    ]
    out = pl.pallas_call(
        _block_kernel,
        grid=(nb + 1,),
        in_specs=[spec for _, spec in operands],
        out_specs=pl.BlockSpec((tb, D_MODEL), prev),
        out_shape=jax.ShapeDtypeStruct((seqlen, D_MODEL), jnp.float32),
        scratch_shapes=[
            pltpu.VMEM((IN_WIDTH // MXU_N, D_MODEL, MXU_N), bf16),
            pltpu.VMEM((SSM_WIDTH // MXU_N, SSM_WIDTH, MXU_N), bf16),
            pltpu.VMEM((D_MODEL // MXU_N, SSM_WIDTH, MXU_N), bf16),
            pltpu.VMEM((D_MODEL // MXU_N, CONV_WIDTH, MXU_N), bf16),
            pltpu.VMEM((D_MODEL // MXU_N, D_MODEL, MXU_N), bf16),
            pltpu.VMEM((D_MODEL // MXU_N, PLE_DIM, MXU_N), bf16),
            pltpu.VMEM((D_MODEL // MXU_N, D_MODEL, MXU_N), bf16),
            pltpu.VMEM((2, STAGE_ROWS, IN_WIDTH), jnp.float32),
            pltpu.SemaphoreType.DMA((2,)),
            pltpu.VMEM((N_TILES, N_COLS, 2 * LANES, MXU_N), jnp.bfloat16),
            pltpu.VMEM((N_TILES, TILE_STATE + 2 * LANES, 2 * LANES), jnp.bfloat16),
            pltpu.VMEM((2, SUBLANES, N_STATE), jnp.float32),
            pltpu.VMEM((8, SUBLANES, N_STATE), jnp.float32),
            pltpu.VMEM((2, SUBLANES, N_STATE), jnp.float32),
            pltpu.VMEM((D_MODEL // MXU_N, tb, MXU_N), jnp.bfloat16),
            pltpu.VMEM((N_TILES, tb, LANES), jnp.float32),
            pltpu.VMEM((2, N_TILES, tb // 2, LANES), jnp.float32),
            pltpu.VMEM((CONV_WIDTH // MXU_N, tb + SUBLANES, MXU_N), jnp.float32),
            pltpu.VMEM((N_TILES, tb // 2, TILE_STATE), jnp.float32),
            pltpu.VMEM((N_TILES, N_COLS, tb // 2, MXU_N), jnp.bfloat16),
            pltpu.VMEM((CONV_WIDTH // MXU_N, tb, MXU_N), jnp.bfloat16),
            pltpu.VMEM((SSM_WIDTH // MXU_N, tb, MXU_N), jnp.float32),
            pltpu.VMEM((D_MODEL // MXU_N, tb, MXU_N), jnp.float32),
            pltpu.VMEM((2, N_TILES, tb // 2, LANES), jnp.float32),
            pltpu.VMEM((N_TILES, tb, LANES), jnp.float32),
            pltpu.VMEM((SSM_WIDTH // MXU_N, tb, MXU_N), jnp.bfloat16),
            pltpu.VMEM((D_MODEL // MXU_N, tb, MXU_N), jnp.float32),
            pltpu.VMEM((D_MODEL // MXU_N, tb, MXU_N), jnp.float32),
            pltpu.VMEM((D_MODEL // MXU_N, tb, MXU_N), jnp.bfloat16),
            pltpu.VMEM((D_MODEL // MXU_N, tb, MXU_N), jnp.float32),
            pltpu.VMEM((D_MODEL // MXU_N, tb, MXU_N), jnp.bfloat16),
            pltpu.VMEM((tb, D_MODEL), jnp.float32),
        ],
        compiler_params=pltpu.CompilerParams(
            dimension_semantics=("arbitrary",),
            vmem_limit_bytes=VMEM_LIMIT_BYTES),
        name="hybrid_s5_shortconv_block",
    )(*[a for a, _ in operands])
    return out[None]
```

```python
import math

import jax
import jax.numpy as jnp
from jax import lax
from jax.experimental import pallas as pl
from jax.experimental.pallas import tpu as pltpu

D_MODEL = 1024
PLE_DIM = 256
SSM_WIDTH = 512
SSM_GROUP = 16
SSM_GROUPS = 32
SSM_STATE = 64
CONV_WIDTH = 1024
CONV_K = 3
RMS_EPS = 1e-6
IN_WIDTH = 7168

LANES = 128
MXU_N = 256
SUBLANES = 8
ROW_TILE = 16
GROUPS_PER_TILE = LANES // SSM_GROUP
N_TILES = SSM_WIDTH // LANES
HALF = GROUPS_PER_TILE * SSM_STATE
TILE_STATE = 2 * HALF
N_COLS = HALF // LANES
N_STATE = SSM_GROUPS * SSM_STATE
ROW_BLOCK = 384
SEG_PAIRS = ROW_BLOCK // SUBLANES // 2
STAGE_ROWS = 64
VMEM_LIMIT_BYTES = 58 * 1024 * 1024

O_U, O_GS, O_XC, O_BC, O_CC, O_GC, O_RS, O_RC = 0, 512, 1024, 2048, 3072, 4096, 5120, 6144


def _rms(v, gain):
    return v * lax.rsqrt(jnp.mean(v * v, axis=-1, keepdims=True) + RMS_EPS) * gain


def _sigmoid(v):
    return 1.0 / (1.0 + jnp.exp(-v))


def _gelu_tanh(v):
    c = math.sqrt(2.0 / math.pi)
    return v * (0.5 * (1.0 + jnp.tanh(c * (v + 0.044715 * (v * v * v)))))


def _cmul(ar, ai, xr, xi):
    return ar * xr - ai * xi, ar * xi + ai * xr


def _expand_tables(xb_ref, yc_ref, sk_ref, a_ref, wb_ref, wc_ref, arow_ref, seg_ref):
    f32 = jnp.float32
    bf16 = jnp.bfloat16
    gmask = GROUPS_PER_TILE - 1

    def iota(shape, dim):
        return lax.broadcasted_iota(jnp.int32, shape, dim)

    def onehot(cond):
        return jnp.where(cond, 1.0, 0.0).astype(bf16)

    r = iota((LANES, TILE_STATE), 0)
    c = iota((LANES, TILE_STATE), 1)
    rep_b = onehot(((c & (SSM_STATE - 1)) == (r & (SSM_STATE - 1))) & (((c >> 7) & 1) == (r >> 6)))
    r = iota((2 * LANES, TILE_STATE), 0)
    c = iota((2 * LANES, TILE_STATE), 1)
    own_b = ((r >> 4) & gmask) == (((c >> 8) << 1) | ((c >> 6) & 1))
    r = iota((2 * SSM_GROUP, 2 * LANES), 0)
    c = iota((2 * SSM_GROUP, 2 * LANES), 1)
    rep_c = onehot(((c >> 7) == (r >> 4)) & ((c & (SSM_GROUP - 1)) == (r & (SSM_GROUP - 1))))
    r = iota((TILE_STATE, 2 * LANES), 0)
    c = iota((TILE_STATE, 2 * LANES), 1)
    own_c = (((r >> 8) << 1) | ((r >> 6) & 1)) == ((c >> 4) & gmask)
    r = iota((2 * LANES, 2 * LANES), 0)
    c = iota((2 * LANES, 2 * LANES), 1)
    own_s = ((r >> 4) & gmask) == ((c >> 4) & gmask)
    for t in range(N_TILES):
        wide = jnp.dot(xb_ref[t].astype(bf16), rep_b, preferred_element_type=f32)
        wide = jnp.where(own_b, wide, 0.0).astype(bf16)
        for q in range(N_COLS):
            wb_ref[t, q] = wide[:, q * MXU_N:(q + 1) * MXU_N]
        wide = jnp.dot(yc_ref[t].astype(bf16), rep_c, preferred_element_type=f32)
        wc_ref[t, 0:TILE_STATE, :] = jnp.where(own_c, wide, 0.0).astype(bf16)
        wide = jnp.dot(sk_ref[t].astype(bf16), rep_c, preferred_element_type=f32)
        wc_ref[t, TILE_STATE:TILE_STATE + 2 * LANES, :] = jnp.where(own_s, wide, 0.0).astype(bf16)

    sq = [_cmul(a_ref[0:1, :], a_ref[1:2, :], a_ref[0:1, :], a_ref[1:2, :])]
    while len(sq) < SEG_PAIRS.bit_length():
        sq.append(_cmul(*sq[-1], *sq[-1]))
    a_seg = None
    for k in range(SEG_PAIRS.bit_length()):
        if (SEG_PAIRS >> k) & 1:
            a_seg = sq[k] if a_seg is None else _cmul(*a_seg, *sq[k])
    shape = (SUBLANES, N_STATE)
    arow_ref[0] = jnp.broadcast_to(sq[0][0], shape)
    arow_ref[1] = jnp.broadcast_to(sq[0][1], shape)
    row = iota(shape, 0)
    seg_ref[6] = jnp.broadcast_to(a_seg[0], shape)
    seg_ref[7] = jnp.broadcast_to(a_seg[1], shape)
    for k, shift in enumerate((1, 2, 4)):
        seg_ref[2 * k] = jnp.where(row >= shift, jnp.broadcast_to(a_seg[0], shape), 0.0)
        seg_ref[2 * k + 1] = jnp.where(row >= shift, jnp.broadcast_to(a_seg[1], shape), 0.0)
        a_seg = _cmul(*a_seg, *a_seg)


def _load_weights(pairs, stage_ref, sem_ref):
    chunk_rows, stage_cols = stage_ref.shape[1:]
    rounds = []
    for src_ref, dst_ref in pairs:
        rows, width = src_ref.shape
        assert rows % chunk_rows == 0 and width <= stage_cols
        per_round = stage_cols // width
        starts = list(range(0, rows, chunk_rows))
        for i in range(0, len(starts), per_round):
            rounds.append([(src_ref, dst_ref, r0, width, j * width)
                           for j, r0 in enumerate(starts[i:i + per_round])])

    def copies(k):
        return [pltpu.make_async_copy(src_ref.at[pl.ds(r0, chunk_rows), :],
                                      stage_ref.at[k % 2, :, pl.ds(c0, width)],
                                      sem_ref.at[k % 2])
                for src_ref, _, r0, width, c0 in rounds[k]]

    for cp in copies(0):
        cp.start()
    for k, chunks in enumerate(rounds):
        if k + 1 < len(rounds):
            for cp in copies(k + 1):
                cp.start()
        for cp in copies(k):
            cp.wait()
        for _, dst_ref, r0, width, c0 in chunks:
            for j in range(width // MXU_N):
                dst_ref[j, r0:r0 + chunk_rows, :] = (
                    stage_ref[k % 2, :, c0 + j * MXU_N:c0 + (j + 1) * MXU_N].astype(dst_ref.dtype))


def _block_kernel(*refs):
    last = pl.num_programs(0) - 1

    @pl.when(pl.program_id(0) == 0)
    def _():
        _first_step_setup(*refs)

    @pl.when(pl.program_id(0) < last)
    def _():
        _run_step(True, *refs)

    @pl.when(pl.program_id(0) == last)
    def _():
        _run_step(False, *refs)


def _first_step_setup(x_ref, p_ref, pre_ref, win_hbm, xb_ref, yc_ref, sk_ref, a_ref,
                      wglu_hbm, bglu_ref, wso_hbm, convw_ref, wco_hbm, wo_hbm, post_ref,
                      wple_hbm, wpg_hbm, plen_ref, out_ref,
                      win_ref, wglu_ref, wso_ref, wco_ref, wo_ref, wple_ref, wpg_ref, stage_ref,
                      sem_ref, wb_ref, wc_ref, arow_ref, seg_ref, state_ref, h_ref, un_ref, up_ref,
                      v_ref, sloc_ref, s_ref, z_ref, gs_ref, mg_ref,
                      yp_ref, y_ref, y2_ref, o_ref, x1_ref, x1b_ref, e_ref, mprev_ref, xprev_ref):
    tb = x_ref.shape[0]
    _load_weights([(wo_hbm, wo_ref), (win_hbm, win_ref), (wpg_hbm, wpg_ref),
                   (wple_hbm, wple_ref), (wco_hbm, wco_ref), (wglu_hbm, wglu_ref),
                   (wso_hbm, wso_ref)], stage_ref, sem_ref)
    _expand_tables(xb_ref, yc_ref, sk_ref, a_ref, wb_ref, wc_ref, arow_ref, seg_ref)
    state_ref[...] = jnp.zeros_like(state_ref)
    v_ref[:, tb:tb + SUBLANES, :] = jnp.zeros((CONV_WIDTH // MXU_N, SUBLANES, MXU_N), jnp.float32)
    mprev_ref[...] = jnp.zeros_like(mprev_ref)
    xprev_ref[...] = jnp.zeros_like(xprev_ref)


def _run_step(mixers, x_ref, p_ref, pre_ref, win_hbm, xb_ref, yc_ref, sk_ref, a_ref,
              wglu_hbm, bglu_ref, wso_hbm, convw_ref, wco_hbm, wo_hbm, post_ref,
              wple_hbm, wpg_hbm, plen_ref, out_ref,
              win_ref, wglu_ref, wso_ref, wco_ref, wo_ref, wple_ref, wpg_ref, stage_ref, sem_ref,
              wb_ref, wc_ref, arow_ref, seg_ref, state_ref, h_ref, un_ref, up_ref, v_ref,
              sloc_ref, s_ref, z_ref, gs_ref, mg_ref,
              yp_ref, y_ref, y2_ref, o_ref, x1_ref, x1b_ref, e_ref, mprev_ref, xprev_ref):
    tb = x_ref.shape[0]
    ns = tb // SUBLANES
    npair = ns // 2
    f32 = jnp.float32
    bf16 = jnp.bfloat16

    def cols(n):
        return slice(n * MXU_N, (n + 1) * MXU_N)

    def tile_dot(lhs_ref, w_ref, off, n):
        lhs = jnp.concatenate([lhs_ref[k] for k in range(lhs_ref.shape[0])], axis=1)
        return jnp.dot(lhs, w_ref[off // MXU_N + n], preferred_element_type=f32)

    def load_ktiles(src_ref, rows):
        return jnp.concatenate([src_ref[k, rows, :] for k in range(src_ref.shape[0])], axis=1)

    def store_ktiles(dst_ref, rows, val):
        for k in range(dst_ref.shape[0]):
            dst_ref[k, rows, :] = val[:, k * MXU_N:(k + 1) * MXU_N]

    def finish_project():
        for n in range(D_MODEL // MXU_N):
            o_ref[n] = tile_dot(mprev_ref, wo_ref, 0, n)

    def finish_residual():
        for r in range(0, tb, ROW_TILE):
            rows = slice(r, r + ROW_TILE)
            x1 = load_ktiles(xprev_ref, rows) + _rms(load_ktiles(o_ref, rows), post_ref[...])
            store_ktiles(x1_ref, rows, x1)
            store_ktiles(x1b_ref, rows, x1.astype(bf16))

    def finish_embed(n):
        eg = tile_dot(x1b_ref, wpg_ref, 0, n)
        ei = jnp.dot(p_ref[...].astype(bf16), wple_ref[n], preferred_element_type=f32)
        e_ref[n] = ei * _sigmoid(eg)

    def finish_store():
        for r in range(0, tb, ROW_TILE):
            rows = slice(r, r + ROW_TILE)
            out_ref[rows, :] = load_ktiles(x1_ref, rows) + _rms(load_ktiles(e_ref, rows), plen_ref[...])

    if not mixers:
        finish_project()
        finish_residual()
        for n in range(D_MODEL // MXU_N):
            finish_embed(n)
        finish_store()
        return

    finish_project()
    finish_residual()
    for r in range(0, tb, ROW_TILE):
        rows = slice(r, r + ROW_TILE)
        xs = x_ref[rows, :]
        store_ktiles(h_ref, rows, _rms(xs, pre_ref[...]).astype(bf16))
        store_ktiles(xprev_ref, rows, xs)

    for n in range(SSM_WIDTH // MXU_N):
        u = tile_dot(h_ref, win_ref, O_U, n)
        for c in range(MXU_N // LANES):
            un_ref[n * (MXU_N // LANES) + c] = u[:, c * LANES:(c + 1) * LANES]
    for t in range(N_TILES):
        for par in range(2):
            for j in range(SUBLANES):
                for mh in range(npair // SUBLANES):
                    up_ref[par, t, pl.ds(SUBLANES * SUBLANES * mh + j, SUBLANES,
                                         stride=SUBLANES), :] = (
                        un_ref[t, pl.ds(j * ns + 2 * SUBLANES * mh + par, SUBLANES, stride=2), :])

    v_ref[:, 0:SUBLANES, :] = v_ref[:, tb:tb + SUBLANES, :]
    for n in range(CONV_WIDTH // MXU_N):
        v_ref[n, SUBLANES:SUBLANES + tb, :] = (
            tile_dot(h_ref, win_ref, O_CC, n) * tile_dot(h_ref, win_ref, O_XC, n))

    srow = lax.broadcasted_iota(jnp.int32, (SUBLANES, LANES), 0)
    row0 = srow == 0

    def scan_tile(t):
        ub = jnp.concatenate([up_ref[0, t], up_ref[1, t]], axis=1).astype(bf16)
        for q in range(N_COLS):
            bu = jnp.dot(ub, wb_ref[t, q],
                         preferred_element_type=f32)
            sl = slice(t * HALF + q * LANES, t * HALF + (q + 1) * LANES)
            ar = arow_ref[0, :, sl]
            ai = arow_ref[1, :, sl]
            sr = bu[0:SUBLANES, 0:LANES]
            si = bu[0:SUBLANES, LANES:2 * LANES]
            for m in range(1, npair):
                rows = slice(m * SUBLANES, (m + 1) * SUBLANES)
                sloc_ref[t, q, rows, 0:LANES] = sr
                sloc_ref[t, q, rows, LANES:2 * LANES] = si
                mr, mi = _cmul(ar, ai, sr, si)
                sr = mr + bu[rows, 0:LANES]
                si = mi + bu[rows, LANES:2 * LANES]
            gr = jnp.where(row0, state_ref[0, :, sl], pltpu.roll(sr, 1, 0))
            gi = jnp.where(row0, state_ref[1, :, sl], pltpu.roll(si, 1, 0))
            for k, shift in enumerate((1, 2, 4)):
                keep = srow >= shift
                fr, fi = _cmul(seg_ref[2 * k, :, sl], seg_ref[2 * k + 1, :, sl],
                               jnp.where(keep, pltpu.roll(gr, shift, 0), 0.0),
                               jnp.where(keep, pltpu.roll(gi, shift, 0), 0.0))
                gr, gi = gr + fr, gi + fi
            fr, fi = _cmul(seg_ref[6, :, sl], seg_ref[7, :, sl], gr, gi)
            nr, ni = fr + sr, fi + si
            state_ref[0, :, sl] = jnp.broadcast_to(nr[SUBLANES - 1:SUBLANES, :], nr.shape)
            state_ref[1, :, sl] = jnp.broadcast_to(ni[SUBLANES - 1:SUBLANES, :], ni.shape)
            fr, fi = gr, gi
            for m2 in range(npair // 2):
                parts_r, parts_i = [], []
                for m in (2 * m2, 2 * m2 + 1):
                    if m == 0:
                        parts_r.append(gr)
                        parts_i.append(gi)
                        continue
                    rows = slice(m * SUBLANES, (m + 1) * SUBLANES)
                    fr, fi = _cmul(ar, ai, fr, fi)
                    parts_r.append(sloc_ref[t, q, rows, 0:LANES] + fr)
                    parts_i.append(sloc_ref[t, q, rows, LANES:2 * LANES] + fi)
                rows2 = slice(2 * m2 * SUBLANES, (2 * m2 + 2) * SUBLANES)
                s_ref[t, q, rows2, 0:LANES] = jnp.concatenate(parts_r, axis=0).astype(bf16)
                s_ref[t, q, rows2, LANES:2 * LANES] = jnp.concatenate(parts_i, axis=0).astype(bf16)

    scan_tile(0)

    for n in range(CONV_WIDTH // MXU_N):
        bc = tile_dot(h_ref, win_ref, O_BC, n)
        gc = tile_dot(h_ref, win_ref, O_GC, n)
        conv = (convw_ref[0:1, cols(n)] * v_ref[n, SUBLANES - 2:SUBLANES - 2 + tb, :]
                + convw_ref[1:2, cols(n)] * v_ref[n, SUBLANES - 1:SUBLANES - 1 + tb, :]
                + convw_ref[2:3, cols(n)] * v_ref[n, SUBLANES:SUBLANES + tb, :])
        z_ref[n] = (bc * conv * (gc * _sigmoid(gc))).astype(bf16)
        if n == 1:
            scan_tile(1)

    for n in range(D_MODEL // MXU_N):
        finish_embed(n)
        if n == 1:
            scan_tile(2)

    for n in range(SSM_WIDTH // MXU_N):
        gs = tile_dot(h_ref, win_ref, O_GS, n)
        gs_ref[n] = gs * _sigmoid(gs)
    scan_tile(3)

    finish_store()

    def gated_b(n):
        rc = tile_dot(h_ref, win_ref, O_RC, n)
        yb = tile_dot(z_ref, wco_ref, 0, n)
        mg_ref[n] = _sigmoid(rc) * yb

    gated_b(0)
    gated_b(1)

    for t in range(N_TILES):
        lhs = jnp.concatenate(
            [s_ref[t, q] for q in range(N_COLS)]
            + [up_ref[0, t].astype(bf16), up_ref[1, t].astype(bf16)], axis=1)
        yy = jnp.dot(lhs, wc_ref[t], preferred_element_type=f32)
        yp_ref[0, t] = yy[:, 0:LANES]
        yp_ref[1, t] = yy[:, LANES:2 * LANES]
    gated_b(2)
    gated_b(3)
    for t in range(N_TILES):
        for par in range(2):
            for j in range(SUBLANES):
                for mh in range(npair // SUBLANES):
                    y_ref[t, pl.ds(j * ns + 2 * SUBLANES * mh + par, SUBLANES, stride=2), :] = (
                        _gelu_tanh(yp_ref[par, t, pl.ds(SUBLANES * SUBLANES * mh + j, SUBLANES,
                                                        stride=SUBLANES), :]))
    y = jnp.concatenate([y_ref[t] for t in range(N_TILES)], axis=1)
    yb16 = y.astype(bf16)
    for n in range(SSM_WIDTH // MXU_N):
        glu = jnp.dot(yb16, wglu_ref[n], preferred_element_type=f32)
        y2_ref[n] = (y[:, cols(n)] * _sigmoid(glu + bglu_ref[:, cols(n)])
                     * gs_ref[n]).astype(bf16)

    n_out = D_MODEL // MXU_N
    rs = [tile_dot(h_ref, win_ref, O_RS, n) for n in range(2)]
    for n in range(n_out):
        ya = tile_dot(y2_ref, wso_ref, 0, n)
        mprev_ref[n] = (_sigmoid(rs[n]) * ya + mg_ref[n]).astype(bf16)
        if n + 2 < n_out:
            rs.append(tile_dot(h_ref, win_ref, O_RS, n + 2))


def _ssm_params(a_re, a_im, log_step, b_re, b_im, c_re, c_im, d_skip):
    f32 = jnp.float32
    G, P, H, T, GT = SSM_GROUPS, SSM_STATE, SSM_GROUP, N_TILES, GROUPS_PER_TILE
    dt = jnp.exp(log_step.astype(f32))[:, None]
    mag = jnp.exp(a_re * dt)
    abar_r = mag * jnp.cos(a_im * dt)
    abar_i = mag * jnp.sin(a_im * dt)
    den = a_re * a_re + a_im * a_im
    nr = abar_r - 1.0
    fr = (nr * a_re + abar_i * a_im) / den
    fi = (abar_i * a_re - nr * a_im) / den
    bbar_r = fr[..., None] * b_re - fi[..., None] * b_im
    bbar_i = fr[..., None] * b_im + fi[..., None] * b_re
    ab_r, ab_i = _cmul(abar_r[..., None], abar_i[..., None], bbar_r, bbar_i)
    ca_r, ca_i = _cmul(abar_r[:, None, :], abar_i[:, None, :], c_re, c_im)
    caa_r, caa_i = _cmul(abar_r[:, None, :], abar_i[:, None, :], ca_r, ca_i)

    def re_prod(xr, xi, yr, yi):
        yr_t = jnp.swapaxes(yr, 1, 2)[:, None]
        yi_t = jnp.swapaxes(yi, 1, 2)[:, None]
        return jnp.sum(xr[:, :, None, :] * yr_t - xi[:, :, None, :] * yi_t, axis=-1)

    cb_d = (re_prod(c_re, c_im, bbar_r, bbar_i)
            + jnp.eye(H, dtype=f32) * d_skip.reshape(G, H, 1))
    cab = re_prod(ca_r, ca_i, bbar_r, bbar_i)

    xb = jnp.stack([jnp.stack([ab_r, ab_i]), jnp.stack([bbar_r, bbar_i])])
    xb = xb.reshape(2, 2, T, GT, P, H).transpose(2, 0, 3, 5, 1, 4).reshape(T, 2 * LANES, 2 * P)
    yc = jnp.stack([jnp.stack([ca_r, -ca_i]), jnp.stack([caa_r, -caa_i])])
    yc = yc.reshape(2, 2, T, N_COLS, 2, H, P).transpose(2, 3, 1, 4, 6, 0, 5)
    yc = yc.reshape(T, TILE_STATE, 2 * H)
    zero = jnp.zeros_like(cab)
    sk = jnp.stack([jnp.stack([cb_d, cab]), jnp.stack([zero, cb_d])])
    sk = sk.reshape(2, 2, T, GT, H, H).transpose(2, 0, 3, 5, 1, 4).reshape(T, 2 * LANES, 2 * H)
    a = jnp.stack([abar_r.reshape(N_STATE), abar_i.reshape(N_STATE)])
    bf16 = jnp.bfloat16
    return xb.astype(bf16), yc.astype(bf16), sk.astype(bf16), a


def _resident(shape):
    return pl.BlockSpec(shape, lambda i: (0,) * len(shape), pipeline_mode=pl.Buffered(1))


@jax.jit
def kernel(x, p, pre_norm, w_in, a_re, a_im, log_step, b_re, b_im, c_re, c_im, d_skip,
           w_glu, b_glu, w_ssm_out, conv_w, w_conv_out, w_o, post_norm,
           w_ple, w_ple_gate, ple_norm):
    bsz, seqlen, _ = x.shape
    depth = p.shape[0]
    assert bsz == 1 and depth == 1
    bf16 = jnp.bfloat16
    tb = ROW_BLOCK

    xb, yc, sk, a_bar = _ssm_params(
        a_re[0], a_im[0], log_step[0], b_re[0], b_im[0], c_re[0], c_im[0], d_skip[0])
    nb = pl.cdiv(seqlen, tb)
    cur = lambda i: (jnp.minimum(i, nb - 1), 0)
    prev = lambda i: (jnp.maximum(i - 1, 0), 0)
    hbm = pl.BlockSpec(memory_space=pl.ANY)
    operands = [
        (x[0], pl.BlockSpec((tb, D_MODEL), cur)),
        (p[0, 0], pl.BlockSpec((tb, PLE_DIM), prev)),
        (pre_norm, _resident((1, D_MODEL))),
        (w_in.reshape(D_MODEL, IN_WIDTH), hbm),
        (xb, _resident(xb.shape)),
        (yc, _resident(yc.shape)),
        (sk, _resident(sk.shape)),
        (a_bar, _resident(a_bar.shape)),
        (w_glu.reshape(SSM_WIDTH, SSM_WIDTH), hbm),
        (b_glu, _resident((1, SSM_WIDTH))),
        (w_ssm_out.reshape(SSM_WIDTH, D_MODEL), hbm),
        (conv_w[0], _resident((CONV_K, CONV_WIDTH))),
        (w_conv_out.reshape(CONV_WIDTH, D_MODEL), hbm),
        (w_o.reshape(D_MODEL, D_MODEL), hbm),
        (post_norm, _resident((1, D_MODEL))),
        (w_ple.reshape(PLE_DIM, D_MODEL), hbm),
        (w_ple_gate.reshape(D_MODEL, D_MODEL), hbm),
        (ple_norm, _resident((1, D_MODEL))),
    ]
    out = pl.pallas_call(
        _block_kernel,
        grid=(nb + 1,),
        in_specs=[spec for _, spec in operands],
        out_specs=pl.BlockSpec((tb, D_MODEL), prev),
        out_shape=jax.ShapeDtypeStruct((seqlen, D_MODEL), jnp.float32),
        scratch_shapes=[
            pltpu.VMEM((IN_WIDTH // MXU_N, D_MODEL, MXU_N), bf16),
            pltpu.VMEM((SSM_WIDTH // MXU_N, SSM_WIDTH, MXU_N), bf16),
            pltpu.VMEM((D_MODEL // MXU_N, SSM_WIDTH, MXU_N), bf16),
            pltpu.VMEM((D_MODEL // MXU_N, CONV_WIDTH, MXU_N), bf16),
            pltpu.VMEM((D_MODEL // MXU_N, D_MODEL, MXU_N), bf16),
            pltpu.VMEM((D_MODEL // MXU_N, PLE_DIM, MXU_N), bf16),
            pltpu.VMEM((D_MODEL // MXU_N, D_MODEL, MXU_N), bf16),
            pltpu.VMEM((2, STAGE_ROWS, IN_WIDTH), jnp.float32),
            pltpu.SemaphoreType.DMA((2,)),
            pltpu.VMEM((N_TILES, N_COLS, 2 * LANES, MXU_N), jnp.bfloat16),
            pltpu.VMEM((N_TILES, TILE_STATE + 2 * LANES, 2 * LANES), jnp.bfloat16),
            pltpu.VMEM((2, SUBLANES, N_STATE), jnp.float32),
            pltpu.VMEM((8, SUBLANES, N_STATE), jnp.float32),
            pltpu.VMEM((2, SUBLANES, N_STATE), jnp.float32),
            pltpu.VMEM((D_MODEL // MXU_N, tb, MXU_N), jnp.bfloat16),
            pltpu.VMEM((N_TILES, tb, LANES), jnp.float32),
            pltpu.VMEM((2, N_TILES, tb // 2, LANES), jnp.float32),
            pltpu.VMEM((CONV_WIDTH // MXU_N, tb + SUBLANES, MXU_N), jnp.float32),
            pltpu.VMEM((N_TILES, N_COLS, tb // 2, MXU_N), jnp.float32),
            pltpu.VMEM((N_TILES, N_COLS, tb // 2, MXU_N), jnp.bfloat16),
            pltpu.VMEM((CONV_WIDTH // MXU_N, tb, MXU_N), jnp.bfloat16),
            pltpu.VMEM((SSM_WIDTH // MXU_N, tb, MXU_N), jnp.float32),
            pltpu.VMEM((D_MODEL // MXU_N, tb, MXU_N), jnp.float32),
            pltpu.VMEM((2, N_TILES, tb // 2, LANES), jnp.float32),
            pltpu.VMEM((N_TILES, tb, LANES), jnp.float32),
            pltpu.VMEM((SSM_WIDTH // MXU_N, tb, MXU_N), jnp.bfloat16),
            pltpu.VMEM((D_MODEL // MXU_N, tb, MXU_N), jnp.float32),
            pltpu.VMEM((D_MODEL // MXU_N, tb, MXU_N), jnp.float32),
            pltpu.VMEM((D_MODEL // MXU_N, tb, MXU_N), jnp.bfloat16),
            pltpu.VMEM((D_MODEL // MXU_N, tb, MXU_N), jnp.float32),
            pltpu.VMEM((D_MODEL // MXU_N, tb, MXU_N), jnp.bfloat16),
            pltpu.VMEM((D_MODEL // MXU_N, tb, MXU_N), jnp.float32),
        ],
        compiler_params=pltpu.CompilerParams(
            dimension_semantics=("arbitrary",),
            vmem_limit_bytes=VMEM_LIMIT_BYTES),
        name="hybrid_s5_shortconv_block",
    )(*[a for a, _ in operands])
    return out[None]
```

```python
import math

import jax
import jax.numpy as jnp
from jax import lax
from jax.experimental import pallas as pl
from jax.experimental.pallas import tpu as pltpu

D_MODEL = 1024
PLE_DIM = 256
SSM_WIDTH = 512
SSM_GROUP = 16
SSM_GROUPS = 32
SSM_STATE = 64
CONV_WIDTH = 1024
CONV_K = 3
RMS_EPS = 1e-6
IN_WIDTH = 7168

LANES = 128
MXU_N = 256
SUBLANES = 8
ROW_TILE = 16
GROUPS_PER_TILE = LANES // SSM_GROUP
N_TILES = SSM_WIDTH // LANES
HALF = GROUPS_PER_TILE * SSM_STATE
TILE_STATE = 2 * HALF
N_COLS = HALF // LANES
N_STATE = SSM_GROUPS * SSM_STATE
ROW_BLOCK = 384
SEG_PAIRS = ROW_BLOCK // SUBLANES // 2
STAGE_ROWS = 64
VMEM_LIMIT_BYTES = 58 * 1024 * 1024

O_U, O_GS, O_XC, O_BC, O_CC, O_GC, O_RS, O_RC = 0, 512, 1024, 2048, 3072, 4096, 5120, 6144


def _rms(v, gain):
    return v * lax.rsqrt(jnp.mean(v * v, axis=-1, keepdims=True) + RMS_EPS) * gain


def _sigmoid(v):
    return 1.0 / (1.0 + jnp.exp(-v))


def _gelu_tanh(v):
    c = math.sqrt(2.0 / math.pi)
    return v * (0.5 * (1.0 + jnp.tanh(c * (v + 0.044715 * (v * v * v)))))


def _cmul(ar, ai, xr, xi):
    return ar * xr - ai * xi, ar * xi + ai * xr


def _expand_tables(xb_ref, yc_ref, sk_ref, a_ref, wb_ref, wc_ref, arow_ref, seg_ref):
    f32 = jnp.float32
    bf16 = jnp.bfloat16
    gmask = GROUPS_PER_TILE - 1

    def iota(shape, dim):
        return lax.broadcasted_iota(jnp.int32, shape, dim)

    def onehot(cond):
        return jnp.where(cond, 1.0, 0.0).astype(bf16)

    r = iota((LANES, TILE_STATE), 0)
    c = iota((LANES, TILE_STATE), 1)
    rep_b = onehot(((c & (SSM_STATE - 1)) == (r & (SSM_STATE - 1))) & (((c >> 7) & 1) == (r >> 6)))
    r = iota((2 * LANES, TILE_STATE), 0)
    c = iota((2 * LANES, TILE_STATE), 1)
    own_b = ((r >> 4) & gmask) == (((c >> 8) << 1) | ((c >> 6) & 1))
    r = iota((2 * SSM_GROUP, 2 * LANES), 0)
    c = iota((2 * SSM_GROUP, 2 * LANES), 1)
    rep_c = onehot(((c >> 7) == (r >> 4)) & ((c & (SSM_GROUP - 1)) == (r & (SSM_GROUP - 1))))
    r = iota((TILE_STATE, 2 * LANES), 0)
    c = iota((TILE_STATE, 2 * LANES), 1)
    own_c = (((r >> 8) << 1) | ((r >> 6) & 1)) == ((c >> 4) & gmask)
    r = iota((2 * LANES, 2 * LANES), 0)
    c = iota((2 * LANES, 2 * LANES), 1)
    own_s = ((r >> 4) & gmask) == ((c >> 4) & gmask)
    for t in range(N_TILES):
        wide = jnp.dot(xb_ref[t].astype(bf16), rep_b, preferred_element_type=f32)
        wide = jnp.where(own_b, wide, 0.0).astype(bf16)
        for q in range(N_COLS):
            wb_ref[t, q] = wide[:, q * MXU_N:(q + 1) * MXU_N]
        wide = jnp.dot(yc_ref[t].astype(bf16), rep_c, preferred_element_type=f32)
        wc_ref[t, 0:TILE_STATE, :] = jnp.where(own_c, wide, 0.0).astype(bf16)
        wide = jnp.dot(sk_ref[t].astype(bf16), rep_c, preferred_element_type=f32)
        wc_ref[t, TILE_STATE:TILE_STATE + 2 * LANES, :] = jnp.where(own_s, wide, 0.0).astype(bf16)

    sq = [_cmul(a_ref[0:1, :], a_ref[1:2, :], a_ref[0:1, :], a_ref[1:2, :])]
    while len(sq) < SEG_PAIRS.bit_length():
        sq.append(_cmul(*sq[-1], *sq[-1]))
    a_seg = None
    for k in range(SEG_PAIRS.bit_length()):
        if (SEG_PAIRS >> k) & 1:
            a_seg = sq[k] if a_seg is None else _cmul(*a_seg, *sq[k])
    shape = (SUBLANES, N_STATE)
    arow_ref[0] = jnp.broadcast_to(sq[0][0], shape)
    arow_ref[1] = jnp.broadcast_to(sq[0][1], shape)
    row = iota(shape, 0)
    seg_ref[6] = jnp.broadcast_to(a_seg[0], shape)
    seg_ref[7] = jnp.broadcast_to(a_seg[1], shape)
    for k, shift in enumerate((1, 2, 4)):
        seg_ref[2 * k] = jnp.where(row >= shift, jnp.broadcast_to(a_seg[0], shape), 0.0)
        seg_ref[2 * k + 1] = jnp.where(row >= shift, jnp.broadcast_to(a_seg[1], shape), 0.0)
        a_seg = _cmul(*a_seg, *a_seg)


def _load_weights(pairs, stage_ref, sem_ref):
    chunk_rows, stage_cols = stage_ref.shape[1:]
    rounds = []
    for src_ref, dst_ref in pairs:
        rows, width = src_ref.shape
        assert rows % chunk_rows == 0 and width <= stage_cols
        per_round = stage_cols // width
        starts = list(range(0, rows, chunk_rows))
        for i in range(0, len(starts), per_round):
            rounds.append([(src_ref, dst_ref, r0, width, j * width)
                           for j, r0 in enumerate(starts[i:i + per_round])])

    def copies(k):
        return [pltpu.make_async_copy(src_ref.at[pl.ds(r0, chunk_rows), :],
                                      stage_ref.at[k % 2, :, pl.ds(c0, width)],
                                      sem_ref.at[k % 2])
                for src_ref, _, r0, width, c0 in rounds[k]]

    for cp in copies(0):
        cp.start()
    for k, chunks in enumerate(rounds):
        if k + 1 < len(rounds):
            for cp in copies(k + 1):
                cp.start()
        for cp in copies(k):
            cp.wait()
        for _, dst_ref, r0, width, c0 in chunks:
            for j in range(width // MXU_N):
                dst_ref[j, r0:r0 + chunk_rows, :] = (
                    stage_ref[k % 2, :, c0 + j * MXU_N:c0 + (j + 1) * MXU_N].astype(dst_ref.dtype))


def _block_kernel(*refs):
    last = pl.num_programs(0) - 1

    @pl.when(pl.program_id(0) == 0)
    def _():
        _first_step_setup(*refs)

    @pl.when(pl.program_id(0) < last)
    def _():
        _run_step(True, *refs)

    @pl.when(pl.program_id(0) == last)
    def _():
        _run_step(False, *refs)


def _first_step_setup(x_ref, p_ref, pre_ref, win_hbm, xb_ref, yc_ref, sk_ref, a_ref,
                      wglu_hbm, bglu_ref, wso_hbm, convw_ref, wco_hbm, wo_hbm, post_ref,
                      wple_hbm, wpg_hbm, plen_ref, zero_ref, out_ref,
                      win_ref, wglu_ref, wso_ref, wco_ref, wo_ref, wple_ref, wpg_ref, stage_ref,
                      sem_ref, wb_ref, wc_ref, arow_ref, seg_ref, state_ref, h_ref, un_ref, up_ref,
                      v_ref, sloc_ref, s_ref, z_ref, gs_ref, mg_ref,
                      yp_ref, y_ref, y2_ref, o_ref, x1_ref, x1b_ref, e_ref, mprev_ref, xprev_ref):
    tb = x_ref.shape[0]
    _load_weights([(wo_hbm, wo_ref), (win_hbm, win_ref), (wpg_hbm, wpg_ref),
                   (wple_hbm, wple_ref), (wco_hbm, wco_ref), (wglu_hbm, wglu_ref),
                   (wso_hbm, wso_ref)], stage_ref, sem_ref)
    _expand_tables(xb_ref, yc_ref, sk_ref, a_ref, wb_ref, wc_ref, arow_ref, seg_ref)
    state_ref[...] = jnp.zeros_like(state_ref)
    v_ref[:, tb:tb + SUBLANES, :] = jnp.zeros((CONV_WIDTH // MXU_N, SUBLANES, MXU_N), jnp.float32)
    mprev_ref[...] = jnp.zeros_like(mprev_ref)
    xprev_ref[...] = jnp.zeros_like(xprev_ref)


def _run_step(mixers, x_ref, p_ref, pre_ref, win_hbm, xb_ref, yc_ref, sk_ref, a_ref,
              wglu_hbm, bglu_ref, wso_hbm, convw_ref, wco_hbm, wo_hbm, post_ref,
              wple_hbm, wpg_hbm, plen_ref, zero_ref, out_ref,
              win_ref, wglu_ref, wso_ref, wco_ref, wo_ref, wple_ref, wpg_ref, stage_ref, sem_ref,
              wb_ref, wc_ref, arow_ref, seg_ref, state_ref, h_ref, un_ref, up_ref, v_ref,
              sloc_ref, s_ref, z_ref, gs_ref, mg_ref,
              yp_ref, y_ref, y2_ref, o_ref, x1_ref, x1b_ref, e_ref, mprev_ref, xprev_ref):
    tb = x_ref.shape[0]
    ns = tb // SUBLANES
    npair = ns // 2
    f32 = jnp.float32
    bf16 = jnp.bfloat16

    def cols(n):
        return slice(n * MXU_N, (n + 1) * MXU_N)

    def tile_dot(lhs_ref, w_ref, off, n):
        lhs = jnp.concatenate([lhs_ref[k] for k in range(lhs_ref.shape[0])], axis=1)
        return jnp.dot(lhs, w_ref[off // MXU_N + n], preferred_element_type=f32)

    def load_ktiles(src_ref, rows):
        return jnp.concatenate([src_ref[k, rows, :] for k in range(src_ref.shape[0])], axis=1)

    def store_ktiles(dst_ref, rows, val):
        for k in range(dst_ref.shape[0]):
            dst_ref[k, rows, :] = val[:, k * MXU_N:(k + 1) * MXU_N]

    def finish_project():
        for n in range(D_MODEL // MXU_N):
            o_ref[n] = tile_dot(mprev_ref, wo_ref, 0, n)

    def finish_residual():
        for r in range(0, tb, ROW_TILE):
            rows = slice(r, r + ROW_TILE)
            x1 = xprev_ref[rows, :] + _rms(load_ktiles(o_ref, rows), post_ref[...])
            store_ktiles(x1_ref, rows, x1)
            store_ktiles(x1b_ref, rows, x1.astype(bf16))

    def finish_embed(n):
        eg = tile_dot(x1b_ref, wpg_ref, 0, n)
        ei = jnp.dot(p_ref[...].astype(bf16), wple_ref[n], preferred_element_type=f32)
        e_ref[n] = ei * _sigmoid(eg)

    def finish_store():
        for r in range(0, tb, ROW_TILE):
            rows = slice(r, r + ROW_TILE)
            out_ref[rows, :] = load_ktiles(x1_ref, rows) + _rms(load_ktiles(e_ref, rows), plen_ref[...])

    if not mixers:
        finish_project()
        finish_residual()
        for n in range(D_MODEL // MXU_N):
            finish_embed(n)
        finish_store()
        return

    finish_project()
    finish_residual()
    for r in range(0, tb, ROW_TILE):
        rows = slice(r, r + ROW_TILE)
        xs = x_ref[rows, :]
        store_ktiles(h_ref, pl.ds(pl.multiple_of(r + zero_ref[0], ROW_TILE), ROW_TILE),
                     _rms(xs, pre_ref[...]).astype(bf16))
        xprev_ref[rows, :] = xs

    for n in range(SSM_WIDTH // MXU_N):
        u = tile_dot(h_ref, win_ref, O_U, n)
        for c in range(MXU_N // LANES):
            un_ref[n * (MXU_N // LANES) + c] = u[:, c * LANES:(c + 1) * LANES]
    for t in range(N_TILES):
        for par in range(2):
            for j in range(SUBLANES):
                for mh in range(npair // SUBLANES):
                    up_ref[par, t, pl.ds(SUBLANES * SUBLANES * mh + j, SUBLANES,
                                         stride=SUBLANES), :] = (
                        un_ref[t, pl.ds(j * ns + 2 * SUBLANES * mh + par, SUBLANES, stride=2), :])

    v_ref[:, 0:SUBLANES, :] = v_ref[:, tb:tb + SUBLANES, :]
    for n in range(CONV_WIDTH // MXU_N):
        v_ref[n, SUBLANES:SUBLANES + tb, :] = (
            tile_dot(h_ref, win_ref, O_CC, n) * tile_dot(h_ref, win_ref, O_XC, n))

    srow = lax.broadcasted_iota(jnp.int32, (SUBLANES, LANES), 0)
    row0 = srow == 0

    def scan_tile(t):
        ub = jnp.concatenate([up_ref[0, t], up_ref[1, t]], axis=1).astype(bf16)
        for q in range(N_COLS):
            bu = jnp.dot(ub, wb_ref[t, q],
                         preferred_element_type=f32)
            re = slice(q * 2 * LANES, q * 2 * LANES + LANES)
            im = slice(q * 2 * LANES + LANES, (q + 1) * 2 * LANES)
            sl = slice(t * HALF + q * LANES, t * HALF + (q + 1) * LANES)
            ar = arow_ref[0, :, sl]
            ai = arow_ref[1, :, sl]
            sr = bu[0:SUBLANES, 0:LANES]
            si = bu[0:SUBLANES, LANES:2 * LANES]
            for m in range(1, npair):
                rows = slice(m * SUBLANES, (m + 1) * SUBLANES)
                sloc_ref[t, rows, re] = sr
                sloc_ref[t, rows, im] = si
                mr, mi = _cmul(ar, ai, sr, si)
                sr = mr + bu[rows, 0:LANES]
                si = mi + bu[rows, LANES:2 * LANES]
            gr = jnp.where(row0, state_ref[0, :, sl], pltpu.roll(sr, 1, 0))
            gi = jnp.where(row0, state_ref[1, :, sl], pltpu.roll(si, 1, 0))
            for k, shift in enumerate((1, 2, 4)):
                keep = srow >= shift
                fr, fi = _cmul(seg_ref[2 * k, :, sl], seg_ref[2 * k + 1, :, sl],
                               jnp.where(keep, pltpu.roll(gr, shift, 0), 0.0),
                               jnp.where(keep, pltpu.roll(gi, shift, 0), 0.0))
                gr, gi = gr + fr, gi + fi
            fr, fi = _cmul(seg_ref[6, :, sl], seg_ref[7, :, sl], gr, gi)
            nr, ni = fr + sr, fi + si
            state_ref[0, :, sl] = jnp.broadcast_to(nr[SUBLANES - 1:SUBLANES, :], nr.shape)
            state_ref[1, :, sl] = jnp.broadcast_to(ni[SUBLANES - 1:SUBLANES, :], ni.shape)
            fr, fi = gr, gi
            for m2 in range(npair // 2):
                parts_r, parts_i = [], []
                for m in (2 * m2, 2 * m2 + 1):
                    if m == 0:
                        parts_r.append(gr)
                        parts_i.append(gi)
                        continue
                    rows = slice(m * SUBLANES, (m + 1) * SUBLANES)
                    fr, fi = _cmul(ar, ai, fr, fi)
                    parts_r.append(sloc_ref[t, rows, re] + fr)
                    parts_i.append(sloc_ref[t, rows, im] + fi)
                rows2 = slice(2 * m2 * SUBLANES, (2 * m2 + 2) * SUBLANES)
                s_ref[t, q, rows2, 0:LANES] = jnp.concatenate(parts_r, axis=0).astype(bf16)
                s_ref[t, q, rows2, LANES:2 * LANES] = jnp.concatenate(parts_i, axis=0).astype(bf16)

    scan_tile(0)

    for n in range(CONV_WIDTH // MXU_N):
        bc = tile_dot(h_ref, win_ref, O_BC, n)
        gc = tile_dot(h_ref, win_ref, O_GC, n)
        conv = (convw_ref[0:1, cols(n)] * v_ref[n, SUBLANES - 2:SUBLANES - 2 + tb, :]
                + convw_ref[1:2, cols(n)] * v_ref[n, SUBLANES - 1:SUBLANES - 1 + tb, :]
                + convw_ref[2:3, cols(n)] * v_ref[n, SUBLANES:SUBLANES + tb, :])
        z_ref[n] = (bc * conv * (gc * _sigmoid(gc))).astype(bf16)
        if n == 1:
            scan_tile(1)

    for n in range(D_MODEL // MXU_N):
        finish_embed(n)
        if n == 1:
            scan_tile(2)

    for n in range(SSM_WIDTH // MXU_N):
        gs = tile_dot(h_ref, win_ref, O_GS, n)
        gs_ref[n] = gs * _sigmoid(gs)
    scan_tile(3)

    finish_store()

    def gated_b(n):
        rc = tile_dot(h_ref, win_ref, O_RC, n)
        yb = tile_dot(z_ref, wco_ref, 0, n)
        mg_ref[n] = _sigmoid(rc) * yb

    gated_b(0)
    gated_b(1)

    for t in range(N_TILES):
        lhs = jnp.concatenate(
            [s_ref[t, q] for q in range(N_COLS)]
            + [up_ref[0, t].astype(bf16), up_ref[1, t].astype(bf16)], axis=1)
        yy = jnp.dot(lhs, wc_ref[t], preferred_element_type=f32)
        yp_ref[0, t] = yy[:, 0:LANES]
        yp_ref[1, t] = yy[:, LANES:2 * LANES]
    gated_b(2)
    gated_b(3)
    for t in range(N_TILES):
        for par in range(2):
            for j in range(SUBLANES):
                for mh in range(npair // SUBLANES):
                    y_ref[t, pl.ds(j * ns + 2 * SUBLANES * mh + par, SUBLANES, stride=2), :] = (
                        _gelu_tanh(yp_ref[par, t, pl.ds(SUBLANES * SUBLANES * mh + j, SUBLANES,
                                                        stride=SUBLANES), :]))
    y = jnp.concatenate([y_ref[t] for t in range(N_TILES)], axis=1)
    yb16 = y.astype(bf16)
    for n in range(SSM_WIDTH // MXU_N):
        glu = jnp.dot(yb16, wglu_ref[n], preferred_element_type=f32)
        y2_ref[n] = (y[:, cols(n)] * _sigmoid(glu + bglu_ref[:, cols(n)])
                     * gs_ref[n]).astype(bf16)

    n_out = D_MODEL // MXU_N
    rs = [tile_dot(h_ref, win_ref, O_RS, n) for n in range(2)]
    for n in range(n_out):
        ya = tile_dot(y2_ref, wso_ref, 0, n)
        mprev_ref[n] = (_sigmoid(rs[n]) * ya + mg_ref[n]).astype(bf16)
        if n + 2 < n_out:
            rs.append(tile_dot(h_ref, win_ref, O_RS, n + 2))


def _ssm_params(a_re, a_im, log_step, b_re, b_im, c_re, c_im, d_skip):
    f32 = jnp.float32
    G, P, H, T, GT = SSM_GROUPS, SSM_STATE, SSM_GROUP, N_TILES, GROUPS_PER_TILE
    dt = jnp.exp(log_step.astype(f32))[:, None]
    mag = jnp.exp(a_re * dt)
    abar_r = mag * jnp.cos(a_im * dt)
    abar_i = mag * jnp.sin(a_im * dt)
    den = a_re * a_re + a_im * a_im
    nr = abar_r - 1.0
    fr = (nr * a_re + abar_i * a_im) / den
    fi = (abar_i * a_re - nr * a_im) / den
    bbar_r = fr[..., None] * b_re - fi[..., None] * b_im
    bbar_i = fr[..., None] * b_im + fi[..., None] * b_re
    ab_r, ab_i = _cmul(abar_r[..., None], abar_i[..., None], bbar_r, bbar_i)
    ca_r, ca_i = _cmul(abar_r[:, None, :], abar_i[:, None, :], c_re, c_im)
    caa_r, caa_i = _cmul(abar_r[:, None, :], abar_i[:, None, :], ca_r, ca_i)

    def re_prod(xr, xi, yr, yi):
        yr_t = jnp.swapaxes(yr, 1, 2)[:, None]
        yi_t = jnp.swapaxes(yi, 1, 2)[:, None]
        return jnp.sum(xr[:, :, None, :] * yr_t - xi[:, :, None, :] * yi_t, axis=-1)

    cb_d = (re_prod(c_re, c_im, bbar_r, bbar_i)
            + jnp.eye(H, dtype=f32) * d_skip.reshape(G, H, 1))
    cab = re_prod(ca_r, ca_i, bbar_r, bbar_i)

    xb = jnp.stack([jnp.stack([ab_r, ab_i]), jnp.stack([bbar_r, bbar_i])])
    xb = xb.reshape(2, 2, T, GT, P, H).transpose(2, 0, 3, 5, 1, 4).reshape(T, 2 * LANES, 2 * P)
    yc = jnp.stack([jnp.stack([ca_r, -ca_i]), jnp.stack([caa_r, -caa_i])])
    yc = yc.reshape(2, 2, T, N_COLS, 2, H, P).transpose(2, 3, 1, 4, 6, 0, 5)
    yc = yc.reshape(T, TILE_STATE, 2 * H)
    zero = jnp.zeros_like(cab)
    sk = jnp.stack([jnp.stack([cb_d, cab]), jnp.stack([zero, cb_d])])
    sk = sk.reshape(2, 2, T, GT, H, H).transpose(2, 0, 3, 5, 1, 4).reshape(T, 2 * LANES, 2 * H)
    a = jnp.stack([abar_r.reshape(N_STATE), abar_i.reshape(N_STATE)])
    bf16 = jnp.bfloat16
    return xb.astype(bf16), yc.astype(bf16), sk.astype(bf16), a


def _resident(shape):
    return pl.BlockSpec(shape, lambda i: (0,) * len(shape), pipeline_mode=pl.Buffered(1))


@jax.jit
def kernel(x, p, pre_norm, w_in, a_re, a_im, log_step, b_re, b_im, c_re, c_im, d_skip,
           w_glu, b_glu, w_ssm_out, conv_w, w_conv_out, w_o, post_norm,
           w_ple, w_ple_gate, ple_norm):
    bsz, seqlen, _ = x.shape
    depth = p.shape[0]
    assert bsz == 1 and depth == 1
    bf16 = jnp.bfloat16
    tb = ROW_BLOCK

    xb, yc, sk, a_bar = _ssm_params(
        a_re[0], a_im[0], log_step[0], b_re[0], b_im[0], c_re[0], c_im[0], d_skip[0])
    nb = pl.cdiv(seqlen, tb)
    cur = lambda i: (jnp.minimum(i, nb - 1), 0)
    prev = lambda i: (jnp.maximum(i - 1, 0), 0)
    hbm = pl.BlockSpec(memory_space=pl.ANY)
    operands = [
        (x[0], pl.BlockSpec((tb, D_MODEL), cur)),
        (p[0, 0], pl.BlockSpec((tb, PLE_DIM), prev)),
        (pre_norm, _resident((1, D_MODEL))),
        (w_in.reshape(D_MODEL, IN_WIDTH), hbm),
        (xb, _resident(xb.shape)),
        (yc, _resident(yc.shape)),
        (sk, _resident(sk.shape)),
        (a_bar, _resident(a_bar.shape)),
        (w_glu.reshape(SSM_WIDTH, SSM_WIDTH), hbm),
        (b_glu, _resident((1, SSM_WIDTH))),
        (w_ssm_out.reshape(SSM_WIDTH, D_MODEL), hbm),
        (conv_w[0], _resident((CONV_K, CONV_WIDTH))),
        (w_conv_out.reshape(CONV_WIDTH, D_MODEL), hbm),
        (w_o.reshape(D_MODEL, D_MODEL), hbm),
        (post_norm, _resident((1, D_MODEL))),
        (w_ple.reshape(PLE_DIM, D_MODEL), hbm),
        (w_ple_gate.reshape(D_MODEL, D_MODEL), hbm),
        (ple_norm, _resident((1, D_MODEL))),
        (jnp.zeros((1,), jnp.int32), pl.BlockSpec(memory_space=pltpu.SMEM)),
    ]
    out = pl.pallas_call(
        _block_kernel,
        grid=(nb + 1,),
        in_specs=[spec for _, spec in operands],
        out_specs=pl.BlockSpec((tb, D_MODEL), prev),
        out_shape=jax.ShapeDtypeStruct((seqlen, D_MODEL), jnp.float32),
        scratch_shapes=[
            pltpu.VMEM((IN_WIDTH // MXU_N, D_MODEL, MXU_N), bf16),
            pltpu.VMEM((SSM_WIDTH // MXU_N, SSM_WIDTH, MXU_N), bf16),
            pltpu.VMEM((D_MODEL // MXU_N, SSM_WIDTH, MXU_N), bf16),
            pltpu.VMEM((D_MODEL // MXU_N, CONV_WIDTH, MXU_N), bf16),
            pltpu.VMEM((D_MODEL // MXU_N, D_MODEL, MXU_N), bf16),
            pltpu.VMEM((D_MODEL // MXU_N, PLE_DIM, MXU_N), bf16),
            pltpu.VMEM((D_MODEL // MXU_N, D_MODEL, MXU_N), bf16),
            pltpu.VMEM((2, STAGE_ROWS, IN_WIDTH), jnp.float32),
            pltpu.SemaphoreType.DMA((2,)),
            pltpu.VMEM((N_TILES, N_COLS, 2 * LANES, MXU_N), jnp.bfloat16),
            pltpu.VMEM((N_TILES, TILE_STATE + 2 * LANES, 2 * LANES), jnp.bfloat16),
            pltpu.VMEM((2, SUBLANES, N_STATE), jnp.float32),
            pltpu.VMEM((8, SUBLANES, N_STATE), jnp.float32),
            pltpu.VMEM((2, SUBLANES, N_STATE), jnp.float32),
            pltpu.VMEM((D_MODEL // MXU_N, tb, MXU_N), jnp.bfloat16),
            pltpu.VMEM((N_TILES, tb, LANES), jnp.float32),
            pltpu.VMEM((2, N_TILES, tb // 2, LANES), jnp.float32),
            pltpu.VMEM((CONV_WIDTH // MXU_N, tb + SUBLANES, MXU_N), jnp.float32),
            pltpu.VMEM((N_TILES, tb // 2, TILE_STATE), jnp.float32),
            pltpu.VMEM((N_TILES, N_COLS, tb // 2, MXU_N), jnp.bfloat16),
            pltpu.VMEM((CONV_WIDTH // MXU_N, tb, MXU_N), jnp.bfloat16),
            pltpu.VMEM((SSM_WIDTH // MXU_N, tb, MXU_N), jnp.float32),
            pltpu.VMEM((D_MODEL // MXU_N, tb, MXU_N), jnp.float32),
            pltpu.VMEM((2, N_TILES, tb // 2, LANES), jnp.float32),
            pltpu.VMEM((N_TILES, tb, LANES), jnp.float32),
            pltpu.VMEM((SSM_WIDTH // MXU_N, tb, MXU_N), jnp.bfloat16),
            pltpu.VMEM((D_MODEL // MXU_N, tb, MXU_N), jnp.float32),
            pltpu.VMEM((D_MODEL // MXU_N, tb, MXU_N), jnp.float32),
            pltpu.VMEM((D_MODEL // MXU_N, tb, MXU_N), jnp.bfloat16),
            pltpu.VMEM((D_MODEL // MXU_N, tb, MXU_N), jnp.float32),
            pltpu.VMEM((D_MODEL // MXU_N, tb, MXU_N), jnp.bfloat16),
            pltpu.VMEM((tb, D_MODEL), jnp.float32),
        ],
        compiler_params=pltpu.CompilerParams(
            dimension_semantics=("arbitrary",),
            vmem_limit_bytes=VMEM_LIMIT_BYTES),
        name="hybrid_s5_shortconv_block",
    )(*[a for a, _ in operands])
    return out[None]
```
